```python
import numpy as np
import jax
import jax.numpy as jnp
from jax import lax

D_MODEL = 2048
BATCH = 1
SEQ = 8192
DEPTH = 4

CHUNK = 64
EPS = 1e-6

HG_HEADS = 8
HG_DK = 128
HG_DV = 128
HG_WIDTH = HG_HEADS * HG_DV
HG_COLS = (HG_HEADS * HG_DK, HG_HEADS * HG_DK, HG_WIDTH, HG_WIDTH)

RW_HEADS = 16
RW_N = 64
RW_WIDTH = RW_HEADS * RW_N
RW_DECAY_LORA = 64
RW_AAA_LORA = 64
RW_GATE_LORA = 160
RW_GN_EPS = 64e-5
RW_COLS = (RW_WIDTH, RW_WIDTH, RW_WIDTH, RW_DECAY_LORA, RW_AAA_LORA, RW_GATE_LORA)

ML_HEADS = 4
ML_DQK = 128
ML_DV = 256
ML_GATE_CAP = 15.0
ML_COLS = (ML_HEADS * ML_DQK, ML_HEADS * ML_DQK, ML_HEADS * ML_DV, ML_HEADS, ML_HEADS, ML_HEADS * ML_DV)

MB_HEADS = 16
MB_HEADDIM = 64
MB_D_INNER = MB_HEADS * MB_HEADDIM
MB_D_STATE = 128
MB_GROUPS = 2
MB_CONV = 4
MB_XBC = MB_D_INNER + 2 * MB_GROUPS * MB_D_STATE
MB_COLS = (MB_D_INNER, MB_XBC, MB_HEADS)

E_IN_EVEN = sum(HG_COLS) + sum(RW_COLS)
E_IN_ODD = sum(ML_COLS) + sum(MB_COLS)
MIX_WIDTH_EVEN = HG_WIDTH + RW_WIDTH
MIX_WIDTH_ODD = ML_HEADS * ML_DV + MB_D_INNER
D_FF = -((-8 * D_MODEL) // (3 * 256)) * 256

kernel_name = 'hybrid_hgrn2_rwkv7_mlstm_mamba2_trunk'


def rms_norm(x, g):
    xf = x.astype(jnp.float32)
    y = xf * lax.rsqrt(jnp.mean(xf * xf, axis=-1, keepdims=True) + EPS)
    return (y * g).astype(x.dtype)


def head_rmsnorm(t, n_heads, g):
    b, s, w = t.shape
    th = t.reshape(b, s, n_heads, w // n_heads)
    th = th * lax.rsqrt(jnp.mean(th * th, axis=-1, keepdims=True) + EPS)
    return th.reshape(b, s, w) * g


def head_layernorm(t, w, bias, eps):
    mu = jnp.mean(t, axis=-1, keepdims=True)
    var = jnp.mean(jnp.square(t - mu), axis=-1, keepdims=True)
    y = (t - mu) * lax.rsqrt(var + eps)
    return y.reshape(t.shape[0], t.shape[1], -1) * w + bias


def split_cols(p, sizes):
    return jnp.split(p, np.cumsum(sizes)[:-1].tolist(), axis=-1)


def token_shift(p):
    return jnp.pad(p, ((0, 0), (1, 0), (0, 0)))[:, :-1]


def causal_dwconv(x, w, bias):
    y = lax.conv_general_dilated(x, w.astype(x.dtype)[:, None, :], window_strides=(1,),
                                 padding=[(w.shape[0] - 1, 0)],
                                 dimension_numbers=('NWC', 'WIO', 'NWC'),
                                 feature_group_count=x.shape[-1])
    return y + bias


def causal_mask():
    return jnp.tril(jnp.ones((CHUNK, CHUNK), dtype=bool))


def chunk_seq(t, n_feat=1):
    b, s = t.shape[:2]
    t = t.reshape(b, s // CHUNK, CHUNK, *t.shape[2:])
    t = jnp.moveaxis(t, 2, t.ndim - 1 - n_feat)
    return jnp.moveaxis(t, 1, 0)


def unchunk_seq(t):
    t = jnp.moveaxis(t, 0, 1)
    t = jnp.moveaxis(t, -2, 2)
    b, nc, l = t.shape[:3]
    return t.reshape(b, nc * l, -1)


def hgrn2_chunked(qc, kc, vc, gc):
    mask = causal_mask()

    def step(state, inp):
        q_, k_, v_, g_ = inp
        bcum = jnp.cumsum(g_, axis=-2)
        rel = jnp.where(mask[:, :, None], bcum[..., :, None, :] - bcum[..., None, :, :], -jnp.inf)
        scores = jnp.einsum('bhtd,bhsd,bhtsd->bhts', q_, k_, jnp.exp(rel))
        o = (jnp.einsum('bhts,bhsv->bhtv', scores, v_)
             + jnp.einsum('bhtd,bhdv->bhtv', q_ * jnp.exp(bcum), state))
        b_last = bcum[..., -1:, :]
        state = (jnp.exp(b_last[..., 0, :])[..., None] * state
                 + jnp.einsum('bhsd,bhsv->bhdv', k_ * jnp.exp(b_last - bcum), v_))
        return state, o

    _, b, n_h, _, dk = qc.shape
    state0 = jnp.zeros((b, n_h, dk, vc.shape[-1]), qc.dtype)
    _, out = lax.scan(step, state0, (qc, kc, vc, gc))
    return out


def rwkv7_scan(r, w, k, v, kk, a):
    b, _, n_h, n = r.shape

    def step(state, inp):
        rt, wt, kt, vt, kkt, at = inp
        sa = jnp.einsum('bhvk,bhk->bhv', state, -kkt)
        state = (state * wt[:, :, None, :] + sa[..., :, None] * (kkt * at)[:, :, None, :]
                 + vt[..., :, None] * kt[:, :, None, :])
        return state, jnp.einsum('bhvk,bhk->bhv', state, rt)

    xs = tuple(jnp.moveaxis(t, 1, 0) for t in (r, w, k, v, kk, a))
    state0 = jnp.zeros((b, n_h, n, n), r.dtype)
    _, out = lax.scan(step, state0, xs)
    return jnp.moveaxis(out, 0, 1)


def mlstm_chunked(qc, kc, vc, ic, fc):
    mask = causal_mask()

    def step(carry, inp):
        c_mat, n_vec, m = carry
        q_, k_, v_, i_, f_ = inp
        bcum = jnp.cumsum(f_, axis=-1)
        dmat = jnp.where(mask, bcum[..., :, None] - bcum[..., None, :] + i_[..., None, :], -jnp.inf)
        inter = bcum + m[..., None]
        m_t = jnp.maximum(inter, jnp.max(dmat, axis=-1))
        w_inter = jnp.exp(inter - m_t)
        pmat = jnp.exp(dmat - m_t[..., None]) * jnp.einsum('bhtd,bhsd->bhts', q_, k_)
        num = (jnp.einsum('bhts,bhsv->bhtv', pmat, v_)
               + w_inter[..., None] * jnp.einsum('bhtd,bhdv->bhtv', q_, c_mat))
        den = jnp.sum(pmat, axis=-1) + w_inter * jnp.einsum('bhtd,bhd->bht', q_, n_vec)
        h_out = num / jnp.maximum(jnp.abs(den), jnp.exp(-m_t))[..., None]
        b_last = bcum[..., -1]
        src = b_last[..., None] - bcum + i_
        m_new = jnp.maximum(b_last + m, jnp.max(src, axis=-1))
        ws = jnp.exp(src - m_new[..., None])
        carry_decay = jnp.exp(b_last + m - m_new)
        c_mat = carry_decay[..., None, None] * c_mat + jnp.einsum('bhs,bhsd,bhsv->bhdv', ws, k_, v_)
        n_vec = carry_decay[..., None] * n_vec + jnp.einsum('bhs,bhsd->bhd', ws, k_)
        return (c_mat, n_vec, m_new), h_out

    _, b, n_h, _, dqk = qc.shape
    dv = vc.shape[-1]
    init = (jnp.zeros((b, n_h, dqk, dv), qc.dtype), jnp.zeros((b, n_h, dqk), qc.dtype),
            jnp.zeros((b, n_h), qc.dtype))
    _, out = lax.scan(step, init, (qc, kc, vc, ic, fc))
    return out


def ssd_chunked(xc, dtc, dac, bc, cc):
    mask = causal_mask()

    def step(state, inp):
        x_, dt_, da_, b_, c_ = inp
        a = jnp.cumsum(da_, axis=-1)
        seg = jnp.exp(jnp.where(mask, a[..., :, None] - a[..., None, :], -jnp.inf))
        cb = jnp.einsum('bgtn,bgsn->bgts', c_, b_)
        scores = seg * cb[:, :, None] * dt_[..., None, :]
        y = (jnp.einsum('bgrts,bgrsp->bgrtp', scores, x_)
             + jnp.exp(a)[..., None] * jnp.einsum('bgtn,bgrpn->bgrtp', c_, state))
        a_last = a[..., -1:]
        ws = jnp.exp(a_last - a) * dt_
        state = (jnp.exp(a_last)[..., None] * state
                 + jnp.einsum('bgrs,bgrsp,bgsn->bgrpn', ws, x_, b_))
        return state, y

    _, b, g, r, _, p = xc.shape
    state0 = jnp.zeros((b, g, r, p, bc.shape[-1]), xc.dtype)
    _, out = lax.scan(step, state0, (xc, dtc, dac, bc, cc))
    return out


def hgrn2_rwkv7_mixer(u, layer, w_in, w_out, hg_lb_table, hg_norm, rw_mu, rw_w0, rw_w2, rw_a0, rw_a2,
                      rw_g2, rw_k_k, rw_k_a, rw_r_k, rw_ln_w, rw_ln_b):
    f32 = jnp.float32
    b, s, _ = u.shape
    p = (u @ w_in).astype(f32)
    n_hg = sum(HG_COLS)
    p_hg, p_rw = p[..., :n_hg], p[..., n_hg:]

    q, f, i, g = split_cols(p_hg, HG_COLS)
    lb = jnp.cumsum(jax.nn.softmax(hg_lb_table.astype(f32), axis=0), axis=0)
    lb = (lb - lb[0])[layer]
    log_f = jnp.logaddexp(jnp.log(lb), jnp.log1p(-lb) + jax.nn.log_sigmoid(f))
    k = (1.0 - lb) * jax.nn.sigmoid(-f)
    q = jax.nn.silu(q)
    o_hg = hgrn2_chunked(chunk_seq(q.reshape(b, s, HG_HEADS, HG_DK)), chunk_seq(k.reshape(b, s, HG_HEADS, HG_DK)),
                         chunk_seq(i.reshape(b, s, HG_HEADS, HG_DV)), chunk_seq(log_f.reshape(b, s, HG_HEADS, HG_DK)))
    o_hg = head_rmsnorm(unchunk_seq(o_hg), HG_HEADS, hg_norm) * jax.nn.sigmoid(g)

    p_rw = p_rw + (token_shift(p_rw) - p_rw) * rw_mu
    r, k, v, wl, al, gl = split_cols(p_rw, RW_COLS)
    w_log = -jax.nn.softplus(-(rw_w0 + jnp.tanh(wl) @ rw_w2)) - 0.5
    decay = jnp.exp(-jnp.exp(w_log))
    a = jax.nn.sigmoid(rw_a0 + al @ rw_a2)
    gate = jax.nn.sigmoid(gl) @ rw_g2
    heads = lambda t: t.reshape(b, s, RW_HEADS, RW_N)
    kk = heads(k * rw_k_k)
    kk = kk / jnp.maximum(jnp.sqrt(jnp.sum(kk * kk, axis=-1, keepdims=True)), 1e-12)
    k = k * (1.0 + (a - 1.0) * rw_k_a)
    rh, kh, vh = heads(r), heads(k), heads(v)
    o_rw = rwkv7_scan(rh, heads(decay), kh, vh, kk, heads(a))
    o_rw = head_layernorm(o_rw, rw_ln_w, rw_ln_b, RW_GN_EPS)
    bonus = (jnp.sum(rh * kh * rw_r_k, axis=-1, keepdims=True) * vh).reshape(b, s, RW_WIDTH)
    o_rw = (o_rw + bonus) * gate

    return jnp.concatenate([o_hg, o_rw], axis=-1).astype(u.dtype) @ w_out


def mlstm_mamba2_mixer(u, w_in, w_out, ml_i_bias, ml_f_bias, ml_norm, mb_conv_w, mb_conv_b, mb_dt_bias,
                       mb_A_log, mb_D, mb_norm):
    f32 = jnp.float32
    b, s, _ = u.shape
    p = (u @ w_in).astype(f32)
    n_ml = sum(ML_COLS)

    q, k, v, ig, fg, og = split_cols(p[..., :n_ml], ML_COLS)
    softcap = lambda t: ML_GATE_CAP * jnp.tanh(t / ML_GATE_CAP)
    i_pre = softcap(ig + ml_i_bias)
    log_f = jax.nn.log_sigmoid(softcap(fg + ml_f_bias))
    q = q * (ML_DQK ** -0.5)
    o_ml = mlstm_chunked(chunk_seq(q.reshape(b, s, ML_HEADS, ML_DQK)), chunk_seq(k.reshape(b, s, ML_HEADS, ML_DQK)),
                         chunk_seq(v.reshape(b, s, ML_HEADS, ML_DV)), chunk_seq(i_pre, 0), chunk_seq(log_f, 0))
    o_ml = head_rmsnorm(unchunk_seq(o_ml), ML_HEADS, ml_norm) * jax.nn.sigmoid(og)

    z, xbc, dt = split_cols(p[..., n_ml:], MB_COLS)
    xbc = jax.nn.silu(causal_dwconv(xbc, mb_conv_w, mb_conv_b))
    xm, bm, cm = split_cols(xbc, (MB_D_INNER, MB_GROUPS * MB_D_STATE, MB_GROUPS * MB_D_STATE))
    dt = jax.nn.softplus(dt + mb_dt_bias)
    d_a = -jnp.exp(mb_A_log.astype(f32)) * dt
    rep = MB_HEADS // MB_GROUPS
    y = ssd_chunked(chunk_seq(xm.reshape(b, s, MB_GROUPS, rep, MB_HEADDIM)),
                    chunk_seq(dt.reshape(b, s, MB_GROUPS, rep), 0), chunk_seq(d_a.reshape(b, s, MB_GROUPS, rep), 0),
                    chunk_seq(bm.reshape(b, s, MB_GROUPS, MB_D_STATE)), chunk_seq(cm.reshape(b, s, MB_GROUPS, MB_D_STATE)))
    y = unchunk_seq(y) + xm * jnp.repeat(mb_D, MB_HEADDIM)
    y = head_rmsnorm(y * jax.nn.silu(z), MB_GROUPS, mb_norm)

    return jnp.concatenate([o_ml, y], axis=-1).astype(u.dtype) @ w_out


def swiglu(h, w_up, w_down):
    gate, up = jnp.split(h @ w_up, 2, axis=-1)
    return (jax.nn.silu(gate) * up) @ w_down


def setup_inputs(seed: int = 0) -> dict:
    key = jax.random.key(seed)
    keys = iter(jax.random.split(key, 64))
    nrm = lambda shape, scale: scale * jax.random.normal(next(keys), shape, jnp.float32)
    uni = lambda shape, lo, hi: jax.random.uniform(next(keys), shape, jnp.float32, lo, hi)
    ne, no = (DEPTH + 1) // 2, DEPTH // 2
    dt0 = jnp.exp(uni((no, MB_HEADS), float(np.log(1e-3)), float(np.log(1e-1))))
    return {
        'x': nrm((BATCH, SEQ, D_MODEL), 1.0),
        'norm_mix': 1.0 + nrm((DEPTH, D_MODEL), 0.05),
        'norm_ffn': 1.0 + nrm((DEPTH, D_MODEL), 0.05),
        'norm_final': 1.0 + nrm((D_MODEL,), 0.05),
        'w_in_even': nrm((ne, D_MODEL, E_IN_EVEN), D_MODEL ** -0.5),
        'w_out_even': nrm((ne, MIX_WIDTH_EVEN, D_MODEL), 0.5 * MIX_WIDTH_EVEN ** -0.5),
        'hg_lb_table': nrm((DEPTH, HG_HEADS * HG_DK), 0.5),
        'hg_norm': 1.0 + nrm((ne, HG_WIDTH), 0.05),
        'rw_mu': uni((ne, sum(RW_COLS)), 0.0, 1.0),
        'rw_w0': -1.0 + nrm((ne, RW_WIDTH), 0.5),
        'rw_w2': nrm((ne, RW_DECAY_LORA, RW_WIDTH), 0.5 * RW_DECAY_LORA ** -0.5),
        'rw_a0': nrm((ne, RW_WIDTH), 0.1),
        'rw_a2': nrm((ne, RW_AAA_LORA, RW_WIDTH), RW_AAA_LORA ** -0.5),
        'rw_g2': nrm((ne, RW_GATE_LORA, RW_WIDTH), RW_GATE_LORA ** -0.5),
        'rw_k_k': 0.85 + nrm((ne, RW_WIDTH), 0.05),
        'rw_k_a': 1.0 + nrm((ne, RW_WIDTH), 0.05),
        'rw_r_k': nrm((ne, RW_HEADS, RW_N), 0.1),
        'rw_ln_w': 1.0 + nrm((ne, RW_WIDTH), 0.05),
        'rw_ln_b': nrm((ne, RW_WIDTH), 0.02),
        'w_in_odd': nrm((no, D_MODEL, E_IN_ODD), D_MODEL ** -0.5),
        'w_out_odd': nrm((no, MIX_WIDTH_ODD, D_MODEL), 0.5 * MIX_WIDTH_ODD ** -0.5),
        'ml_i_bias': -2.0 + nrm((no, ML_HEADS), 0.5),
        'ml_f_bias': uni((no, ML_HEADS), 3.0, 6.0),
        'ml_norm': 1.0 + nrm((no, ML_HEADS * ML_DV), 0.05),
        'mb_conv_w': nrm((no, MB_CONV, MB_XBC), 0.5 * MB_CONV ** -0.5),
        'mb_conv_b': nrm((no, MB_XBC), 0.02),
        'mb_dt_bias': dt0 + jnp.log(-jnp.expm1(-dt0)),
        'mb_A_log': jnp.log(uni((no, MB_HEADS), 1.0, 16.0)),
        'mb_D': 1.0 + nrm((no, MB_HEADS), 0.1),
        'mb_norm': 1.0 + nrm((no, MB_D_INNER), 0.05),
        'ffn_w_up': nrm((DEPTH, D_MODEL, 2 * D_FF), D_MODEL ** -0.5),
        'ffn_w_down': nrm((DEPTH, D_FF, D_MODEL), 0.5 * D_FF ** -0.5),
    }


def reference(x, norm_mix, norm_ffn, norm_final, w_in_even, w_out_even, hg_lb_table, hg_norm, rw_mu, rw_w0,
              rw_w2, rw_a0, rw_a2, rw_g2, rw_k_k, rw_k_a, rw_r_k, rw_ln_w, rw_ln_b, w_in_odd, w_out_odd,
              ml_i_bias, ml_f_bias, ml_norm, mb_conv_w, mb_conv_b, mb_dt_bias, mb_A_log, mb_D, mb_norm,
              ffn_w_up, ffn_w_down):
    h = x
    for layer in range(DEPTH):
        u = rms_norm(h, norm_mix[layer])
        j = layer // 2
        if layer % 2 == 0:
            m = hgrn2_rwkv7_mixer(u, layer, w_in_even[j], w_out_even[j], hg_lb_table, hg_norm[j], rw_mu[j],
                                  rw_w0[j], rw_w2[j], rw_a0[j], rw_a2[j], rw_g2[j], rw_k_k[j], rw_k_a[j],
                                  rw_r_k[j], rw_ln_w[j], rw_ln_b[j])
        else:
            m = mlstm_mamba2_mixer(u, w_in_odd[j], w_out_odd[j], ml_i_bias[j], ml_f_bias[j], ml_norm[j],
                                   mb_conv_w[j], mb_conv_b[j], mb_dt_bias[j], mb_A_log[j], mb_D[j], mb_norm[j])
        h = h + m.astype(h.dtype)
        h = h + swiglu(rms_norm(h, norm_ffn[layer]), ffn_w_up[layer], ffn_w_down[layer]).astype(h.dtype)
    return rms_norm(h, norm_final)
```

```python
import functools

import jax
import jax.numpy as jnp
from jax import lax
from jax.experimental import pallas as pl
from jax.experimental.pallas import tpu as pltpu

F32 = jnp.float32
BF16 = jnp.bfloat16
HI = lax.Precision.HIGHEST

D_MODEL = 2048
DEPTH = 4
CHUNK = 64
EPS = 1e-6
LANES = 128
SUB = 16

HG_HEADS, HG_D = 8, 128
RW_HEADS, RW_N, RW_W = 16, 64, 1024
RW_GN_EPS = 64e-5
ML_HEADS, ML_DQK, ML_DV = 4, 128, 256
ML_CAP = 15.0
MB_HEADS, MB_P, MB_N, MB_GROUPS, MB_CONV = 16, 64, 128, 2, 4
MB_DI = MB_HEADS * MB_P
D_FF = 5632

NP_EVEN = 7680
NP_ODD = 6144
RW_TAIL = 7168
OD_SMALL = 4096
OD_XBC = 4608

TM = 512
TN = 512
VMEM_LIMIT = 56 * 1024 * 1024


def _cparams(sem):
    return pltpu.CompilerParams(dimension_semantics=sem, vmem_limit_bytes=VMEM_LIMIT)


def _dot(a, b, prec=None):
    return jnp.dot(a, b, preferred_element_type=F32, precision=prec)


def _dot_nt(a, b, prec=None):
    return lax.dot_general(a, b, (((1,), (1,)), ((), ())), preferred_element_type=F32, precision=prec)


def _dot_tn(a, b, prec=None):
    return lax.dot_general(a, b, (((0,), (0,)), ((), ())), preferred_element_type=F32, precision=prec)


def _sigmoid(x):
    return 1.0 / (1.0 + jnp.exp(-x))


def _log_sigmoid(x):
    return jnp.minimum(x, 0.0) - jnp.log1p(jnp.exp(-jnp.abs(x)))


def _softplus(x):
    return jnp.maximum(x, 0.0) + jnp.log1p(jnp.exp(-jnp.abs(x)))


def _iota(shape, dim):
    return lax.broadcasted_iota(jnp.int32, shape, dim)


def _tril(n, strict=False):
    r, c = _iota((n, n), 0), _iota((n, n), 1)
    return (r > c) if strict else (r >= c)


def _rms_rows(x, g):
    return (x * lax.rsqrt(jnp.mean(x * x, axis=-1, keepdims=True) + EPS)) * g


def _norm_mm_kernel(x_ref, g_ref, w_ref, o_ref, xn_ref):
    @pl.when(pl.program_id(1) == 0)
    def _():
        xn_ref[...] = _rms_rows(x_ref[...], g_ref[...]).astype(BF16)

    o_ref[...] = _dot(xn_ref[...], w_ref[...])


def _norm_matmul(x, g, w):
    s, k = x.shape
    n = w.shape[1]
    return pl.pallas_call(
        _norm_mm_kernel,
        grid=(s // TM, n // TN),
        in_specs=[pl.BlockSpec((TM, k), lambda i, j: (i, 0)),
                  pl.BlockSpec((1, k), lambda i, j: (0, 0)),
                  pl.BlockSpec((k, TN), lambda i, j: (0, j))],
        out_specs=pl.BlockSpec((TM, TN), lambda i, j: (i, j)),
        out_shape=jax.ShapeDtypeStruct((s, n), F32),
        scratch_shapes=[pltpu.VMEM((TM, k), BF16)],
        compiler_params=_cparams(("parallel", "arbitrary")),
        name="norm_in_proj",
    )(x, g.reshape(1, k), w)


def _norm_swiglu_kernel(x_ref, g_ref, wg_ref, wu_ref, o_ref, xn_ref):
    @pl.when(pl.program_id(1) == 0)
    def _():
        xn_ref[...] = _rms_rows(x_ref[...], g_ref[...]).astype(BF16)

    xn = xn_ref[...]
    gate = _dot(xn, wg_ref[...])
    up = _dot(xn, wu_ref[...])
    o_ref[...] = (gate * _sigmoid(gate) * up).astype(BF16)


def _norm_swiglu(x, g, w_up):
    s, k = x.shape
    nj = D_FF // TN
    return pl.pallas_call(
        _norm_swiglu_kernel,
        grid=(s // TM, nj),
        in_specs=[pl.BlockSpec((TM, k), lambda i, j: (i, 0)),
                  pl.BlockSpec((1, k), lambda i, j: (0, 0)),
                  pl.BlockSpec((k, TN), lambda i, j: (0, j)),
                  pl.BlockSpec((k, TN), lambda i, j: (0, j + nj))],
        out_specs=pl.BlockSpec((TM, TN), lambda i, j: (i, j)),
        out_shape=jax.ShapeDtypeStruct((s, D_FF), BF16),
        scratch_shapes=[pltpu.VMEM((TM, k), BF16)],
        compiler_params=_cparams(("parallel", "arbitrary")),
        name="norm_ffn_up",
    )(x, g.reshape(1, k), w_up, w_up)


def _mm_res_kernel(x_ref, w_ref, r_ref, o_ref):
    o_ref[...] = r_ref[...] + _dot(x_ref[...], w_ref[...])


def _matmul_residual(x, w, res):
    s, k = x.shape
    n = w.shape[1]
    return pl.pallas_call(
        _mm_res_kernel,
        grid=(s // TM, n // TN),
        in_specs=[pl.BlockSpec((TM, k), lambda i, j: (i, 0)),
                  pl.BlockSpec((k, TN), lambda i, j: (0, j)),
                  pl.BlockSpec((TM, TN), lambda i, j: (i, j))],
        out_specs=pl.BlockSpec((TM, TN), lambda i, j: (i, j)),
        out_shape=jax.ShapeDtypeStruct((s, n), F32),
        compiler_params=_cparams(("parallel", "parallel")),
        name="proj_residual",
    )(x, w, res)


def _rmsnorm_kernel(x_ref, g_ref, o_ref):
    o_ref[...] = _rms_rows(x_ref[...], g_ref[...])


def _rmsnorm(x, g):
    s, k = x.shape
    return pl.pallas_call(
        _rmsnorm_kernel,
        grid=(s // TM,),
        in_specs=[pl.BlockSpec((TM, k), lambda i: (i, 0)), pl.BlockSpec((1, k), lambda i: (0, 0))],
        out_specs=pl.BlockSpec((TM, k), lambda i: (i, 0)),
        out_shape=jax.ShapeDtypeStruct((s, k), F32),
        compiler_params=_cparams(("parallel",)),
        name="final_norm",
    )(x, g.reshape(1, k))


def _hgrn2_kernel(q_ref, f_ref, v_ref, g_ref, lbt_ref, nw_ref, o_ref, st_ref, *, layer, nchunk):
    @pl.when(pl.program_id(1) == 0)
    def _():
        st_ref[...] = jnp.zeros_like(st_ref)

    t = lbt_ref[...]
    e = jnp.exp(t - jnp.max(t, axis=0, keepdims=True))
    sm = e / jnp.sum(e, axis=0, keepdims=True)
    lb = jnp.zeros((1, HG_D), F32)
    for i in range(1, layer + 1):
        lb = lb + sm[i:i + 1, :]
    log_lb = jnp.log(lb)
    log_1m = jnp.log1p(-lb)
    nw = nw_ref[...]

    tril = _tril(CHUNK).astype(F32)
    rowid = _iota((SUB, HG_D), 0)
    nsub = CHUNK // SUB

    def chunk(c, carry):
        rows = pl.ds(pl.multiple_of(c * CHUNK, CHUNK), CHUNK)
        qp, fp, v = q_ref[rows, :], f_ref[rows, :], v_ref[rows, :]
        b2 = log_1m + _log_sigmoid(fp)
        log_f = jnp.maximum(log_lb, b2) + jnp.log1p(jnp.exp(-jnp.abs(log_lb - b2)))
        k = (1.0 - lb) * _sigmoid(-fp)
        q = qp * _sigmoid(qp)
        bc = _dot(tril, log_f, HI)
        st = st_ref[...]
        o_inter = _dot_nt(q * jnp.exp(bc), st)
        outs = []
        for blk in range(nsub):
            lo = blk * SUB
            b_i, q_i, k_i, v_i = bc[lo:lo + SUB], q[lo:lo + SUB], k[lo:lo + SUB], v[lo:lo + SUB]
            acc = o_inter[lo:lo + SUB]
            for s in range(SUB):
                d = b_i - b_i[s:s + 1, :]
                if s > 0:
                    d = jnp.where(rowid >= s, d, -jnp.inf)
                col = jnp.sum(q_i * k_i[s:s + 1, :] * jnp.exp(d), axis=1, keepdims=True)
                acc = acc + col * v_i[s:s + 1, :]
            if blk > 0:
                ref_b = bc[lo - 1:lo, :]
                qx = q_i * jnp.exp(b_i - ref_b)
                kx = k[0:lo] * jnp.exp(ref_b - bc[0:lo])
                acc = acc + _dot(_dot_nt(qx, kx), v[0:lo])
            outs.append(acc)
        o = jnp.concatenate(outs, axis=0)
        b_last = bc[CHUNK - 1:CHUNK, :]
        st_ref[...] = jnp.exp(b_last) * st + _dot_tn(v, k * jnp.exp(b_last - bc))
        y = o * lax.rsqrt(jnp.mean(o * o, axis=-1, keepdims=True) + EPS) * nw
        o_ref[rows, :] = (y * _sigmoid(g_ref[rows, :])).astype(BF16)
        return carry

    lax.fori_loop(0, nchunk, chunk, 0)


def _hgrn2(p, lb_table, norm_w, layer, rb):
    s = p.shape[0]
    nb = HG_HEADS
    col = lambda off: pl.BlockSpec((rb, HG_D), lambda h, r, off=off: (r, off + h))
    return pl.pallas_call(
        functools.partial(_hgrn2_kernel, layer=layer, nchunk=rb // CHUNK),
        grid=(HG_HEADS, s // rb),
        in_specs=[col(0), col(nb), col(2 * nb), col(3 * nb),
                  pl.BlockSpec((DEPTH, HG_D), lambda h, r: (0, h)),
                  pl.BlockSpec((1, HG_D), lambda h, r: (0, h))],
        out_specs=pl.BlockSpec((rb, HG_D), lambda h, r: (r, h)),
        out_shape=jax.ShapeDtypeStruct((s, HG_HEADS * HG_D), BF16),
        scratch_shapes=[pltpu.VMEM((HG_D, HG_D), F32)],
        compiler_params=_cparams(("parallel", "arbitrary")),
        name="hgrn2",
    )(p, p, p, p, lb_table, norm_w.reshape(1, -1))


def _shift_lerp(x, prev_row, mu):
    rolled = pltpu.roll(x, 1, 0)
    shifted = jnp.where(_iota(x.shape, 0) == 0, prev_row, rolled)
    return x + (shifted - x) * mu


def _neumann_inverse(a):
    n = a.shape[0]
    eye = (_iota((n, n), 0) == _iota((n, n), 1)).astype(F32)
    t = eye + a
    p = a
    m = 2
    while m < n:
        p = _dot(p, p, HI)
        t = t + _dot(t, p, HI)
        m *= 2
    return t


def _rwkv7_kernel(r_ref, k_ref, v_ref, wa_ref, gl_ref,
                  mu_r_ref, mu_k_ref, mu_v_ref, mu_wa_ref, mu_gl_ref,
                  w0_ref, a0_ref, w2_ref, a2_ref, g2_ref, kk_ref, ka_ref, rk_ref, lnw_ref, lnb_ref,
                  o_ref,
                  st_ref, prev_ref, lw_s, r_s, k_s, v_s, kk_s, a_s, o_s, *, nchunk):
    rb = r_ref.shape[0]

    @pl.when(pl.program_id(1) == 0)
    def _():
        st_ref[...] = jnp.zeros_like(st_ref)
        prev_ref[...] = jnp.zeros_like(prev_ref)

    lane = _iota((1, LANES), 1)
    head0 = lane < RW_N

    def head_sum(x):
        s0 = jnp.sum(jnp.where(head0, x, 0.0), axis=1, keepdims=True)
        s1 = jnp.sum(jnp.where(head0, 0.0, x), axis=1, keepdims=True)
        return jnp.where(head0, s0, s1)

    r_raw, k_raw, v_raw, wa_raw, gl_raw = r_ref[...], k_ref[...], v_ref[...], wa_ref[...], gl_ref[...]
    r = _shift_lerp(r_raw, prev_ref[0:1, 0:LANES], mu_r_ref[...])
    k = _shift_lerp(k_raw, prev_ref[1:2, 0:LANES], mu_k_ref[...])
    v = _shift_lerp(v_raw, prev_ref[2:3, 0:LANES], mu_v_ref[...])
    wa = _shift_lerp(wa_raw, prev_ref[3:4, 0:LANES], mu_wa_ref[...])
    gl = _shift_lerp(gl_raw, prev_ref[4:5, :], mu_gl_ref[...])
    prev_ref[0:1, 0:LANES] = r_raw[rb - 1:rb]
    prev_ref[1:2, 0:LANES] = k_raw[rb - 1:rb]
    prev_ref[2:3, 0:LANES] = v_raw[rb - 1:rb]
    prev_ref[3:4, 0:LANES] = wa_raw[rb - 1:rb]
    prev_ref[4:5, :] = gl_raw[rb - 1:rb]

    w_log = -_softplus(-(w0_ref[...] + _dot(jnp.tanh(wa), w2_ref[...]))) - 0.5
    lw = -jnp.exp(w_log)
    a = _sigmoid(a0_ref[...] + _dot(wa, a2_ref[...]))
    gate = _dot(_sigmoid(gl), g2_ref[...])
    kk = k * kk_ref[...]
    kk = kk / jnp.maximum(jnp.sqrt(head_sum(kk * kk)), 1e-12)
    k = k * (1.0 + (a - 1.0) * ka_ref[...])
    bonus = head_sum(r * k * rk_ref[...]) * v

    lw_s[...] = lw
    r_s[...] = r
    k_s[...] = k
    v_s[...] = v
    kk_s[...] = kk
    a_s[...] = a

    incl = _tril(CHUNK)
    strict = _tril(CHUNK, strict=True)
    tril = incl.astype(F32)
    bd = (_iota((LANES, LANES), 0) < RW_N) == (_iota((LANES, LANES), 1) < RW_N)

    def chunk(c, carry):
        rows = pl.ds(pl.multiple_of(c * CHUNK, CHUNK), CHUNK)
        lw_c, r_c, k_c, v_c, kk_c, a_c = lw_s[rows, :], r_s[rows, :], k_s[rows, :], v_s[rows, :], kk_s[rows, :], a_s[rows, :]
        g = _dot(tril, lw_c, HI)
        ieg = jnp.exp(-g)
        rt = r_c * jnp.exp(g)
        at = -kk_c * jnp.exp(g - lw_c)
        bt = kk_c * a_c * ieg
        kt = k_c * ieg
        st = st_ref[...]
        w_in = _dot_nt(at, st, HI)
        o_in = _dot_nt(rt, st, HI)
        w_parts, t_inv, rab, rak = [], [], [], []
        for h in range(2):
            hm = head0 if h == 0 else jnp.logical_not(head0)
            at_h = jnp.where(hm, at, 0.0)
            rt_h = jnp.where(hm, rt, 0.0)
            a_ab = jnp.where(strict, _dot_nt(at_h, bt, HI), 0.0)
            a_ak = jnp.where(strict, _dot_nt(at_h, kt, HI), 0.0)
            rab.append(jnp.where(incl, _dot_nt(rt_h, bt, HI), 0.0))
            rak.append(jnp.where(incl, _dot_nt(rt_h, kt, HI), 0.0))
            t_inv.append(_neumann_inverse(a_ab))
            w_parts.append(_dot(a_ak, v_c, HI))
        w_full = w_in + jnp.where(head0, w_parts[0], w_parts[1])
        u = jnp.where(head0, _dot(t_inv[0], w_full, HI), _dot(t_inv[1], w_full, HI))
        o0 = _dot(rab[0], u, HI) + _dot(rak[0], v_c, HI)
        o1 = _dot(rab[1], u, HI) + _dot(rak[1], v_c, HI)
        o_s[rows, :] = o_in + jnp.where(head0, o0, o1)
        eg_last = jnp.exp(g[CHUNK - 1:CHUNK, :])
        upd = _dot_tn(u, bt * eg_last, HI) + _dot_tn(v_c, kt * eg_last, HI)
        st_ref[...] = jnp.where(bd, st * eg_last + upd, 0.0)
        return carry

    lax.fori_loop(0, nchunk, chunk, 0)

    o = o_s[...]
    mu = head_sum(o) * (1.0 / RW_N)
    d = o - mu
    var = head_sum(d * d) * (1.0 / RW_N)
    y = d * lax.rsqrt(var + RW_GN_EPS) * lnw_ref[...] + lnb_ref[...]
    o_ref[...] = ((y + bonus) * gate).astype(BF16)


def _rwkv7(p, mu, w0, w2p, a0, a2p, g2p, k_k, k_a, r_k, ln_w, ln_b, rb):
    s = p.shape[0]
    row = lambda x: x.reshape(1, -1)
    base = 4 * HG_HEADS
    nb = RW_W // LANES
    col = lambda off: pl.BlockSpec((rb, LANES), lambda h, r, off=off: (r, off + h))
    par = pl.BlockSpec((1, LANES), lambda h, r: (0, h))
    par_off = lambda off: pl.BlockSpec((1, LANES), lambda h, r, off=off: (0, off + h))
    tail = RW_TAIL // LANES
    mu_p = jnp.concatenate([mu[:3 * RW_W], mu[3 * RW_W:3 * RW_W + 128], jnp.zeros((128,), F32),
                            mu[3 * RW_W + 128:], jnp.zeros((96,), F32)]).reshape(1, -1)
    return pl.pallas_call(
        functools.partial(_rwkv7_kernel, nchunk=rb // CHUNK),
        grid=(nb, s // rb),
        in_specs=[col(base), col(base + nb), col(base + 2 * nb),
                  pl.BlockSpec((rb, LANES), lambda h, r: (r, tail)),
                  pl.BlockSpec((rb, 2 * LANES), lambda h, r: (r, tail // 2 + 1)),
                  par, par_off(nb), par_off(2 * nb),
                  pl.BlockSpec((1, LANES), lambda h, r: (0, 3 * nb)),
                  pl.BlockSpec((1, 2 * LANES), lambda h, r: (0, 3 * nb // 2 + 1)),
                  par, par,
                  pl.BlockSpec((LANES, LANES), lambda h, r: (0, h)),
                  pl.BlockSpec((LANES, LANES), lambda h, r: (0, h)),
                  pl.BlockSpec((2 * LANES, LANES), lambda h, r: (0, h)),
                  par, par, par, par, par],
        out_specs=pl.BlockSpec((rb, LANES), lambda h, r: (r, h)),
        out_shape=jax.ShapeDtypeStruct((s, RW_W), BF16),
        scratch_shapes=[pltpu.VMEM((LANES, LANES), F32), pltpu.VMEM((8, 2 * LANES), F32)]
                       + [pltpu.VMEM((rb, LANES), F32) for _ in range(7)],
        compiler_params=_cparams(("parallel", "arbitrary")),
        name="rwkv7",
    )(p, p, p, p, p, mu_p, mu_p, mu_p, mu_p, mu_p,
      row(w0), row(a0), w2p, a2p, g2p, row(k_k), row(k_a), row(r_k), row(ln_w), row(ln_b))


def _lane_col(x, idx):
    return jnp.sum(jnp.where(_iota(x.shape, 1) == idx, x, 0.0), axis=1, keepdims=True)


def _transpose_rows(x):
    eye = (_iota((LANES, LANES), 0) == _iota((LANES, LANES), 1)).astype(F32)
    return _dot_nt(eye, x, HI)


def _mlstm_kernel(q_ref, k_ref, v_ref, og_ref, sm_ref, bias_ref, nw_ref, o_ref, c_ref, n_ref, m_ref, *, nchunk):
    @pl.when(pl.program_id(0) == 0)
    def _():
        c_ref[...] = jnp.zeros_like(c_ref)
        n_ref[...] = jnp.zeros_like(n_ref)
        m_ref[...] = jnp.zeros_like(m_ref)

    incl = _tril(CHUNK)
    tril = incl.astype(F32)
    lane = _iota((CHUNK, LANES), 1)
    is_f = jnp.logical_and(lane >= ML_HEADS, lane < 2 * ML_HEADS)
    scale = ML_DQK ** -0.5

    def chunk(c, carry):
        rows = pl.ds(pl.multiple_of(c * CHUNK, CHUNK), CHUNK)
        pre = sm_ref[rows, :] + bias_ref[...]
        cap = ML_CAP * jnp.tanh(pre / ML_CAP)
        x = jnp.where(is_f, _log_sigmoid(cap), cap)
        cum = _dot(tril, x, HI)
        x_t = _transpose_rows(x)
        cum_t = _transpose_rows(cum)
        for h in range(ML_HEADS):
            qs = pl.ds(h * ML_DQK, ML_DQK)
            vs = pl.ds(h * ML_DV, ML_DV)
            q = q_ref[rows, qs] * scale
            k = k_ref[rows, qs]
            v = v_ref[rows, vs]
            b_col = _lane_col(cum, ML_HEADS + h)
            i_col = _lane_col(x, h)
            b_row = cum_t[ML_HEADS + h:ML_HEADS + h + 1, :]
            i_row = x_t[h:h + 1, :]
            m_prev = m_ref[h:h + 1, 0:1]
            c_mat = c_ref[h]
            n_row = n_ref[h:h + 1, :]
            dmat = jnp.where(incl, b_col - b_row + i_row, -jnp.inf)
            inter = b_col + m_prev
            m_t = jnp.maximum(inter, jnp.max(dmat, axis=1, keepdims=True))
            w_inter = jnp.exp(inter - m_t)
            pmat = jnp.exp(dmat - m_t) * _dot_nt(q, k)
            num = _dot(pmat, v) + w_inter * _dot(q, c_mat)
            den = jnp.sum(pmat, axis=1, keepdims=True) + w_inter * jnp.sum(q * n_row, axis=1, keepdims=True)
            h_out = num / jnp.maximum(jnp.abs(den), jnp.exp(-m_t))
            b_last = b_row[:, CHUNK - 1:CHUNK]
            src_row = b_last - b_row + i_row
            src_col = b_last - b_col + i_col
            m_new = jnp.maximum(b_last + m_prev, jnp.max(src_row, axis=1, keepdims=True))
            ws = jnp.exp(src_col - m_new)
            decay = jnp.exp(b_last + m_prev - m_new)
            wk = ws * k
            c_ref[h] = decay * c_mat + _dot_tn(wk, v)
            n_ref[h:h + 1, :] = decay * n_row + jnp.sum(wk, axis=0, keepdims=True)
            m_ref[h:h + 1, :] = jnp.broadcast_to(m_new, (1, LANES))
            y = h_out * lax.rsqrt(jnp.mean(h_out * h_out, axis=-1, keepdims=True) + EPS) * nw_ref[:, vs]
            o_ref[rows, vs] = (y * _sigmoid(og_ref[rows, vs])).astype(BF16)
        return carry

    lax.fori_loop(0, nchunk, chunk, 0)


def _mlstm(p, bias_row, norm_w, rb):
    s = p.shape[0]
    nq = ML_HEADS * ML_DQK
    nv = ML_HEADS * ML_DV
    return pl.pallas_call(
        functools.partial(_mlstm_kernel, nchunk=rb // CHUNK),
        grid=(s // rb,),
        in_specs=[pl.BlockSpec((rb, nq), lambda r: (r, 0)),
                  pl.BlockSpec((rb, nq), lambda r: (r, 1)),
                  pl.BlockSpec((rb, nv), lambda r: (r, 1)),
                  pl.BlockSpec((rb, nv), lambda r: (r, 2)),
                  pl.BlockSpec((rb, LANES), lambda r: (r, OD_SMALL // LANES)),
                  pl.BlockSpec((1, LANES), lambda r: (0, 0)),
                  pl.BlockSpec((1, nv), lambda r: (0, 0))],
        out_specs=pl.BlockSpec((rb, nv), lambda r: (r, 0)),
        out_shape=jax.ShapeDtypeStruct((s, nv), BF16),
        scratch_shapes=[pltpu.VMEM((ML_HEADS, ML_DQK, ML_DV), F32),
                        pltpu.VMEM((8, ML_DQK), F32), pltpu.VMEM((8, LANES), F32)],
        compiler_params=_cparams(("arbitrary",)),
        name="mlstm",
    )(p, p, p, p, p, bias_row, norm_w.reshape(1, -1))


def _ssd_kernel(z_ref, xbc_ref, sm_ref, cw_ref, cb_ref, bias_ref, negA_ref, dvec_ref, nw_ref, o_ref,
                st_ref, xin_s, xc_s, *, nchunk):
    rb = z_ref.shape[0]
    pad = 8

    @pl.when(pl.program_id(0) == 0)
    def _():
        st_ref[...] = jnp.zeros_like(st_ref)
        xin_s[rb:rb + pad, :] = jnp.zeros((pad, xin_s.shape[1]), F32)

    xin_s[0:pad, :] = xin_s[rb:rb + pad, :]
    xin_s[pad:rb + pad, :] = xbc_ref[...]
    xin = xin_s[...]
    acc = cb_ref[...] + xin[pad:, :] * cw_ref[MB_CONV - 1:MB_CONV, :]
    for j in range(MB_CONV - 1):
        acc = acc + pltpu.roll(xin, MB_CONV - 1 - j, 0)[pad:, :] * cw_ref[j:j + 1, :]
    xc_s[...] = acc * _sigmoid(acc)

    incl = _tril(CHUNK)
    tril = incl.astype(F32)
    lane = _iota((1, LANES), 1)
    head0 = lane < MB_P
    sub0 = _iota((LANES, 1), 0) < MB_P
    hpg = MB_HEADS // MB_GROUPS
    gw = MB_DI // MB_GROUPS
    dt_lane0 = 2 * ML_HEADS

    def chunk(c, carry):
        rows = pl.ds(pl.multiple_of(c * CHUNK, CHUNK), CHUNK)
        dt = _softplus(sm_ref[rows, :] + bias_ref[...])
        da = negA_ref[...] * dt
        a_cum = _dot(tril, da, HI)
        a_t = _transpose_rows(a_cum)
        dt_t = _transpose_rows(dt)
        for g in range(MB_GROUPS):
            bm = xc_s[rows, pl.ds(MB_DI + g * MB_N, MB_N)]
            cm = xc_s[rows, pl.ds(MB_DI + MB_GROUPS * MB_N + g * MB_N, MB_N)]
            cb = _dot_nt(cm, bm)
            ys = []
            for j in range(hpg // 2):
                xs = pl.ds(g * gw + j * LANES, LANES)
                x2 = xc_s[rows, xs]
                sc, ea, ws, el = [], [], [], []
                for e in range(2):
                    ln = dt_lane0 + g * hpg + 2 * j + e
                    a_col = _lane_col(a_cum, ln)
                    a_row = a_t[ln:ln + 1, :]
                    dt_row = dt_t[ln:ln + 1, :]
                    dt_col = _lane_col(dt, ln)
                    seg = jnp.exp(jnp.where(incl, a_col - a_row, -jnp.inf))
                    sc.append(seg * cb * dt_row)
                    ea.append(jnp.exp(a_col))
                    a_last = a_row[:, CHUNK - 1:CHUNK]
                    ws.append(jnp.exp(a_last - a_col) * dt_col)
                    el.append(jnp.exp(a_last))
                si = g * (hpg // 2) + j
                st = st_ref[si]
                y = jnp.where(head0, _dot(sc[0], x2), _dot(sc[1], x2))
                y = y + jnp.where(head0, ea[0], ea[1]) * _dot_nt(cm, st)
                wsx = x2 * jnp.where(head0, ws[0], ws[1])
                st_ref[si] = jnp.where(sub0, el[0], el[1]) * st + _dot_tn(wsx, bm)
                ys.append(y + x2 * dvec_ref[:, xs])
            yg = jnp.concatenate(ys, axis=1)
            zg = z_ref[rows, pl.ds(g * gw, gw)]
            yg = yg * (zg * _sigmoid(zg))
            yg = yg * lax.rsqrt(jnp.mean(yg * yg, axis=-1, keepdims=True) + EPS) * nw_ref[:, pl.ds(g * gw, gw)]
            o_ref[rows, pl.ds(g * gw, gw)] = yg.astype(BF16)
        return carry

    lax.fori_loop(0, nchunk, chunk, 0)


def _ssd(p, conv_w, conv_b, bias_row, negA_row, d_row, norm_w, rb):
    s = p.shape[0]
    nx = conv_w.shape[1]
    return pl.pallas_call(
        functools.partial(_ssd_kernel, nchunk=rb // CHUNK),
        grid=(s // rb,),
        in_specs=[pl.BlockSpec((rb, MB_DI), lambda r: (r, 3)),
                  pl.BlockSpec((rb, nx), lambda r: (r, OD_XBC // nx)),
                  pl.BlockSpec((rb, LANES), lambda r: (r, OD_SMALL // LANES)),
                  pl.BlockSpec((MB_CONV, nx), lambda r: (0, 0)),
                  pl.BlockSpec((1, nx), lambda r: (0, 0)),
                  pl.BlockSpec((1, LANES), lambda r: (0, 0)),
                  pl.BlockSpec((1, LANES), lambda r: (0, 0)),
                  pl.BlockSpec((1, MB_DI), lambda r: (0, 0)),
                  pl.BlockSpec((1, MB_DI), lambda r: (0, 0))],
        out_specs=pl.BlockSpec((rb, MB_DI), lambda r: (r, 0)),
        out_shape=jax.ShapeDtypeStruct((s, MB_DI), BF16),
        scratch_shapes=[pltpu.VMEM((MB_HEADS // 2, 2 * MB_P, MB_N), F32),
                        pltpu.VMEM((rb + 8, nx), F32), pltpu.VMEM((rb, nx), F32)],
        compiler_params=_cparams(("arbitrary",)),
        name="ssd",
    )(p, p, p, conv_w, conv_b.reshape(1, -1), bias_row, negA_row, d_row, norm_w.reshape(1, -1))


def _pad_cols(w, n):
    return jnp.pad(w, ((0, 0), (0, n - w.shape[1])))


def _even_in_weight(w):
    k = w.shape[0]
    z = lambda n: jnp.zeros((k, n), w.dtype)
    return jnp.concatenate([w[:, :RW_TAIL + 128], z(128), w[:, RW_TAIL + 128:], z(96)], axis=1).astype(BF16)


def _odd_in_weight(w):
    head = jnp.concatenate([w[:, 0:2048], w[:, 2056:3080], w[:, 3080:4104], w[:, 2048:2056], w[:, 5640:5656]], axis=1)
    return jnp.concatenate([_pad_cols(head, OD_XBC), w[:, 4104:5640]], axis=1).astype(BF16)


def _even_mixer(h, norm_g, layer, w_in, w_out, lb_table, hg_norm, mu, w0, w2, a0, a2, g2, k_k, k_a, r_k, ln_w, ln_b,
                rb_hg, rb_rw):
    p = _norm_matmul(h, norm_g, _even_in_weight(w_in))
    o_hg = _hgrn2(p, lb_table, hg_norm, layer, rb_hg)
    lora = w2.shape[0]
    w2p = jnp.concatenate([w2, jnp.zeros_like(w2)], axis=0)
    a2p = jnp.concatenate([jnp.zeros_like(a2), a2], axis=0)
    g2p = jnp.pad(g2, ((0, 2 * LANES - g2.shape[0]), (0, 0)))
    assert 2 * lora == LANES
    o_rw = _rwkv7(p, mu, w0, w2p, a0, a2p, g2p, k_k, k_a, r_k.reshape(-1), ln_w, ln_b, rb_rw)
    mix = jnp.concatenate([o_hg, o_rw], axis=1)
    return _matmul_residual(mix, w_out.astype(BF16), h)


def _odd_mixer(h, norm_g, w_in, w_out, i_bias, f_bias, ml_norm, conv_w, conv_b, dt_bias, a_log, d_vec, mb_norm,
               rb_ml, rb_ssd):
    p = _norm_matmul(h, norm_g, _odd_in_weight(w_in))
    small = lambda *xs: jnp.pad(jnp.concatenate(xs), (0, LANES - sum(x.shape[0] for x in xs))).reshape(1, LANES)
    zeros8 = jnp.zeros((2 * ML_HEADS,), F32)
    o_ml = _mlstm(p, small(i_bias, f_bias), ml_norm, rb_ml)
    neg_a = -jnp.exp(a_log.astype(F32))
    y = _ssd(p, conv_w, conv_b, small(zeros8, dt_bias), small(zeros8, neg_a),
             jnp.repeat(d_vec, MB_P).reshape(1, -1), mb_norm, rb_ssd)
    mix = jnp.concatenate([o_ml, y], axis=1)
    return _matmul_residual(mix, w_out.astype(BF16), h)


def _ffn(h, norm_g, w_up, w_down):
    act = _norm_swiglu(h, norm_g, w_up.astype(BF16))
    return _matmul_residual(act, w_down.astype(BF16), h)


def kernel(x, norm_mix, norm_ffn, norm_final, w_in_even, w_out_even, hg_lb_table, hg_norm, rw_mu, rw_w0, rw_w2, rw_a0, rw_a2, rw_g2, rw_k_k, rw_k_a, rw_r_k, rw_ln_w, rw_ln_b, w_in_odd, w_out_odd, ml_i_bias, ml_f_bias, ml_norm, mb_conv_w, mb_conv_b, mb_dt_bias, mb_A_log, mb_D, mb_norm, ffn_w_up, ffn_w_down):
    b, s, d = x.shape
    assert b == 1 and d == D_MODEL and s % TM == 0
    rb_hg, rb_rw, rb_ml, rb_ssd = min(s, 512), min(s, 256), min(s, 256), min(s, 256)
    h = x.reshape(s, d)
    for layer in range(DEPTH):
        j = layer // 2
        if layer % 2 == 0:
            h = _even_mixer(h, norm_mix[layer], layer, w_in_even[j], w_out_even[j], hg_lb_table, hg_norm[j],
                            rw_mu[j], rw_w0[j], rw_w2[j], rw_a0[j], rw_a2[j], rw_g2[j], rw_k_k[j], rw_k_a[j],
                            rw_r_k[j], rw_ln_w[j], rw_ln_b[j], rb_hg, rb_rw)
        else:
            h = _odd_mixer(h, norm_mix[layer], w_in_odd[j], w_out_odd[j], ml_i_bias[j], ml_f_bias[j], ml_norm[j],
                           mb_conv_w[j], mb_conv_b[j], mb_dt_bias[j], mb_A_log[j], mb_D[j], mb_norm[j],
                           rb_ml, rb_ssd)
        h = _ffn(h, norm_ffn[layer], ffn_w_up[layer], ffn_w_down[layer])
    return _rmsnorm(h, norm_final).reshape(b, s, d)
```

```python
import functools

import jax
import jax.numpy as jnp
from jax import lax
from jax.experimental import pallas as pl
from jax.experimental.pallas import tpu as pltpu

F32 = jnp.float32
BF16 = jnp.bfloat16
HI = lax.Precision.HIGHEST

D_MODEL = 2048
DEPTH = 4
CHUNK = 64
EPS = 1e-6
LANES = 128
SUB = 16

HG_HEADS, HG_D = 8, 128
RW_HEADS, RW_N, RW_W = 16, 64, 1024
RW_GN_EPS = 64e-5
ML_HEADS, ML_DQK, ML_DV = 4, 128, 256
ML_CAP = 15.0
MB_HEADS, MB_P, MB_N, MB_GROUPS, MB_CONV = 16, 64, 128, 2, 4
MB_DI = MB_HEADS * MB_P
D_FF = 5632

NP_EVEN = 7680
NP_ODD = 6144
RW_TAIL = 7168
OD_SMALL = 4096
OD_XBC = 4608

TM = 1024
TN = 512
VMEM_LIMIT = 56 * 1024 * 1024


def _cparams(sem):
    return pltpu.CompilerParams(dimension_semantics=sem, vmem_limit_bytes=VMEM_LIMIT)


def _dot(a, b, prec=None):
    return jnp.dot(a, b, preferred_element_type=F32, precision=prec)


def _dot_nt(a, b, prec=None):
    return lax.dot_general(a, b, (((1,), (1,)), ((), ())), preferred_element_type=F32, precision=prec)


def _dot_tn(a, b, prec=None):
    return lax.dot_general(a, b, (((0,), (0,)), ((), ())), preferred_element_type=F32, precision=prec)


def _sigmoid(x):
    return 1.0 / (1.0 + jnp.exp(-x))


def _log_sigmoid(x):
    return jnp.minimum(x, 0.0) - jnp.log1p(jnp.exp(-jnp.abs(x)))


def _softplus(x):
    return jnp.maximum(x, 0.0) + jnp.log1p(jnp.exp(-jnp.abs(x)))


def _iota(shape, dim):
    return lax.broadcasted_iota(jnp.int32, shape, dim)


def _tril(n, strict=False):
    r, c = _iota((n, n), 0), _iota((n, n), 1)
    return (r > c) if strict else (r >= c)


def _rms_rows(x, g):
    return (x * lax.rsqrt(jnp.mean(x * x, axis=-1, keepdims=True) + EPS)) * g


def _norm_mm_kernel(x_ref, g_ref, w_ref, o_ref, xn_ref):
    @pl.when(pl.program_id(1) == 0)
    def _():
        xn_ref[...] = _rms_rows(x_ref[...], g_ref[...]).astype(BF16)

    o_ref[...] = _dot(xn_ref[...], w_ref[...])


def _norm_matmul(x, g, w):
    s, k = x.shape
    n = w.shape[1]
    return pl.pallas_call(
        _norm_mm_kernel,
        grid=(s // TM, n // TN),
        in_specs=[pl.BlockSpec((TM, k), lambda i, j: (i, 0)),
                  pl.BlockSpec((1, k), lambda i, j: (0, 0)),
                  pl.BlockSpec((k, TN), lambda i, j: (0, j))],
        out_specs=pl.BlockSpec((TM, TN), lambda i, j: (i, j)),
        out_shape=jax.ShapeDtypeStruct((s, n), F32),
        scratch_shapes=[pltpu.VMEM((TM, k), BF16)],
        compiler_params=_cparams(("parallel", "arbitrary")),
        name="norm_in_proj",
    )(x, g.reshape(1, k), w)


def _norm_swiglu_kernel(x_ref, g_ref, wg_ref, wu_ref, o_ref, xn_ref):
    @pl.when(pl.program_id(1) == 0)
    def _():
        xn_ref[...] = _rms_rows(x_ref[...], g_ref[...]).astype(BF16)

    xn = xn_ref[...]
    gate = _dot(xn, wg_ref[...])
    up = _dot(xn, wu_ref[...])
    o_ref[...] = (gate * _sigmoid(gate) * up).astype(BF16)


def _norm_swiglu(x, g, w_up):
    s, k = x.shape
    nj = D_FF // TN
    return pl.pallas_call(
        _norm_swiglu_kernel,
        grid=(s // TM, nj),
        in_specs=[pl.BlockSpec((TM, k), lambda i, j: (i, 0)),
                  pl.BlockSpec((1, k), lambda i, j: (0, 0)),
                  pl.BlockSpec((k, TN), lambda i, j: (0, j)),
                  pl.BlockSpec((k, TN), lambda i, j: (0, j + nj))],
        out_specs=pl.BlockSpec((TM, TN), lambda i, j: (i, j)),
        out_shape=jax.ShapeDtypeStruct((s, D_FF), BF16),
        scratch_shapes=[pltpu.VMEM((TM, k), BF16)],
        compiler_params=_cparams(("parallel", "arbitrary")),
        name="norm_ffn_up",
    )(x, g.reshape(1, k), w_up, w_up)


def _mm_res_kernel(x_ref, w_ref, r_ref, o_ref):
    o_ref[...] = r_ref[...] + _dot(x_ref[...], w_ref[...])


def _matmul_residual(x, w, res):
    s, k = x.shape
    n = w.shape[1]
    return pl.pallas_call(
        _mm_res_kernel,
        grid=(s // TM, n // TN),
        in_specs=[pl.BlockSpec((TM, k), lambda i, j: (i, 0)),
                  pl.BlockSpec((k, TN), lambda i, j: (0, j)),
                  pl.BlockSpec((TM, TN), lambda i, j: (i, j))],
        out_specs=pl.BlockSpec((TM, TN), lambda i, j: (i, j)),
        out_shape=jax.ShapeDtypeStruct((s, n), F32),
        compiler_params=_cparams(("parallel", "parallel")),
        name="proj_residual",
    )(x, w, res)


def _mm2_res_kernel(x1_ref, x2_ref, w1_ref, w2_ref, r_ref, o_ref):
    o_ref[...] = r_ref[...] + (_dot(x1_ref[...], w1_ref[...]) + _dot(x2_ref[...], w2_ref[...]))


def _matmul2_residual(x1, x2, w, res):
    s, k = x1.shape
    n = w.shape[1]
    assert x2.shape == (s, k) and w.shape[0] == 2 * k
    return pl.pallas_call(
        _mm2_res_kernel,
        grid=(s // TM, n // TN),
        in_specs=[pl.BlockSpec((TM, k), lambda i, j: (i, 0)),
                  pl.BlockSpec((TM, k), lambda i, j: (i, 0)),
                  pl.BlockSpec((k, TN), lambda i, j: (0, j)),
                  pl.BlockSpec((k, TN), lambda i, j: (1, j)),
                  pl.BlockSpec((TM, TN), lambda i, j: (i, j))],
        out_specs=pl.BlockSpec((TM, TN), lambda i, j: (i, j)),
        out_shape=jax.ShapeDtypeStruct((s, n), F32),
        compiler_params=_cparams(("parallel", "parallel")),
        name="out_proj_residual",
    )(x1, x2, w, w, res)


def _rmsnorm_kernel(x_ref, g_ref, o_ref):
    o_ref[...] = _rms_rows(x_ref[...], g_ref[...])


def _rmsnorm(x, g):
    s, k = x.shape
    return pl.pallas_call(
        _rmsnorm_kernel,
        grid=(s // TM,),
        in_specs=[pl.BlockSpec((TM, k), lambda i: (i, 0)), pl.BlockSpec((1, k), lambda i: (0, 0))],
        out_specs=pl.BlockSpec((TM, k), lambda i: (i, 0)),
        out_shape=jax.ShapeDtypeStruct((s, k), F32),
        compiler_params=_cparams(("parallel",)),
        name="final_norm",
    )(x, g.reshape(1, k))


def _hgrn2_kernel(q_ref, f_ref, v_ref, g_ref, lbt_ref, nw_ref, o_ref, st_ref, *, layer, nchunk):
    @pl.when(pl.program_id(1) == 0)
    def _():
        st_ref[...] = jnp.zeros_like(st_ref)

    t = lbt_ref[...]
    e = jnp.exp(t - jnp.max(t, axis=0, keepdims=True))
    sm = e / jnp.sum(e, axis=0, keepdims=True)
    lb = jnp.zeros((1, HG_D), F32)
    for i in range(1, layer + 1):
        lb = lb + sm[i:i + 1, :]
    log_lb = jnp.log(lb)
    log_1m = jnp.log1p(-lb)
    nw = nw_ref[...]

    tril = _tril(CHUNK).astype(BF16)
    rowid = _iota((SUB, HG_D), 0)
    nsub = CHUNK // SUB
    cs = range(nchunk)
    rows = [slice(c * CHUNK, (c + 1) * CHUNK) for c in cs]

    fp = [f_ref[r, :] for r in rows]
    v = [v_ref[r, :] for r in rows]
    b2 = [log_1m + _log_sigmoid(x) for x in fp]
    log_f = [jnp.maximum(log_lb, x) + jnp.log1p(jnp.exp(-jnp.abs(log_lb - x))) for x in b2]
    k = [(1.0 - lb) * _sigmoid(-x) for x in fp]
    q = [x * _sigmoid(x) for x in (q_ref[r, :] for r in rows)]
    bc = [_cumsum_rows(tril, x) for x in log_f]
    b_last = [x[CHUNK - 1:CHUNK, :] for x in bc]
    kv = [_mm(v[c], k[c] * jnp.exp(b_last[c] - bc[c]), "bf16", "tn") for c in cs]
    st = st_ref[...]
    starts = []
    for c in cs:
        starts.append(st)
        st = jnp.exp(b_last[c]) * st + kv[c]
    st_ref[...] = st
    acc = [_mm(q[c] * jnp.exp(bc[c]), starts[c], "bf16", "nt") for c in cs]
    acc = [[a[blk * SUB:(blk + 1) * SUB] for blk in range(nsub)] for a in acc]
    for blk in range(1, nsub):
        lo = blk * SUB
        ref_b = [x[lo - 1:lo, :] for x in bc]
        qx = [q[c][lo:lo + SUB] * jnp.exp(bc[c][lo:lo + SUB] - ref_b[c]) for c in cs]
        kx = [k[c][0:lo] * jnp.exp(ref_b[c] - bc[c][0:lo]) for c in cs]
        sc = [_mm(qx[c], kx[c], "bf16", "nt") for c in cs]
        od = [_mm(sc[c], v[c][0:lo], "bf16") for c in cs]
        for c in cs:
            acc[c][blk] = acc[c][blk] + od[c]
    for s in range(SUB):
        for c in cs:
            for blk in range(nsub):
                lo = blk * SUB
                b_i = bc[c][lo:lo + SUB]
                d = b_i - bc[c][lo + s:lo + s + 1, :]
                if s > 0:
                    d = jnp.where(rowid >= s, d, -jnp.inf)
                col = jnp.sum(q[c][lo:lo + SUB] * k[c][lo + s:lo + s + 1, :] * jnp.exp(d), axis=1, keepdims=True)
                acc[c][blk] = acc[c][blk] + col * v[c][lo + s:lo + s + 1, :]
    for c in cs:
        o = jnp.concatenate(acc[c], axis=0)
        y = o * lax.rsqrt(jnp.mean(o * o, axis=-1, keepdims=True) + EPS) * nw
        o_ref[rows[c], :] = (y * _sigmoid(g_ref[rows[c], :])).astype(BF16)


def _hgrn2(p, lb_table, norm_w, layer, rb):
    s = p.shape[0]
    nb = HG_HEADS
    col = lambda off: pl.BlockSpec((rb, HG_D), lambda h, r, off=off: (r, off + h))
    return pl.pallas_call(
        functools.partial(_hgrn2_kernel, layer=layer, nchunk=rb // CHUNK),
        grid=(HG_HEADS, s // rb),
        in_specs=[col(0), col(nb), col(2 * nb), col(3 * nb),
                  pl.BlockSpec((DEPTH, HG_D), lambda h, r: (0, h)),
                  pl.BlockSpec((1, HG_D), lambda h, r: (0, h))],
        out_specs=pl.BlockSpec((rb, HG_D), lambda h, r: (r, h)),
        out_shape=jax.ShapeDtypeStruct((s, HG_HEADS * HG_D), BF16),
        scratch_shapes=[pltpu.VMEM((HG_D, HG_D), F32)],
        compiler_params=_cparams(("parallel", "arbitrary")),
        name="hgrn2",
    )(p, p, p, p, lb_table, norm_w.reshape(1, -1))


def _shift_lerp(x, prev_row, mu):
    rolled = pltpu.roll(x, 1, 0)
    shifted = jnp.where(_iota(x.shape, 0) == 0, prev_row, rolled)
    return x + (shifted - x) * mu


def _split_bf16(x):
    hi = x.astype(BF16)
    lo = (x - hi.astype(F32)).astype(BF16)
    return hi, lo


def _mm(a, b, mode, kind="nn"):
    dot = {"nn": _dot, "nt": _dot_nt, "tn": _dot_tn}[kind]
    if mode == "hi":
        return dot(a, b, HI)
    if mode == "bf16":
        return dot(a.astype(BF16), b.astype(BF16))
    ah, al = _split_bf16(a)
    bh, bl = _split_bf16(b)
    return dot(ah, bh) + (dot(ah, bl) + dot(al, bh))


def _cumsum_rows(tril_bf16, x):
    hi, lo = _split_bf16(x)
    lo2 = (x - hi.astype(F32) - lo.astype(F32)).astype(BF16)
    return _dot(tril_bf16, hi) + (_dot(tril_bf16, lo) + _dot(tril_bf16, lo2))


def _neumann_inverse(mats, mode):
    n = mats[0].shape[0]
    eye = (_iota((n, n), 0) == _iota((n, n), 1)).astype(F32)
    sq = lambda xs: [_mm(x, x, mode) for x in xs]
    mul = lambda xs, ys: [_mm(x, y, mode) for x, y in zip(xs, ys)]
    one_plus = lambda xs: [eye + x for x in xs]
    p1 = sq(mats)
    m01 = mul(one_plus(mats), one_plus(p1))
    p2 = sq(p1)
    p3 = sq(p2)
    m23 = mul(one_plus(p2), one_plus(p3))
    p4 = sq(p3)
    m03 = mul(m01, m23)
    p5 = sq(p4)
    m45 = mul(one_plus(p4), one_plus(p5))
    return mul(m03, m45)


RW_PREC = dict(score="bf16", neumann="bf16", apply="bf16", state="bf16")


def _rwkv7_kernel(r_ref, k_ref, v_ref, wa_ref, gl_ref,
                  mu_r_ref, mu_k_ref, mu_v_ref, mu_wa_ref, mu_gl_ref,
                  w0_ref, a0_ref, w2_ref, a2_ref, g2_ref, kk_ref, ka_ref, rk_ref, lnw_ref, lnb_ref,
                  o_ref, st_ref, prev_ref, *, nchunk):
    rb = r_ref.shape[0]
    pc = RW_PREC

    @pl.when(pl.program_id(1) == 0)
    def _():
        st_ref[...] = jnp.zeros_like(st_ref)
        prev_ref[...] = jnp.zeros_like(prev_ref)

    lane = _iota((1, LANES), 1)
    head0 = lane < RW_N

    def head_sum(x):
        s0 = jnp.sum(jnp.where(head0, x, 0.0), axis=1, keepdims=True)
        s1 = jnp.sum(jnp.where(head0, 0.0, x), axis=1, keepdims=True)
        return jnp.where(head0, s0, s1)

    r_raw, k_raw, v_raw, wa_raw, gl_raw = r_ref[...], k_ref[...], v_ref[...], wa_ref[...], gl_ref[...]
    r = _shift_lerp(r_raw, prev_ref[0:1, 0:LANES], mu_r_ref[...])
    k = _shift_lerp(k_raw, prev_ref[1:2, 0:LANES], mu_k_ref[...])
    v = _shift_lerp(v_raw, prev_ref[2:3, 0:LANES], mu_v_ref[...])
    wa = _shift_lerp(wa_raw, prev_ref[3:4, 0:LANES], mu_wa_ref[...])
    gl = _shift_lerp(gl_raw, prev_ref[4:5, :], mu_gl_ref[...])
    prev_ref[0:1, 0:LANES] = r_raw[rb - 1:rb]
    prev_ref[1:2, 0:LANES] = k_raw[rb - 1:rb]
    prev_ref[2:3, 0:LANES] = v_raw[rb - 1:rb]
    prev_ref[3:4, 0:LANES] = wa_raw[rb - 1:rb]
    prev_ref[4:5, :] = gl_raw[rb - 1:rb]

    w_log = -_softplus(-(w0_ref[...] + _dot(jnp.tanh(wa), w2_ref[...]))) - 0.5
    lw = -jnp.exp(w_log)
    a = _sigmoid(a0_ref[...] + _dot(wa, a2_ref[...]))
    gate = _dot(_sigmoid(gl), g2_ref[...])
    kk = k * kk_ref[...]
    kk = kk / jnp.maximum(jnp.sqrt(head_sum(kk * kk)), 1e-12)
    k = k * (1.0 + (a - 1.0) * ka_ref[...])
    bonus = head_sum(r * k * rk_ref[...]) * v

    head1 = jnp.logical_not(head0)
    bd = (_iota((LANES, LANES), 0) < RW_N) == (_iota((LANES, LANES), 1) < RW_N)
    gc = 1
    gr = gc * CHUNK
    gs = 2 * gr
    ri, ci = _iota((gs, gs), 0), _iota((gs, gs), 1)
    same = (ri // CHUNK) == (ci // CHUNK)
    strict = jnp.logical_and(same, ri > ci)
    incl = jnp.logical_and(same, ri >= ci)
    rj, cj = _iota((gr, gr), 0), _iota((gr, gr), 1)
    tril = jnp.logical_and((rj // CHUNK) == (cj // CHUNK), rj >= cj).astype(BF16)

    def stack_heads(x, masked):
        parts = []
        for c in range(gc):
            xc = x[c * CHUNK:(c + 1) * CHUNK]
            parts += [jnp.where(head0, xc, 0.0), jnp.where(head1, xc, 0.0)] if masked else [xc, xc]
        return jnp.concatenate(parts, axis=0)

    def unstack_heads(x, c):
        lo = 2 * c * CHUNK
        return jnp.where(head0 if x.shape[1] == LANES else jnp.concatenate([head0, head0], axis=1),
                         x[lo:lo + CHUNK], x[lo + CHUNK:lo + 2 * CHUNK])

    grps = range(nchunk // gc)
    rows = [slice(i * gr, (i + 1) * gr) for i in grps]
    g = [_cumsum_rows(tril, lw[s]) for s in rows]
    ieg = [jnp.exp(-x) for x in g]
    rt = [r[s] * jnp.exp(x) for s, x in zip(rows, g)]
    at = [-kk[s] * jnp.exp(x - lw[s]) for s, x in zip(rows, g)]
    bt = [kk[s] * a[s] * e for s, e in zip(rows, ieg)]
    kt = [k[s] * e for s, e in zip(rows, ieg)]
    l_a = [stack_heads(x, True) for x in at]
    l_r = [stack_heads(x, True) for x in rt]
    r_b = [stack_heads(x, True) for x in bt]
    r_k = [stack_heads(x, True) for x in kt]
    a_ab = [jnp.where(strict, _mm(x, y, pc["score"], "nt"), 0.0) for x, y in zip(l_a, r_b)]
    a_ak = [jnp.where(strict, _mm(x, y, pc["score"], "nt"), 0.0) for x, y in zip(l_a, r_k)]
    rab = [jnp.where(incl, _mm(x, y, pc["score"], "nt"), 0.0) for x, y in zip(l_r, r_b)]
    rak = [jnp.where(incl, _mm(x, y, pc["score"], "nt"), 0.0) for x, y in zip(l_r, r_k)]
    v_st = [stack_heads(v[s], False) for s in rows]
    aakv = [_mm(x, y, pc["apply"]) for x, y in zip(a_ak, v_st)]
    rakv = [_mm(x, y, pc["apply"]) for x, y in zip(rak, v_st)]
    t_inv = _neumann_inverse(a_ab, pc["neumann"])
    xs = [_mm(t, jnp.concatenate([stack_heads(x, False), y], axis=1), pc["apply"])
          for t, x, y in zip(t_inv, at, aakv)]
    pre = []
    for i in grps:
        for c in range(gc):
            cs = slice(c * CHUNK, (c + 1) * CHUNK)
            egl = jnp.exp(g[i][(c + 1) * CHUNK - 1:(c + 1) * CHUNK, :])
            xc = unstack_heads(xs[i], c)
            ta, tav = xc[:, :LANES], xc[:, LANES:]
            bte, kte = bt[i][cs] * egl, kt[i][cs] * egl
            blk = slice(2 * c * CHUNK, 2 * (c + 1) * CHUNK)
            pre.append(dict(ta=ta, tav=tav, rt=rt[i][cs], rab=rab[i][blk, blk], rakv=unstack_heads(rakv[i], c),
                            egl=egl, bte=bte, lhs=jnp.concatenate([tav, v[rows[i]][cs]], axis=0),
                            rhs=jnp.concatenate([bte, kte], axis=0)))
    for q in pre:
        q["m"] = jnp.where(bd, _mm(q["ta"], q["bte"], pc["state"], "tn"), 0.0)
    for q in pre:
        q["c"] = jnp.where(bd, _mm(q["lhs"], q["rhs"], pc["state"], "tn"), 0.0)

    st = st_ref[...]
    starts = []
    for q in pre:
        starts.append(st)
        st = st * q["egl"] + _mm(st, q["m"], pc["state"]) + q["c"]
    st_ref[...] = st
    outs = []
    for q, s0 in zip(pre, starts):
        u = _mm(q["ta"], s0, pc["state"], "nt") + q["tav"]
        o_in = _mm(q["rt"], s0, pc["state"], "nt")
        y = _mm(q["rab"], jnp.concatenate([jnp.where(head0, u, 0.0), jnp.where(head1, u, 0.0)], axis=0), pc["apply"])
        outs.append(o_in + (y[:CHUNK] + y[CHUNK:]) + q["rakv"])

    o = jnp.concatenate(outs, axis=0)
    mu = head_sum(o) * (1.0 / RW_N)
    d = o - mu
    var = head_sum(d * d) * (1.0 / RW_N)
    y = d * lax.rsqrt(var + RW_GN_EPS) * lnw_ref[...] + lnb_ref[...]
    o_ref[...] = ((y + bonus) * gate).astype(BF16)


def _rwkv7(p, mu, w0, w2p, a0, a2p, g2p, k_k, k_a, r_k, ln_w, ln_b, rb):
    s = p.shape[0]
    row = lambda x: x.reshape(1, -1)
    base = 4 * HG_HEADS
    nb = RW_W // LANES
    col = lambda off: pl.BlockSpec((rb, LANES), lambda h, r, off=off: (r, off + h))
    par = pl.BlockSpec((1, LANES), lambda h, r: (0, h))
    par_off = lambda off: pl.BlockSpec((1, LANES), lambda h, r, off=off: (0, off + h))
    tail = RW_TAIL // LANES
    mu_p = jnp.concatenate([mu[:3 * RW_W], mu[3 * RW_W:3 * RW_W + 128], jnp.zeros((128,), F32),
                            mu[3 * RW_W + 128:], jnp.zeros((96,), F32)]).reshape(1, -1)
    return pl.pallas_call(
        functools.partial(_rwkv7_kernel, nchunk=rb // CHUNK),
        grid=(nb, s // rb),
        in_specs=[col(base), col(base + nb), col(base + 2 * nb),
                  pl.BlockSpec((rb, LANES), lambda h, r: (r, tail)),
                  pl.BlockSpec((rb, 2 * LANES), lambda h, r: (r, tail // 2 + 1)),
                  par, par_off(nb), par_off(2 * nb),
                  pl.BlockSpec((1, LANES), lambda h, r: (0, 3 * nb)),
                  pl.BlockSpec((1, 2 * LANES), lambda h, r: (0, 3 * nb // 2 + 1)),
                  par, par,
                  pl.BlockSpec((LANES, LANES), lambda h, r: (0, h)),
                  pl.BlockSpec((LANES, LANES), lambda h, r: (0, h)),
                  pl.BlockSpec((2 * LANES, LANES), lambda h, r: (0, h)),
                  par, par, par, par, par],
        out_specs=pl.BlockSpec((rb, LANES), lambda h, r: (r, h)),
        out_shape=jax.ShapeDtypeStruct((s, RW_W), BF16),
        scratch_shapes=[pltpu.VMEM((LANES, LANES), F32), pltpu.VMEM((8, 2 * LANES), F32)],
        compiler_params=_cparams(("parallel", "arbitrary")),
        name="rwkv7",
    )(p, p, p, p, p, mu_p, mu_p, mu_p, mu_p, mu_p,
      row(w0), row(a0), w2p, a2p, g2p, row(k_k), row(k_a), row(r_k), row(ln_w), row(ln_b))


def _lane_col(x, idx):
    return jnp.sum(jnp.where(_iota(x.shape, 1) == idx, x, 0.0), axis=1, keepdims=True)


def _transpose_rows(x):
    eye = (_iota((LANES, LANES), 0) == _iota((LANES, LANES), 1)).astype(F32)
    return _dot_nt(eye, x, HI)


def _mlstm_kernel(q_ref, k_ref, v_ref, og_ref, sm_ref, bias_ref, nw_ref, o_ref, c_ref, n_ref, m_ref, *, nchunk):
    @pl.when(pl.program_id(0) == 0)
    def _():
        c_ref[...] = jnp.zeros_like(c_ref)
        n_ref[...] = jnp.zeros_like(n_ref)
        m_ref[...] = jnp.zeros_like(m_ref)

    incl = _tril(CHUNK)
    tril = incl.astype(F32)
    lane = _iota((CHUNK, LANES), 1)
    is_f = jnp.logical_and(lane >= ML_HEADS, lane < 2 * ML_HEADS)
    scale = ML_DQK ** -0.5

    def chunk(c, carry):
        rows = pl.ds(pl.multiple_of(c * CHUNK, CHUNK), CHUNK)
        pre = sm_ref[rows, :] + bias_ref[...]
        cap = ML_CAP * jnp.tanh(pre / ML_CAP)
        x = jnp.where(is_f, _log_sigmoid(cap), cap)
        cum = _dot(tril, x, HI)
        x_t = _transpose_rows(x)
        cum_t = _transpose_rows(cum)
        for h in range(ML_HEADS):
            qs = pl.ds(h * ML_DQK, ML_DQK)
            vs = pl.ds(h * ML_DV, ML_DV)
            q = q_ref[rows, qs] * scale
            k = k_ref[rows, qs]
            v = v_ref[rows, vs]
            b_col = _lane_col(cum, ML_HEADS + h)
            i_col = _lane_col(x, h)
            b_row = cum_t[ML_HEADS + h:ML_HEADS + h + 1, :]
            i_row = x_t[h:h + 1, :]
            m_prev = m_ref[h:h + 1, 0:1]
            c_mat = c_ref[h]
            n_row = n_ref[h:h + 1, :]
            dmat = jnp.where(incl, b_col - b_row + i_row, -jnp.inf)
            inter = b_col + m_prev
            m_t = jnp.maximum(inter, jnp.max(dmat, axis=1, keepdims=True))
            w_inter = jnp.exp(inter - m_t)
            pmat = jnp.exp(dmat - m_t) * _dot_nt(q, k)
            num = _dot(pmat, v) + w_inter * _dot(q, c_mat)
            den = jnp.sum(pmat, axis=1, keepdims=True) + w_inter * jnp.sum(q * n_row, axis=1, keepdims=True)
            h_out = num / jnp.maximum(jnp.abs(den), jnp.exp(-m_t))
            b_last = b_row[:, CHUNK - 1:CHUNK]
            src_row = b_last - b_row + i_row
            src_col = b_last - b_col + i_col
            m_new = jnp.maximum(b_last + m_prev, jnp.max(src_row, axis=1, keepdims=True))
            ws = jnp.exp(src_col - m_new)
            decay = jnp.exp(b_last + m_prev - m_new)
            wk = ws * k
            c_ref[h] = decay * c_mat + _dot_tn(wk, v)
            n_ref[h:h + 1, :] = decay * n_row + jnp.sum(wk, axis=0, keepdims=True)
            m_ref[h:h + 1, :] = jnp.broadcast_to(m_new, (1, LANES))
            y = h_out * lax.rsqrt(jnp.mean(h_out * h_out, axis=-1, keepdims=True) + EPS) * nw_ref[:, vs]
            o_ref[rows, vs] = (y * _sigmoid(og_ref[rows, vs])).astype(BF16)
        return carry

    lax.fori_loop(0, nchunk, chunk, 0)


def _mlstm(p, bias_row, norm_w, rb):
    s = p.shape[0]
    nq = ML_HEADS * ML_DQK
    nv = ML_HEADS * ML_DV
    return pl.pallas_call(
        functools.partial(_mlstm_kernel, nchunk=rb // CHUNK),
        grid=(s // rb,),
        in_specs=[pl.BlockSpec((rb, nq), lambda r: (r, 0)),
                  pl.BlockSpec((rb, nq), lambda r: (r, 1)),
                  pl.BlockSpec((rb, nv), lambda r: (r, 1)),
                  pl.BlockSpec((rb, nv), lambda r: (r, 2)),
                  pl.BlockSpec((rb, LANES), lambda r: (r, OD_SMALL // LANES)),
                  pl.BlockSpec((1, LANES), lambda r: (0, 0)),
                  pl.BlockSpec((1, nv), lambda r: (0, 0))],
        out_specs=pl.BlockSpec((rb, nv), lambda r: (r, 0)),
        out_shape=jax.ShapeDtypeStruct((s, nv), BF16),
        scratch_shapes=[pltpu.VMEM((ML_HEADS, ML_DQK, ML_DV), F32),
                        pltpu.VMEM((8, ML_DQK), F32), pltpu.VMEM((8, LANES), F32)],
        compiler_params=_cparams(("arbitrary",)),
        name="mlstm",
    )(p, p, p, p, p, bias_row, norm_w.reshape(1, -1))


def _ssd_kernel(z_ref, xbc_ref, sm_ref, cw_ref, cb_ref, bias_ref, negA_ref, dvec_ref, nw_ref, o_ref,
                st_ref, xin_s, xc_s, *, nchunk):
    rb = z_ref.shape[0]
    pad = 8

    @pl.when(pl.program_id(0) == 0)
    def _():
        st_ref[...] = jnp.zeros_like(st_ref)
        xin_s[rb:rb + pad, :] = jnp.zeros((pad, xin_s.shape[1]), F32)

    xin_s[0:pad, :] = xin_s[rb:rb + pad, :]
    xin_s[pad:rb + pad, :] = xbc_ref[...]
    xin = xin_s[...]
    acc = cb_ref[...] + xin[pad:, :] * cw_ref[MB_CONV - 1:MB_CONV, :]
    for j in range(MB_CONV - 1):
        acc = acc + pltpu.roll(xin, MB_CONV - 1 - j, 0)[pad:, :] * cw_ref[j:j + 1, :]
    xc_s[...] = acc * _sigmoid(acc)

    incl = _tril(CHUNK)
    tril = incl.astype(F32)
    lane = _iota((1, LANES), 1)
    head0 = lane < MB_P
    sub0 = _iota((LANES, 1), 0) < MB_P
    hpg = MB_HEADS // MB_GROUPS
    gw = MB_DI // MB_GROUPS
    dt_lane0 = 2 * ML_HEADS

    def chunk(c, carry):
        rows = pl.ds(pl.multiple_of(c * CHUNK, CHUNK), CHUNK)
        dt = _softplus(sm_ref[rows, :] + bias_ref[...])
        da = negA_ref[...] * dt
        a_cum = _dot(tril, da, HI)
        a_t = _transpose_rows(a_cum)
        dt_t = _transpose_rows(dt)
        for g in range(MB_GROUPS):
            bm = xc_s[rows, pl.ds(MB_DI + g * MB_N, MB_N)]
            cm = xc_s[rows, pl.ds(MB_DI + MB_GROUPS * MB_N + g * MB_N, MB_N)]
            cb = _dot_nt(cm, bm)
            ys = []
            for j in range(hpg // 2):
                xs = pl.ds(g * gw + j * LANES, LANES)
                x2 = xc_s[rows, xs]
                sc, ea, ws, el = [], [], [], []
                for e in range(2):
                    ln = dt_lane0 + g * hpg + 2 * j + e
                    a_col = _lane_col(a_cum, ln)
                    a_row = a_t[ln:ln + 1, :]
                    dt_row = dt_t[ln:ln + 1, :]
                    dt_col = _lane_col(dt, ln)
                    seg = jnp.exp(jnp.where(incl, a_col - a_row, -jnp.inf))
                    sc.append(seg * cb * dt_row)
                    ea.append(jnp.exp(a_col))
                    a_last = a_row[:, CHUNK - 1:CHUNK]
                    ws.append(jnp.exp(a_last - a_col) * dt_col)
                    el.append(jnp.exp(a_last))
                si = g * (hpg // 2) + j
                st = st_ref[si]
                y = jnp.where(head0, _dot(sc[0], x2), _dot(sc[1], x2))
                y = y + jnp.where(head0, ea[0], ea[1]) * _dot_nt(cm, st)
                wsx = x2 * jnp.where(head0, ws[0], ws[1])
                st_ref[si] = jnp.where(sub0, el[0], el[1]) * st + _dot_tn(wsx, bm)
                ys.append(y + x2 * dvec_ref[:, xs])
            yg = jnp.concatenate(ys, axis=1)
            zg = z_ref[rows, pl.ds(g * gw, gw)]
            yg = yg * (zg * _sigmoid(zg))
            yg = yg * lax.rsqrt(jnp.mean(yg * yg, axis=-1, keepdims=True) + EPS) * nw_ref[:, pl.ds(g * gw, gw)]
            o_ref[rows, pl.ds(g * gw, gw)] = yg.astype(BF16)
        return carry

    lax.fori_loop(0, nchunk, chunk, 0)


def _ssd(p, conv_w, conv_b, bias_row, negA_row, d_row, norm_w, rb):
    s = p.shape[0]
    nx = conv_w.shape[1]
    return pl.pallas_call(
        functools.partial(_ssd_kernel, nchunk=rb // CHUNK),
        grid=(s // rb,),
        in_specs=[pl.BlockSpec((rb, MB_DI), lambda r: (r, 3)),
                  pl.BlockSpec((rb, nx), lambda r: (r, OD_XBC // nx)),
                  pl.BlockSpec((rb, LANES), lambda r: (r, OD_SMALL // LANES)),
                  pl.BlockSpec((MB_CONV, nx), lambda r: (0, 0)),
                  pl.BlockSpec((1, nx), lambda r: (0, 0)),
                  pl.BlockSpec((1, LANES), lambda r: (0, 0)),
                  pl.BlockSpec((1, LANES), lambda r: (0, 0)),
                  pl.BlockSpec((1, MB_DI), lambda r: (0, 0)),
                  pl.BlockSpec((1, MB_DI), lambda r: (0, 0))],
        out_specs=pl.BlockSpec((rb, MB_DI), lambda r: (r, 0)),
        out_shape=jax.ShapeDtypeStruct((s, MB_DI), BF16),
        scratch_shapes=[pltpu.VMEM((MB_HEADS // 2, 2 * MB_P, MB_N), F32),
                        pltpu.VMEM((rb + 8, nx), F32), pltpu.VMEM((rb, nx), F32)],
        compiler_params=_cparams(("arbitrary",)),
        name="ssd",
    )(p, p, p, conv_w, conv_b.reshape(1, -1), bias_row, negA_row, d_row, norm_w.reshape(1, -1))


def _pad_cols(w, n):
    return jnp.pad(w, ((0, 0), (0, n - w.shape[1])))


def _even_in_weight(w):
    k = w.shape[0]
    z = lambda n: jnp.zeros((k, n), w.dtype)
    return jnp.concatenate([w[:, :RW_TAIL + 128], z(128), w[:, RW_TAIL + 128:], z(96)], axis=1).astype(BF16)


def _odd_in_weight(w):
    head = jnp.concatenate([w[:, 0:2048], w[:, 2056:3080], w[:, 3080:4104], w[:, 2048:2056], w[:, 5640:5656]], axis=1)
    return jnp.concatenate([_pad_cols(head, OD_XBC), w[:, 4104:5640]], axis=1).astype(BF16)


def _even_mixer(h, norm_g, layer, w_in, w_out, lb_table, hg_norm, mu, w0, w2, a0, a2, g2, k_k, k_a, r_k, ln_w, ln_b,
                rb_hg, rb_rw):
    p = _norm_matmul(h, norm_g, _even_in_weight(w_in))
    o_hg = _hgrn2(p, lb_table, hg_norm, layer, rb_hg)
    lora = w2.shape[0]
    w2p = jnp.concatenate([w2, jnp.zeros_like(w2)], axis=0)
    a2p = jnp.concatenate([jnp.zeros_like(a2), a2], axis=0)
    g2p = jnp.pad(g2, ((0, 2 * LANES - g2.shape[0]), (0, 0)))
    assert 2 * lora == LANES
    o_rw = _rwkv7(p, mu, w0, w2p, a0, a2p, g2p, k_k, k_a, r_k.reshape(-1), ln_w, ln_b, rb_rw)
    return _matmul2_residual(o_hg, o_rw, w_out.astype(BF16), h)


def _odd_mixer(h, norm_g, w_in, w_out, i_bias, f_bias, ml_norm, conv_w, conv_b, dt_bias, a_log, d_vec, mb_norm,
               rb_ml, rb_ssd):
    p = _norm_matmul(h, norm_g, _odd_in_weight(w_in))
    small = lambda *xs: jnp.pad(jnp.concatenate(xs), (0, LANES - sum(x.shape[0] for x in xs))).reshape(1, LANES)
    zeros8 = jnp.zeros((2 * ML_HEADS,), F32)
    o_ml = _mlstm(p, small(i_bias, f_bias), ml_norm, rb_ml)
    neg_a = -jnp.exp(a_log.astype(F32))
    y = _ssd(p, conv_w, conv_b, small(zeros8, dt_bias), small(zeros8, neg_a),
             jnp.repeat(d_vec, MB_P).reshape(1, -1), mb_norm, rb_ssd)
    return _matmul2_residual(o_ml, y, w_out.astype(BF16), h)


def _ffn(h, norm_g, w_up, w_down):
    act = _norm_swiglu(h, norm_g, w_up.astype(BF16))
    return _matmul_residual(act, w_down.astype(BF16), h)


def kernel(x, norm_mix, norm_ffn, norm_final, w_in_even, w_out_even, hg_lb_table, hg_norm, rw_mu, rw_w0, rw_w2, rw_a0, rw_a2, rw_g2, rw_k_k, rw_k_a, rw_r_k, rw_ln_w, rw_ln_b, w_in_odd, w_out_odd, ml_i_bias, ml_f_bias, ml_norm, mb_conv_w, mb_conv_b, mb_dt_bias, mb_A_log, mb_D, mb_norm, ffn_w_up, ffn_w_down):
    b, s, d = x.shape
    assert b == 1 and d == D_MODEL and s % TM == 0
    rb_hg, rb_rw, rb_ml, rb_ssd = min(s, 256), min(s, 512), min(s, 256), min(s, 256)
    h = x.reshape(s, d)
    for layer in range(DEPTH):
        j = layer // 2
        if layer % 2 == 0:
            h = _even_mixer(h, norm_mix[layer], layer, w_in_even[j], w_out_even[j], hg_lb_table, hg_norm[j],
                            rw_mu[j], rw_w0[j], rw_w2[j], rw_a0[j], rw_a2[j], rw_g2[j], rw_k_k[j], rw_k_a[j],
                            rw_r_k[j], rw_ln_w[j], rw_ln_b[j], rb_hg, rb_rw)
        else:
            h = _odd_mixer(h, norm_mix[layer], w_in_odd[j], w_out_odd[j], ml_i_bias[j], ml_f_bias[j], ml_norm[j],
                           mb_conv_w[j], mb_conv_b[j], mb_dt_bias[j], mb_A_log[j], mb_D[j], mb_norm[j],
                           rb_ml, rb_ssd)
        h = _ffn(h, norm_ffn[layer], ffn_w_up[layer], ffn_w_down[layer])
    return _rmsnorm(h, norm_final).reshape(b, s, d)
```

```python
import functools

import jax
import jax.numpy as jnp
from jax import lax
from jax.experimental import pallas as pl
from jax.experimental.pallas import tpu as pltpu

F32 = jnp.float32
BF16 = jnp.bfloat16
HI = lax.Precision.HIGHEST

D_MODEL = 2048
DEPTH = 4
CHUNK = 64
EPS = 1e-6
LANES = 128
SUB = 16

HG_HEADS, HG_D = 8, 128
RW_HEADS, RW_N, RW_W = 16, 64, 1024
RW_GN_EPS = 64e-5
ML_HEADS, ML_DQK, ML_DV = 4, 128, 256
ML_CAP = 15.0
MB_HEADS, MB_P, MB_N, MB_GROUPS, MB_CONV = 16, 64, 128, 2, 4
MB_DI = MB_HEADS * MB_P
D_FF = 5632

NP_EVEN = 7680
NP_ODD = 6144
RW_TAIL = 7168
OD_SMALL = 4096
OD_XBC = 4608

TM = 1024
TN = 512
VMEM_LIMIT = 56 * 1024 * 1024


def _cparams(sem):
    return pltpu.CompilerParams(dimension_semantics=sem, vmem_limit_bytes=VMEM_LIMIT)


def _dot(a, b, prec=None):
    return jnp.dot(a, b, preferred_element_type=F32, precision=prec)


def _dot_nt(a, b, prec=None):
    return lax.dot_general(a, b, (((1,), (1,)), ((), ())), preferred_element_type=F32, precision=prec)


def _dot_tn(a, b, prec=None):
    return lax.dot_general(a, b, (((0,), (0,)), ((), ())), preferred_element_type=F32, precision=prec)


def _sigmoid(x):
    return 1.0 / (1.0 + jnp.exp(-x))


def _log_sigmoid(x):
    return jnp.minimum(x, 0.0) - jnp.log1p(jnp.exp(-jnp.abs(x)))


def _softplus(x):
    return jnp.maximum(x, 0.0) + jnp.log1p(jnp.exp(-jnp.abs(x)))


def _iota(shape, dim):
    return lax.broadcasted_iota(jnp.int32, shape, dim)


def _tril(n, strict=False):
    r, c = _iota((n, n), 0), _iota((n, n), 1)
    return (r > c) if strict else (r >= c)


def _rms_rows(x, g):
    return (x * lax.rsqrt(jnp.mean(x * x, axis=-1, keepdims=True) + EPS)) * g


def _norm_mm_kernel(x_ref, g_ref, w_ref, o_ref, xn_ref):
    @pl.when(pl.program_id(1) == 0)
    def _():
        xn_ref[...] = _rms_rows(x_ref[...], g_ref[...]).astype(BF16)

    o_ref[...] = _dot(xn_ref[...], w_ref[...])


def _norm_matmul(x, g, gl, w, wl):
    s, k = x.shape
    n = w.shape[2]
    return pl.pallas_call(
        _norm_mm_kernel,
        grid=(s // TM, n // TN),
        in_specs=[pl.BlockSpec((TM, k), lambda i, j: (i, 0)),
                  pl.BlockSpec((None, 1, k), lambda i, j: (gl, 0, 0)),
                  pl.BlockSpec((None, k, TN), lambda i, j: (wl, 0, j))],
        out_specs=pl.BlockSpec((TM, TN), lambda i, j: (i, j)),
        out_shape=jax.ShapeDtypeStruct((s, n), F32),
        scratch_shapes=[pltpu.VMEM((TM, k), BF16)],
        compiler_params=_cparams(("parallel", "arbitrary")),
        name="norm_in_proj",
    )(x, g, w)


def _norm_swiglu_kernel(x_ref, g_ref, wg_ref, wu_ref, o_ref, xn_ref):
    @pl.when(pl.program_id(1) == 0)
    def _():
        xn_ref[...] = _rms_rows(x_ref[...], g_ref[...]).astype(BF16)

    xn = xn_ref[...]
    gate = _dot(xn, wg_ref[...])
    up = _dot(xn, wu_ref[...])
    o_ref[...] = (gate * _sigmoid(gate) * up).astype(BF16)


def _norm_swiglu(x, g, w_up, layer):
    s, k = x.shape
    nj = D_FF // TN
    return pl.pallas_call(
        _norm_swiglu_kernel,
        grid=(s // TM, nj),
        in_specs=[pl.BlockSpec((TM, k), lambda i, j: (i, 0)),
                  pl.BlockSpec((None, 1, k), lambda i, j: (layer, 0, 0)),
                  pl.BlockSpec((None, k, TN), lambda i, j: (layer, 0, j)),
                  pl.BlockSpec((None, k, TN), lambda i, j: (layer, 0, j + nj))],
        out_specs=pl.BlockSpec((TM, TN), lambda i, j: (i, j)),
        out_shape=jax.ShapeDtypeStruct((s, D_FF), BF16),
        scratch_shapes=[pltpu.VMEM((TM, k), BF16)],
        compiler_params=_cparams(("parallel", "arbitrary")),
        name="norm_ffn_up",
    )(x, g, w_up, w_up)


def _mm_res_kernel(x_ref, w_ref, r_ref, o_ref):
    o_ref[...] = r_ref[...] + _dot(x_ref[...], w_ref[...])


def _matmul_residual(x, w, wl, res):
    s, k = x.shape
    n = w.shape[2]
    return pl.pallas_call(
        _mm_res_kernel,
        grid=(s // TM, n // TN),
        in_specs=[pl.BlockSpec((TM, k), lambda i, j: (i, 0)),
                  pl.BlockSpec((None, k, TN), lambda i, j: (wl, 0, j)),
                  pl.BlockSpec((TM, TN), lambda i, j: (i, j))],
        out_specs=pl.BlockSpec((TM, TN), lambda i, j: (i, j)),
        out_shape=jax.ShapeDtypeStruct((s, n), F32),
        compiler_params=_cparams(("parallel", "parallel")),
        name="proj_residual",
    )(x, w, res)


def _mm2_res_kernel(x1_ref, x2_ref, w1_ref, w2_ref, r_ref, o_ref):
    o_ref[...] = r_ref[...] + (_dot(x1_ref[...], w1_ref[...]) + _dot(x2_ref[...], w2_ref[...]))


def _matmul2_residual(x1, x2, w, wl, res):
    s, k = x1.shape
    n = w.shape[2]
    assert x2.shape == (s, k) and w.shape[1] == 2 * k
    return pl.pallas_call(
        _mm2_res_kernel,
        grid=(s // TM, n // TN),
        in_specs=[pl.BlockSpec((TM, k), lambda i, j: (i, 0)),
                  pl.BlockSpec((TM, k), lambda i, j: (i, 0)),
                  pl.BlockSpec((None, k, TN), lambda i, j: (wl, 0, j)),
                  pl.BlockSpec((None, k, TN), lambda i, j: (wl, 1, j)),
                  pl.BlockSpec((TM, TN), lambda i, j: (i, j))],
        out_specs=pl.BlockSpec((TM, TN), lambda i, j: (i, j)),
        out_shape=jax.ShapeDtypeStruct((s, n), F32),
        compiler_params=_cparams(("parallel", "parallel")),
        name="out_proj_residual",
    )(x1, x2, w, w, res)


def _rmsnorm_kernel(x_ref, g_ref, o_ref):
    o_ref[...] = _rms_rows(x_ref[...], g_ref[...])


def _rmsnorm(x, g):
    s, k = x.shape
    return pl.pallas_call(
        _rmsnorm_kernel,
        grid=(s // TM,),
        in_specs=[pl.BlockSpec((TM, k), lambda i: (i, 0)), pl.BlockSpec((1, k), lambda i: (0, 0))],
        out_specs=pl.BlockSpec((TM, k), lambda i: (i, 0)),
        out_shape=jax.ShapeDtypeStruct((s, k), F32),
        compiler_params=_cparams(("parallel",)),
        name="final_norm",
    )(x, g.reshape(1, k))


def _hgrn2_kernel(q_ref, f_ref, v_ref, g_ref, lbt_ref, nw_ref, o_ref, st_ref, *, layer, nchunk):
    @pl.when(pl.program_id(1) == 0)
    def _():
        st_ref[...] = jnp.zeros_like(st_ref)

    t = lbt_ref[...]
    e = jnp.exp(t - jnp.max(t, axis=0, keepdims=True))
    sm = e / jnp.sum(e, axis=0, keepdims=True)
    lb = jnp.zeros((1, HG_D), F32)
    for i in range(1, layer + 1):
        lb = lb + sm[i:i + 1, :]
    log_lb = jnp.log(lb)
    log_1m = jnp.log1p(-lb)
    nw = nw_ref[...]

    tril = _tril(CHUNK).astype(BF16)
    rowid = _iota((SUB, HG_D), 0)
    nsub = CHUNK // SUB
    cs = range(nchunk)
    rows = [slice(c * CHUNK, (c + 1) * CHUNK) for c in cs]

    fp = [f_ref[r, :] for r in rows]
    v = [v_ref[r, :] for r in rows]
    b2 = [log_1m + _log_sigmoid(x) for x in fp]
    log_f = [jnp.maximum(log_lb, x) + jnp.log1p(jnp.exp(-jnp.abs(log_lb - x))) for x in b2]
    k = [(1.0 - lb) * _sigmoid(-x) for x in fp]
    q = [x * _sigmoid(x) for x in (q_ref[r, :] for r in rows)]
    bc = [_cumsum_rows(tril, x) for x in log_f]
    b_last = [x[CHUNK - 1:CHUNK, :] for x in bc]
    kv = [_mm(v[c], k[c] * jnp.exp(b_last[c] - bc[c]), "bf16", "tn") for c in cs]
    st = st_ref[...]
    starts = []
    for c in cs:
        starts.append(st)
        st = jnp.exp(b_last[c]) * st + kv[c]
    st_ref[...] = st
    acc = [_mm(q[c] * jnp.exp(bc[c]), starts[c], "bf16", "nt") for c in cs]
    acc = [[a[blk * SUB:(blk + 1) * SUB] for blk in range(nsub)] for a in acc]
    for blk in range(1, nsub):
        lo = blk * SUB
        ref_b = [x[lo - 1:lo, :] for x in bc]
        qx = [q[c][lo:lo + SUB] * jnp.exp(bc[c][lo:lo + SUB] - ref_b[c]) for c in cs]
        kx = [k[c][0:lo] * jnp.exp(ref_b[c] - bc[c][0:lo]) for c in cs]
        sc = [_mm(qx[c], kx[c], "bf16", "nt") for c in cs]
        od = [_mm(sc[c], v[c][0:lo], "bf16") for c in cs]
        for c in cs:
            acc[c][blk] = acc[c][blk] + od[c]
    for s in range(SUB):
        for c in cs:
            for blk in range(nsub):
                lo = blk * SUB
                b_i = bc[c][lo:lo + SUB]
                d = b_i - bc[c][lo + s:lo + s + 1, :]
                if s > 0:
                    d = jnp.where(rowid >= s, d, -jnp.inf)
                col = jnp.sum(q[c][lo:lo + SUB] * k[c][lo + s:lo + s + 1, :] * jnp.exp(d), axis=1, keepdims=True)
                acc[c][blk] = acc[c][blk] + col * v[c][lo + s:lo + s + 1, :]
    for c in cs:
        o = jnp.concatenate(acc[c], axis=0)
        y = o * lax.rsqrt(jnp.mean(o * o, axis=-1, keepdims=True) + EPS) * nw
        o_ref[rows[c], :] = (y * _sigmoid(g_ref[rows[c], :])).astype(BF16)


def _hgrn2(p, lb_table, norm_w, layer, rb):
    s = p.shape[0]
    nb = HG_HEADS
    col = lambda off: pl.BlockSpec((rb, HG_D), lambda h, r, off=off: (r, off + h))
    return pl.pallas_call(
        functools.partial(_hgrn2_kernel, layer=layer, nchunk=rb // CHUNK),
        grid=(HG_HEADS, s // rb),
        in_specs=[col(0), col(nb), col(2 * nb), col(3 * nb),
                  pl.BlockSpec((DEPTH, HG_D), lambda h, r: (0, h)),
                  pl.BlockSpec((None, 1, HG_D), lambda h, r: (layer // 2, 0, h))],
        out_specs=pl.BlockSpec((rb, HG_D), lambda h, r: (r, h)),
        out_shape=jax.ShapeDtypeStruct((s, HG_HEADS * HG_D), BF16),
        scratch_shapes=[pltpu.VMEM((HG_D, HG_D), F32)],
        compiler_params=_cparams(("parallel", "arbitrary")),
        name="hgrn2",
    )(p, p, p, p, lb_table, norm_w)


def _shift_lerp(x, prev_row, mu):
    rolled = pltpu.roll(x, 1, 0)
    shifted = jnp.where(_iota(x.shape, 0) == 0, prev_row, rolled)
    return x + (shifted - x) * mu


def _split_bf16(x):
    hi = x.astype(BF16)
    lo = (x - hi.astype(F32)).astype(BF16)
    return hi, lo


def _mm(a, b, mode, kind="nn"):
    dot = {"nn": _dot, "nt": _dot_nt, "tn": _dot_tn}[kind]
    if mode == "hi":
        return dot(a, b, HI)
    if mode == "bf16":
        return dot(a.astype(BF16), b.astype(BF16))
    ah, al = _split_bf16(a)
    bh, bl = _split_bf16(b)
    return dot(ah, bh) + (dot(ah, bl) + dot(al, bh))


def _cumsum_rows(tril_bf16, x):
    hi, lo = _split_bf16(x)
    lo2 = (x - hi.astype(F32) - lo.astype(F32)).astype(BF16)
    return _dot(tril_bf16, hi) + (_dot(tril_bf16, lo) + _dot(tril_bf16, lo2))


def _neumann_inverse(mats, mode):
    n = mats[0].shape[0]
    eye = (_iota((n, n), 0) == _iota((n, n), 1)).astype(F32)
    sq = lambda xs: [_mm(x, x, mode) for x in xs]
    mul = lambda xs, ys: [_mm(x, y, mode) for x, y in zip(xs, ys)]
    one_plus = lambda xs: [eye + x for x in xs]
    p1 = sq(mats)
    m01 = mul(one_plus(mats), one_plus(p1))
    p2 = sq(p1)
    p3 = sq(p2)
    m23 = mul(one_plus(p2), one_plus(p3))
    p4 = sq(p3)
    m03 = mul(m01, m23)
    p5 = sq(p4)
    m45 = mul(one_plus(p4), one_plus(p5))
    return mul(m03, m45)


RW_PREC = dict(score="bf16", neumann="bf16", apply="bf16", state="bf16")


def _rwkv7_kernel(r_ref, k_ref, v_ref, wa_ref, gl_ref,
                  mu_r_ref, mu_k_ref, mu_v_ref, mu_wa_ref, mu_gl_ref,
                  w0_ref, a0_ref, w2_ref, a2_ref, g2_ref, kk_ref, ka_ref, rk_ref, lnw_ref, lnb_ref,
                  o_ref, st_ref, prev_ref, *, nchunk):
    rb = r_ref.shape[0]
    pc = RW_PREC

    @pl.when(pl.program_id(1) == 0)
    def _():
        st_ref[...] = jnp.zeros_like(st_ref)
        prev_ref[...] = jnp.zeros_like(prev_ref)

    lane = _iota((1, LANES), 1)
    head0 = lane < RW_N

    def head_sum(x):
        s0 = jnp.sum(jnp.where(head0, x, 0.0), axis=1, keepdims=True)
        s1 = jnp.sum(jnp.where(head0, 0.0, x), axis=1, keepdims=True)
        return jnp.where(head0, s0, s1)

    r_raw, k_raw, v_raw, wa_raw, gl_raw = r_ref[...], k_ref[...], v_ref[...], wa_ref[...], gl_ref[...]
    r = _shift_lerp(r_raw, prev_ref[0:1, 0:LANES], mu_r_ref[...])
    k = _shift_lerp(k_raw, prev_ref[1:2, 0:LANES], mu_k_ref[...])
    v = _shift_lerp(v_raw, prev_ref[2:3, 0:LANES], mu_v_ref[...])
    wa = _shift_lerp(wa_raw, prev_ref[3:4, 0:LANES], mu_wa_ref[...])
    gl = _shift_lerp(gl_raw, prev_ref[4:5, :], mu_gl_ref[...])
    prev_ref[0:1, 0:LANES] = r_raw[rb - 1:rb]
    prev_ref[1:2, 0:LANES] = k_raw[rb - 1:rb]
    prev_ref[2:3, 0:LANES] = v_raw[rb - 1:rb]
    prev_ref[3:4, 0:LANES] = wa_raw[rb - 1:rb]
    prev_ref[4:5, :] = gl_raw[rb - 1:rb]

    w_log = -_softplus(-(w0_ref[...] + _dot(jnp.tanh(wa), w2_ref[...]))) - 0.5
    lw = -jnp.exp(w_log)
    a = _sigmoid(a0_ref[...] + _dot(wa, a2_ref[...]))
    gate = _dot(_sigmoid(gl), g2_ref[...])
    kk = k * kk_ref[...]
    kk = kk / jnp.maximum(jnp.sqrt(head_sum(kk * kk)), 1e-12)
    k = k * (1.0 + (a - 1.0) * ka_ref[...])
    bonus = head_sum(r * k * rk_ref[...]) * v

    head1 = jnp.logical_not(head0)
    bd = (_iota((LANES, LANES), 0) < RW_N) == (_iota((LANES, LANES), 1) < RW_N)
    gc = 1
    gr = gc * CHUNK
    gs = 2 * gr
    ri, ci = _iota((gs, gs), 0), _iota((gs, gs), 1)
    same = (ri // CHUNK) == (ci // CHUNK)
    strict = jnp.logical_and(same, ri > ci)
    incl = jnp.logical_and(same, ri >= ci)
    rj, cj = _iota((gr, gr), 0), _iota((gr, gr), 1)
    tril = jnp.logical_and((rj // CHUNK) == (cj // CHUNK), rj >= cj).astype(BF16)

    def stack_heads(x, masked):
        parts = []
        for c in range(gc):
            xc = x[c * CHUNK:(c + 1) * CHUNK]
            parts += [jnp.where(head0, xc, 0.0), jnp.where(head1, xc, 0.0)] if masked else [xc, xc]
        return jnp.concatenate(parts, axis=0)

    def unstack_heads(x, c):
        lo = 2 * c * CHUNK
        return jnp.where(head0 if x.shape[1] == LANES else jnp.concatenate([head0, head0], axis=1),
                         x[lo:lo + CHUNK], x[lo + CHUNK:lo + 2 * CHUNK])

    grps = range(nchunk // gc)
    rows = [slice(i * gr, (i + 1) * gr) for i in grps]
    g = [_cumsum_rows(tril, lw[s]) for s in rows]
    ieg = [jnp.exp(-x) for x in g]
    rt = [r[s] * jnp.exp(x) for s, x in zip(rows, g)]
    at = [-kk[s] * jnp.exp(x - lw[s]) for s, x in zip(rows, g)]
    bt = [kk[s] * a[s] * e for s, e in zip(rows, ieg)]
    kt = [k[s] * e for s, e in zip(rows, ieg)]
    l_a = [stack_heads(x, True) for x in at]
    l_r = [stack_heads(x, True) for x in rt]
    r_b = [stack_heads(x, True) for x in bt]
    r_k = [stack_heads(x, True) for x in kt]
    a_ab = [jnp.where(strict, _mm(x, y, pc["score"], "nt"), 0.0) for x, y in zip(l_a, r_b)]
    a_ak = [jnp.where(strict, _mm(x, y, pc["score"], "nt"), 0.0) for x, y in zip(l_a, r_k)]
    rab = [jnp.where(incl, _mm(x, y, pc["score"], "nt"), 0.0) for x, y in zip(l_r, r_b)]
    rak = [jnp.where(incl, _mm(x, y, pc["score"], "nt"), 0.0) for x, y in zip(l_r, r_k)]
    v_st = [stack_heads(v[s], False) for s in rows]
    aakv = [_mm(x, y, pc["apply"]) for x, y in zip(a_ak, v_st)]
    rakv = [_mm(x, y, pc["apply"]) for x, y in zip(rak, v_st)]
    t_inv = _neumann_inverse(a_ab, pc["neumann"])
    xs = [_mm(t, jnp.concatenate([stack_heads(x, False), y], axis=1), pc["apply"])
          for t, x, y in zip(t_inv, at, aakv)]
    pre = []
    for i in grps:
        for c in range(gc):
            cs = slice(c * CHUNK, (c + 1) * CHUNK)
            egl = jnp.exp(g[i][(c + 1) * CHUNK - 1:(c + 1) * CHUNK, :])
            xc = unstack_heads(xs[i], c)
            ta, tav = xc[:, :LANES], xc[:, LANES:]
            bte, kte = bt[i][cs] * egl, kt[i][cs] * egl
            blk = slice(2 * c * CHUNK, 2 * (c + 1) * CHUNK)
            pre.append(dict(ta=ta, tav=tav, rt=rt[i][cs], rab=rab[i][blk, blk], rakv=unstack_heads(rakv[i], c),
                            egl=egl, bte=bte, lhs=jnp.concatenate([tav, v[rows[i]][cs]], axis=0),
                            rhs=jnp.concatenate([bte, kte], axis=0)))
    for q in pre:
        q["m"] = jnp.where(bd, _mm(q["ta"], q["bte"], pc["state"], "tn"), 0.0)
    for q in pre:
        q["c"] = jnp.where(bd, _mm(q["lhs"], q["rhs"], pc["state"], "tn"), 0.0)

    st = st_ref[...]
    starts = []
    for q in pre:
        starts.append(st)
        st = st * q["egl"] + _mm(st, q["m"], pc["state"]) + q["c"]
    st_ref[...] = st
    outs = []
    for q, s0 in zip(pre, starts):
        u = _mm(q["ta"], s0, pc["state"], "nt") + q["tav"]
        o_in = _mm(q["rt"], s0, pc["state"], "nt")
        y = _mm(q["rab"], jnp.concatenate([jnp.where(head0, u, 0.0), jnp.where(head1, u, 0.0)], axis=0), pc["apply"])
        outs.append(o_in + (y[:CHUNK] + y[CHUNK:]) + q["rakv"])

    o = jnp.concatenate(outs, axis=0)
    mu = head_sum(o) * (1.0 / RW_N)
    d = o - mu
    var = head_sum(d * d) * (1.0 / RW_N)
    y = d * lax.rsqrt(var + RW_GN_EPS) * lnw_ref[...] + lnb_ref[...]
    o_ref[...] = ((y + bonus) * gate).astype(BF16)


def _rwkv7_params(mu, w0, w2, a0, a2, g2, k_k, k_a, r_k, ln_w, ln_b):
    n = mu.shape[0]
    row = lambda x: x.reshape(n, 1, -1)
    assert 2 * w2.shape[1] == LANES and 2 * a2.shape[1] == LANES
    z = lambda c: jnp.zeros((n, c), F32)
    mu_p = jnp.concatenate([mu[:, :3 * RW_W + 128], z(128), mu[:, 3 * RW_W + 128:], z(96)], axis=1)
    return dict(mu=row(mu_p), w0=row(w0), a0=row(a0),
                w2=jnp.concatenate([w2, jnp.zeros_like(w2)], axis=1),
                a2=jnp.concatenate([jnp.zeros_like(a2), a2], axis=1),
                g2=jnp.pad(g2, ((0, 0), (0, 2 * LANES - g2.shape[1]), (0, 0))),
                k_k=row(k_k), k_a=row(k_a), r_k=row(r_k), ln_w=row(ln_w), ln_b=row(ln_b))


def _rwkv7(p, prm, j, rb):
    s = p.shape[0]
    base = 4 * HG_HEADS
    nb = RW_W // LANES
    col = lambda off: pl.BlockSpec((rb, LANES), lambda h, r, off=off: (r, off + h))
    par = pl.BlockSpec((None, 1, LANES), lambda h, r: (j, 0, h))
    par_off = lambda off: pl.BlockSpec((None, 1, LANES), lambda h, r, off=off: (j, 0, off + h))
    lora = lambda rows: pl.BlockSpec((None, rows, LANES), lambda h, r: (j, 0, h))
    tail = RW_TAIL // LANES
    return pl.pallas_call(
        functools.partial(_rwkv7_kernel, nchunk=rb // CHUNK),
        grid=(nb, s // rb),
        in_specs=[col(base), col(base + nb), col(base + 2 * nb),
                  pl.BlockSpec((rb, LANES), lambda h, r: (r, tail)),
                  pl.BlockSpec((rb, 2 * LANES), lambda h, r: (r, tail // 2 + 1)),
                  par, par_off(nb), par_off(2 * nb),
                  pl.BlockSpec((None, 1, LANES), lambda h, r: (j, 0, 3 * nb)),
                  pl.BlockSpec((None, 1, 2 * LANES), lambda h, r: (j, 0, 3 * nb // 2 + 1)),
                  par, par, lora(LANES), lora(LANES), lora(2 * LANES),
                  par, par, par, par, par],
        out_specs=pl.BlockSpec((rb, LANES), lambda h, r: (r, h)),
        out_shape=jax.ShapeDtypeStruct((s, RW_W), BF16),
        scratch_shapes=[pltpu.VMEM((LANES, LANES), F32), pltpu.VMEM((8, 2 * LANES), F32)],
        compiler_params=_cparams(("parallel", "arbitrary")),
        name="rwkv7",
    )(p, p, p, p, p, prm["mu"], prm["mu"], prm["mu"], prm["mu"], prm["mu"],
      prm["w0"], prm["a0"], prm["w2"], prm["a2"], prm["g2"],
      prm["k_k"], prm["k_a"], prm["r_k"], prm["ln_w"], prm["ln_b"])


def _lane_col(x, idx):
    return jnp.sum(jnp.where(_iota(x.shape, 1) == idx, x, 0.0), axis=1, keepdims=True)


def _transpose_rows(x):
    eye = (_iota((LANES, LANES), 0) == _iota((LANES, LANES), 1)).astype(BF16)
    hi, lo = _split_bf16(x)
    lo2 = (x - hi.astype(F32) - lo.astype(F32)).astype(BF16)
    return _dot_nt(eye, hi) + (_dot_nt(eye, lo) + _dot_nt(eye, lo2))


def _mlstm_kernel(q_ref, k_ref, v_ref, og_ref, sm_ref, bias_ref, nw_ref, o_ref, c_ref, n_ref, m_ref, *, nchunk):
    @pl.when(pl.program_id(0) == 0)
    def _():
        c_ref[...] = jnp.zeros_like(c_ref)
        n_ref[...] = jnp.zeros_like(n_ref)
        m_ref[...] = jnp.zeros_like(m_ref)

    incl = _tril(CHUNK)
    tril = incl.astype(BF16)
    lane = _iota((CHUNK, LANES), 1)
    is_f = jnp.logical_and(lane >= ML_HEADS, lane < 2 * ML_HEADS)
    scale = ML_DQK ** -0.5
    cs = range(nchunk)
    hs = range(ML_HEADS)
    rows = [slice(c * CHUNK, (c + 1) * CHUNK) for c in cs]
    ch = [(c, h) for c in cs for h in hs]

    pre = [sm_ref[r, :] + bias_ref[...] for r in rows]
    cap = [ML_CAP * jnp.tanh(x / ML_CAP) for x in pre]
    x = [jnp.where(is_f, _log_sigmoid(y), y) for y in cap]
    cum = [_cumsum_rows(tril, y) for y in x]
    x_t = [_transpose_rows(y) for y in x]
    cum_t = [_transpose_rows(y) for y in cum]

    q = {(c, h): q_ref[rows[c], h * ML_DQK:(h + 1) * ML_DQK] * scale for c, h in ch}
    k = {(c, h): k_ref[rows[c], h * ML_DQK:(h + 1) * ML_DQK] for c, h in ch}
    v = {(c, h): v_ref[rows[c], h * ML_DV:(h + 1) * ML_DV] for c, h in ch}
    b_col = {(c, h): _lane_col(cum[c], ML_HEADS + h) for c, h in ch}
    i_col = {(c, h): _lane_col(x[c], h) for c, h in ch}
    b_row = {(c, h): cum_t[c][ML_HEADS + h:ML_HEADS + h + 1, :] for c, h in ch}
    i_row = {(c, h): x_t[c][h:h + 1, :] for c, h in ch}
    qk = {e: _mm(q[e], k[e], "bf16", "nt") for e in ch}
    dmat = {e: jnp.where(incl, b_col[e] - b_row[e] + i_row[e], -jnp.inf) for e in ch}
    dmax = {e: jnp.max(dmat[e], axis=1, keepdims=True) for e in ch}
    b_last = {e: b_row[e][:, CHUNK - 1:CHUNK] for e in ch}
    src_row = {e: b_last[e] - b_row[e] + i_row[e] for e in ch}
    src_col = {e: b_last[e] - b_col[e] + i_col[e] for e in ch}
    src_max = {e: jnp.max(src_row[e], axis=1, keepdims=True) for e in ch}

    m_prev, m_new = {}, {}
    for h in hs:
        m = m_ref[h:h + 1, 0:1]
        for c in cs:
            m_prev[c, h] = m
            m = jnp.maximum(b_last[c, h] + m, src_max[c, h])
            m_new[c, h] = m
        m_ref[h:h + 1, :] = jnp.broadcast_to(m, (1, LANES))
    inter = {e: b_col[e] + m_prev[e] for e in ch}
    m_t = {e: jnp.maximum(inter[e], dmax[e]) for e in ch}
    w_inter = {e: jnp.exp(inter[e] - m_t[e]) for e in ch}
    pmat = {e: jnp.exp(dmat[e] - m_t[e]) * qk[e] for e in ch}
    num = {e: _mm(pmat[e], v[e], "bf16") for e in ch}
    den = {e: jnp.sum(pmat[e], axis=1, keepdims=True) for e in ch}
    decay = {e: jnp.exp(b_last[e] + m_prev[e] - m_new[e]) for e in ch}
    wk = {e: jnp.exp(src_col[e] - m_new[e]) * k[e] for e in ch}
    kv = {e: _mm(wk[e], v[e], "bf16", "tn") for e in ch}
    ksum = {e: jnp.sum(wk[e], axis=0, keepdims=True) for e in ch}

    c_start, n_start = {}, {}
    for h in hs:
        c_mat, n_row = c_ref[h], n_ref[h:h + 1, :]
        for c in cs:
            c_start[c, h], n_start[c, h] = c_mat, n_row
            c_mat = decay[c, h] * c_mat + kv[c, h]
            n_row = decay[c, h] * n_row + ksum[c, h]
        c_ref[h] = c_mat
        n_ref[h:h + 1, :] = n_row

    qc = {e: _mm(q[e], c_start[e], "bf16") for e in ch}
    vs = {h: slice(h * ML_DV, (h + 1) * ML_DV) for h in hs}
    qn = {e: jnp.sum(q[e] * n_start[e], axis=1, keepdims=True) for e in ch}
    num = {e: num[e] + w_inter[e] * qc[e] for e in ch}
    den = {e: den[e] + w_inter[e] * qn[e] for e in ch}
    h_out = {e: num[e] / jnp.maximum(jnp.abs(den[e]), jnp.exp(-m_t[e])) for e in ch}
    ms = {e: jnp.mean(h_out[e] * h_out[e], axis=-1, keepdims=True) for e in ch}
    gate = {(c, h): _sigmoid(og_ref[rows[c], vs[h]]) for c, h in ch}
    y = {(c, h): h_out[c, h] * lax.rsqrt(ms[c, h] + EPS) * nw_ref[:, vs[h]] * gate[c, h] for c, h in ch}
    for c, h in ch:
        o_ref[rows[c], vs[h]] = y[c, h].astype(BF16)


def _mlstm(p, bias_row, norm_w, j, rb):
    s = p.shape[0]
    nq = ML_HEADS * ML_DQK
    nv = ML_HEADS * ML_DV
    return pl.pallas_call(
        functools.partial(_mlstm_kernel, nchunk=rb // CHUNK),
        grid=(s // rb,),
        in_specs=[pl.BlockSpec((rb, nq), lambda r: (r, 0)),
                  pl.BlockSpec((rb, nq), lambda r: (r, 1)),
                  pl.BlockSpec((rb, nv), lambda r: (r, 1)),
                  pl.BlockSpec((rb, nv), lambda r: (r, 2)),
                  pl.BlockSpec((rb, LANES), lambda r: (r, OD_SMALL // LANES)),
                  pl.BlockSpec((None, 1, LANES), lambda r: (j, 0, 0)),
                  pl.BlockSpec((None, 1, nv), lambda r: (j, 0, 0))],
        out_specs=pl.BlockSpec((rb, nv), lambda r: (r, 0)),
        out_shape=jax.ShapeDtypeStruct((s, nv), BF16),
        scratch_shapes=[pltpu.VMEM((ML_HEADS, ML_DQK, ML_DV), F32),
                        pltpu.VMEM((8, ML_DQK), F32), pltpu.VMEM((8, LANES), F32)],
        compiler_params=_cparams(("arbitrary",)),
        name="mlstm",
    )(p, p, p, p, p, bias_row, norm_w)


def _ssd_kernel(z_ref, xbc_ref, sm_ref, cw_ref, cb_ref, bias_ref, negA_ref, dvec_ref, nw_ref, o_ref,
                st_ref, xin_s, xc_s, *, nchunk):
    rb = z_ref.shape[0]
    pad = 8

    @pl.when(pl.program_id(0) == 0)
    def _():
        st_ref[...] = jnp.zeros_like(st_ref)
        xin_s[rb:rb + pad, :] = jnp.zeros((pad, xin_s.shape[1]), F32)

    xin_s[0:pad, :] = xin_s[rb:rb + pad, :]
    xin_s[pad:rb + pad, :] = xbc_ref[...]
    xin = xin_s[...]
    acc = cb_ref[...] + xin[pad:, :] * cw_ref[MB_CONV - 1:MB_CONV, :]
    for j in range(MB_CONV - 1):
        acc = acc + pltpu.roll(xin, MB_CONV - 1 - j, 0)[pad:, :] * cw_ref[j:j + 1, :]
    xc_s[...] = acc * _sigmoid(acc)

    incl = _tril(CHUNK)
    tril = incl.astype(BF16)
    lane = _iota((1, LANES), 1)
    head0 = lane < MB_P
    sub0 = _iota((LANES, 1), 0) < MB_P
    hpg = MB_HEADS // MB_GROUPS
    gw = MB_DI // MB_GROUPS
    dt_lane0 = 2 * ML_HEADS
    npair = hpg // 2
    cs = range(nchunk)
    rows = [slice(c * CHUNK, (c + 1) * CHUNK) for c in cs]
    cg = [(c, g) for c in cs for g in range(MB_GROUPS)]
    cgj = [(c, g, j) for c, g in cg for j in range(npair)]

    dt = [_softplus(sm_ref[r, :] + bias_ref[...]) for r in rows]
    a_cum = [_cumsum_rows(tril, negA_ref[...] * x) for x in dt]
    a_t = [_transpose_rows(x) for x in a_cum]
    dt_t = [_transpose_rows(x) for x in dt]
    bm = {(c, g): xc_s[rows[c], MB_DI + g * MB_N:MB_DI + (g + 1) * MB_N] for c, g in cg}
    cm = {(c, g): xc_s[rows[c], MB_DI + (MB_GROUPS + g) * MB_N:MB_DI + (MB_GROUPS + g + 1) * MB_N] for c, g in cg}
    cb = {e: _mm(cm[e], bm[e], "bf16", "nt") for e in cg}
    xs = {(g, j): slice(g * gw + j * LANES, g * gw + (j + 1) * LANES) for g in range(MB_GROUPS) for j in range(npair)}
    x2 = {(c, g, j): xc_s[rows[c], xs[g, j]] for c, g, j in cgj}

    sc, ea, ws, el = {}, {}, {}, {}
    for c, g, j in cgj:
        for e in range(2):
            ln = dt_lane0 + g * hpg + 2 * j + e
            a_col = _lane_col(a_cum[c], ln)
            a_row = a_t[c][ln:ln + 1, :]
            a_last = a_row[:, CHUNK - 1:CHUNK]
            seg = jnp.exp(jnp.where(incl, a_col - a_row, -jnp.inf))
            sc[c, g, j, e] = seg * cb[c, g] * dt_t[c][ln:ln + 1, :]
            ea[c, g, j, e] = jnp.exp(a_col)
            ws[c, g, j, e] = jnp.exp(a_last - a_col) * _lane_col(dt[c], ln)
            el[c, g, j, e] = jnp.exp(a_last)
    y0 = {e: _mm(sc[e + (0,)], x2[e], "bf16") for e in cgj}
    y1 = {e: _mm(sc[e + (1,)], x2[e], "bf16") for e in cgj}
    kv = {(c, g, j): _mm(x2[c, g, j] * jnp.where(head0, ws[c, g, j, 0], ws[c, g, j, 1]), bm[c, g], "bf16", "tn")
          for c, g, j in cgj}

    start = {}
    for g in range(MB_GROUPS):
        for j in range(npair):
            si = g * npair + j
            st = st_ref[si]
            for c in cs:
                start[c, g, j] = st
                st = jnp.where(sub0, el[c, g, j, 0], el[c, g, j, 1]) * st + kv[c, g, j]
            st_ref[si] = st

    ycs = {(c, g, j): _mm(cm[c, g], start[c, g, j], "bf16", "nt") for c, g, j in cgj}
    y = {e: jnp.where(head0, y0[e], y1[e]) + jnp.where(head0, ea[e + (0,)], ea[e + (1,)]) * ycs[e]
         + x2[e] * dvec_ref[:, xs[e[1], e[2]]] for e in cgj}
    gs = {g: slice(g * gw, (g + 1) * gw) for g in range(MB_GROUPS)}
    zg = {(c, g): z_ref[rows[c], gs[g]] for c, g in cg}
    yg = {(c, g): jnp.concatenate([y[c, g, j] for j in range(npair)], axis=1) * (zg[c, g] * _sigmoid(zg[c, g]))
          for c, g in cg}
    ms = {e: jnp.mean(yg[e] * yg[e], axis=-1, keepdims=True) for e in cg}
    out = {(c, g): yg[c, g] * lax.rsqrt(ms[c, g] + EPS) * nw_ref[:, gs[g]] for c, g in cg}
    for c, g in cg:
        o_ref[rows[c], gs[g]] = out[c, g].astype(BF16)


def _ssd(p, conv_w, conv_b, bias_row, negA_row, d_row, norm_w, j, rb):
    s = p.shape[0]
    nx = conv_w.shape[2]
    prow = lambda c: pl.BlockSpec((None, 1, c), lambda r: (j, 0, 0))
    return pl.pallas_call(
        functools.partial(_ssd_kernel, nchunk=rb // CHUNK),
        grid=(s // rb,),
        in_specs=[pl.BlockSpec((rb, MB_DI), lambda r: (r, 3)),
                  pl.BlockSpec((rb, nx), lambda r: (r, OD_XBC // nx)),
                  pl.BlockSpec((rb, LANES), lambda r: (r, OD_SMALL // LANES)),
                  pl.BlockSpec((None, MB_CONV, nx), lambda r: (j, 0, 0)),
                  prow(nx), prow(LANES), prow(LANES), prow(MB_DI), prow(MB_DI)],
        out_specs=pl.BlockSpec((rb, MB_DI), lambda r: (r, 0)),
        out_shape=jax.ShapeDtypeStruct((s, MB_DI), BF16),
        scratch_shapes=[pltpu.VMEM((MB_HEADS // 2, 2 * MB_P, MB_N), F32),
                        pltpu.VMEM((rb + 8, nx), F32), pltpu.VMEM((rb, nx), F32)],
        compiler_params=_cparams(("arbitrary",)),
        name="ssd",
    )(p, p, p, conv_w, conv_b, bias_row, negA_row, d_row, norm_w)


def _rows3(x):
    return x.reshape(x.shape[0], 1, -1)


def _even_in_weight(w):
    z = lambda n: jnp.zeros(w.shape[:2] + (n,), w.dtype)
    return jnp.concatenate([w[..., :RW_TAIL + 128], z(128), w[..., RW_TAIL + 128:], z(96)], axis=-1).astype(BF16)


def _odd_in_weight(w):
    head_w = 2048 + 1024 + 1024 + 8 + 16
    return jnp.concatenate([w[..., 0:2048], w[..., 2056:3080], w[..., 3080:4104], w[..., 2048:2056], w[..., 5640:5656],
                            jnp.zeros(w.shape[:2] + (OD_XBC - head_w,), w.dtype), w[..., 4104:5640]],
                           axis=-1).astype(BF16)


def _small_rows(*xs):
    row = jnp.concatenate(xs, axis=1)
    return _rows3(jnp.pad(row, ((0, 0), (0, LANES - row.shape[1]))))


def kernel(x, norm_mix, norm_ffn, norm_final, w_in_even, w_out_even, hg_lb_table, hg_norm, rw_mu, rw_w0, rw_w2, rw_a0, rw_a2, rw_g2, rw_k_k, rw_k_a, rw_r_k, rw_ln_w, rw_ln_b, w_in_odd, w_out_odd, ml_i_bias, ml_f_bias, ml_norm, mb_conv_w, mb_conv_b, mb_dt_bias, mb_A_log, mb_D, mb_norm, ffn_w_up, ffn_w_down):
    b, s, d = x.shape
    assert b == 1 and d == D_MODEL and s % TM == 0
    rb_hg, rb_rw, rb_ml, rb_ssd = min(s, 256), min(s, 512), min(s, 256), min(s, 256)

    g_mix, g_ffn = _rows3(norm_mix), _rows3(norm_ffn)
    w_in_e, w_in_o = _even_in_weight(w_in_even), _odd_in_weight(w_in_odd)
    w_out_e, w_out_o = w_out_even.astype(BF16), w_out_odd.astype(BF16)
    w_up, w_down = ffn_w_up.astype(BF16), ffn_w_down.astype(BF16)
    rw = _rwkv7_params(rw_mu, rw_w0, rw_w2, rw_a0, rw_a2, rw_g2, rw_k_k, rw_k_a, rw_r_k, rw_ln_w, rw_ln_b)
    zeros8 = jnp.zeros((ml_i_bias.shape[0], 2 * ML_HEADS), F32)
    ml_bias = _small_rows(ml_i_bias, ml_f_bias)
    dt_bias = _small_rows(zeros8, mb_dt_bias)
    neg_a = _small_rows(zeros8, -jnp.exp(mb_A_log.astype(F32)))
    d_rows = _rows3(jnp.repeat(mb_D, MB_P, axis=1))

    h = x.reshape(s, d)
    for layer in range(DEPTH):
        j = layer // 2
        if layer % 2 == 0:
            p = _norm_matmul(h, g_mix, layer, w_in_e, j)
            o_a = _hgrn2(p, hg_lb_table, _rows3(hg_norm), layer, rb_hg)
            o_b = _rwkv7(p, rw, j, rb_rw)
            h = _matmul2_residual(o_a, o_b, w_out_e, j, h)
        else:
            p = _norm_matmul(h, g_mix, layer, w_in_o, j)
            o_a = _mlstm(p, ml_bias, _rows3(ml_norm), j, rb_ml)
            o_b = _ssd(p, mb_conv_w, _rows3(mb_conv_b), dt_bias, neg_a, d_rows, _rows3(mb_norm), j, rb_ssd)
            h = _matmul2_residual(o_a, o_b, w_out_o, j, h)
        act = _norm_swiglu(h, g_ffn, w_up, layer)
        h = _matmul_residual(act, w_down, layer, h)
    return _rmsnorm(h, norm_final).reshape(b, s, d)
```

```python
import functools

import jax
import jax.numpy as jnp
from jax import lax
from jax.experimental import pallas as pl
from jax.experimental.pallas import tpu as pltpu

F32 = jnp.float32
BF16 = jnp.bfloat16
HI = lax.Precision.HIGHEST

D_MODEL = 2048
DEPTH = 4
CHUNK = 64
EPS = 1e-6
LANES = 128
SUB = 16

HG_HEADS, HG_D = 8, 128
RW_HEADS, RW_N, RW_W = 16, 64, 1024
RW_GN_EPS = 64e-5
ML_HEADS, ML_DQK, ML_DV = 4, 128, 256
ML_CAP = 15.0
MB_HEADS, MB_P, MB_N, MB_GROUPS, MB_CONV = 16, 64, 128, 2, 4
MB_DI = MB_HEADS * MB_P
D_FF = 5632

NP_ODD = 6144
RW_TAIL = 7168
RW_TAIL_COLS = 288
RW_TAIL_W = 512
OD_SMALL = 4096
OD_XBC = 4608

TM = 1024
TN = 512
VMEM_LIMIT = 56 * 1024 * 1024


def _cparams(sem):
    return pltpu.CompilerParams(dimension_semantics=sem, vmem_limit_bytes=VMEM_LIMIT)


def _dot(a, b, prec=None):
    return jnp.dot(a, b, preferred_element_type=F32, precision=prec)


def _dot_nt(a, b, prec=None):
    return lax.dot_general(a, b, (((1,), (1,)), ((), ())), preferred_element_type=F32, precision=prec)


def _dot_tn(a, b, prec=None):
    return lax.dot_general(a, b, (((0,), (0,)), ((), ())), preferred_element_type=F32, precision=prec)


def _sigmoid(x):
    return 1.0 / (1.0 + jnp.exp(-x))


def _log_sigmoid(x):
    return jnp.minimum(x, 0.0) - jnp.log1p(jnp.exp(-jnp.abs(x)))


def _softplus(x):
    return jnp.maximum(x, 0.0) + jnp.log1p(jnp.exp(-jnp.abs(x)))


def _iota(shape, dim):
    return lax.broadcasted_iota(jnp.int32, shape, dim)


def _tril(n, strict=False):
    r, c = _iota((n, n), 0), _iota((n, n), 1)
    return (r > c) if strict else (r >= c)


def _rms_rows(x, g):
    return (x * lax.rsqrt(jnp.mean(x * x, axis=-1, keepdims=True) + EPS)) * g


def _norm_mm_kernel(x_ref, g_ref, w_ref, o_ref, xn_ref):
    @pl.when(pl.program_id(1) == 0)
    def _():
        xn_ref[...] = _rms_rows(x_ref[...], g_ref[...]).astype(BF16)

    o_ref[...] = _dot(xn_ref[...], w_ref[...])


def _norm_matmul(x, g, gl, w, wl):
    s, k = x.shape
    n = w.shape[2]
    return pl.pallas_call(
        _norm_mm_kernel,
        grid=(s // TM, pl.cdiv(n, TN)),
        in_specs=[pl.BlockSpec((TM, k), lambda i, j: (i, 0)),
                  pl.BlockSpec((None, 1, k), lambda i, j: (gl, 0, 0)),
                  pl.BlockSpec((None, k, TN), lambda i, j: (wl, 0, j))],
        out_specs=pl.BlockSpec((TM, TN), lambda i, j: (i, j)),
        out_shape=jax.ShapeDtypeStruct((s, n), F32),
        scratch_shapes=[pltpu.VMEM((TM, k), BF16)],
        compiler_params=_cparams(("parallel", "arbitrary")),
        name="norm_in_proj",
    )(x, g, w)


def _norm_swiglu_kernel(x_ref, g_ref, wg_ref, wu_ref, o_ref, xn_ref):
    @pl.when(pl.program_id(1) == 0)
    def _():
        xn_ref[...] = _rms_rows(x_ref[...], g_ref[...]).astype(BF16)

    xn = xn_ref[...]
    gate = _dot(xn, wg_ref[...])
    up = _dot(xn, wu_ref[...])
    o_ref[...] = (gate * _sigmoid(gate) * up).astype(BF16)


def _norm_swiglu(x, g, w_up, layer):
    s, k = x.shape
    nj = D_FF // TN
    return pl.pallas_call(
        _norm_swiglu_kernel,
        grid=(s // TM, nj),
        in_specs=[pl.BlockSpec((TM, k), lambda i, j: (i, 0)),
                  pl.BlockSpec((None, 1, k), lambda i, j: (layer, 0, 0)),
                  pl.BlockSpec((None, k, TN), lambda i, j: (layer, 0, j)),
                  pl.BlockSpec((None, k, TN), lambda i, j: (layer, 0, j + nj))],
        out_specs=pl.BlockSpec((TM, TN), lambda i, j: (i, j)),
        out_shape=jax.ShapeDtypeStruct((s, D_FF), BF16),
        scratch_shapes=[pltpu.VMEM((TM, k), BF16)],
        compiler_params=_cparams(("parallel", "arbitrary")),
        name="norm_ffn_up",
    )(x, g, w_up, w_up)


def _mm_res_kernel(x_ref, w_ref, r_ref, o_ref):
    o_ref[...] = r_ref[...] + _dot(x_ref[...], w_ref[...])


def _matmul_residual(x, w, wl, res):
    s, k = x.shape
    n = w.shape[2]
    return pl.pallas_call(
        _mm_res_kernel,
        grid=(s // TM, n // TN),
        in_specs=[pl.BlockSpec((TM, k), lambda i, j: (i, 0)),
                  pl.BlockSpec((None, k, TN), lambda i, j: (wl, 0, j)),
                  pl.BlockSpec((TM, TN), lambda i, j: (i, j))],
        out_specs=pl.BlockSpec((TM, TN), lambda i, j: (i, j)),
        out_shape=jax.ShapeDtypeStruct((s, n), F32),
        compiler_params=_cparams(("parallel", "parallel")),
        name="proj_residual",
    )(x, w, res)


def _mm2_res_kernel(x1_ref, x2_ref, w1_ref, w2_ref, r_ref, o_ref):
    o_ref[...] = r_ref[...] + (_dot(x1_ref[...], w1_ref[...]) + _dot(x2_ref[...], w2_ref[...]))


def _matmul2_residual(x1, x2, w, wl, res):
    s, k = x1.shape
    n = w.shape[2]
    assert x2.shape == (s, k) and w.shape[1] == 2 * k
    return pl.pallas_call(
        _mm2_res_kernel,
        grid=(s // TM, n // TN),
        in_specs=[pl.BlockSpec((TM, k), lambda i, j: (i, 0)),
                  pl.BlockSpec((TM, k), lambda i, j: (i, 0)),
                  pl.BlockSpec((None, k, TN), lambda i, j: (wl, 0, j)),
                  pl.BlockSpec((None, k, TN), lambda i, j: (wl, 1, j)),
                  pl.BlockSpec((TM, TN), lambda i, j: (i, j))],
        out_specs=pl.BlockSpec((TM, TN), lambda i, j: (i, j)),
        out_shape=jax.ShapeDtypeStruct((s, n), F32),
        compiler_params=_cparams(("parallel", "parallel")),
        name="out_proj_residual",
    )(x1, x2, w, w, res)


def _rmsnorm_kernel(x_ref, g_ref, o_ref):
    o_ref[...] = _rms_rows(x_ref[...], g_ref[...])


def _rmsnorm(x, g):
    s, k = x.shape
    return pl.pallas_call(
        _rmsnorm_kernel,
        grid=(s // TM,),
        in_specs=[pl.BlockSpec((TM, k), lambda i: (i, 0)), pl.BlockSpec((1, k), lambda i: (0, 0))],
        out_specs=pl.BlockSpec((TM, k), lambda i: (i, 0)),
        out_shape=jax.ShapeDtypeStruct((s, k), F32),
        compiler_params=_cparams(("parallel",)),
        name="final_norm",
    )(x, g.reshape(1, k))


def _hgrn2_kernel(q_ref, f_ref, v_ref, g_ref, lbt_ref, nw_ref, o_ref, st_ref, *, layer, nchunk):
    @pl.when(pl.program_id(1) == 0)
    def _():
        st_ref[...] = jnp.zeros_like(st_ref)

    t = lbt_ref[...]
    e = jnp.exp(t - jnp.max(t, axis=0, keepdims=True))
    sm = e / jnp.sum(e, axis=0, keepdims=True)
    lb = jnp.zeros((1, HG_D), F32)
    for i in range(1, layer + 1):
        lb = lb + sm[i:i + 1, :]
    log_lb = jnp.log(lb)
    log_1m = jnp.log1p(-lb)
    nw = nw_ref[...]

    tril = _tril(CHUNK).astype(BF16)
    rowid = _iota((SUB, HG_D), 0)
    nsub = CHUNK // SUB
    cs = range(nchunk)
    rows = [slice(c * CHUNK, (c + 1) * CHUNK) for c in cs]

    fp = [f_ref[r, :] for r in rows]
    v = [v_ref[r, :] for r in rows]
    b2 = [log_1m + _log_sigmoid(x) for x in fp]
    log_f = [jnp.maximum(log_lb, x) + jnp.log1p(jnp.exp(-jnp.abs(log_lb - x))) for x in b2]
    k = [(1.0 - lb) * _sigmoid(-x) for x in fp]
    q = [x * _sigmoid(x) for x in (q_ref[r, :] for r in rows)]
    bc = [_cumsum_rows(tril, x) for x in log_f]
    b_last = [x[CHUNK - 1:CHUNK, :] for x in bc]
    kv = [_mm(v[c], k[c] * jnp.exp(b_last[c] - bc[c]), "bf16", "tn") for c in cs]
    st = st_ref[...]
    starts = []
    for c in cs:
        starts.append(st)
        st = jnp.exp(b_last[c]) * st + kv[c]
    st_ref[...] = st
    acc = [_mm(q[c] * jnp.exp(bc[c]), starts[c], "bf16", "nt") for c in cs]
    acc = [[a[blk * SUB:(blk + 1) * SUB] for blk in range(nsub)] for a in acc]
    cb = [(c, blk) for blk in range(1, nsub) for c in cs]
    ref_b = {(c, blk): bc[c][blk * SUB - 1:blk * SUB, :] for c, blk in cb}
    qx = {(c, blk): q[c][blk * SUB:(blk + 1) * SUB] * jnp.exp(bc[c][blk * SUB:(blk + 1) * SUB] - ref_b[c, blk])
          for c, blk in cb}
    kx = {(c, blk): k[c][0:blk * SUB] * jnp.exp(ref_b[c, blk] - bc[c][0:blk * SUB]) for c, blk in cb}
    sc = {e: _mm(qx[e], kx[e], "bf16", "nt") for e in cb}
    od = {(c, blk): _mm(sc[c, blk], v[c][0:blk * SUB], "bf16") for c, blk in cb}
    for c, blk in cb:
        acc[c][blk] = acc[c][blk] + od[c, blk]
    for s in range(SUB):
        for c in cs:
            for blk in range(nsub):
                lo = blk * SUB
                b_i = bc[c][lo:lo + SUB]
                d = b_i - bc[c][lo + s:lo + s + 1, :]
                if s > 0:
                    d = jnp.where(rowid >= s, d, -jnp.inf)
                col = jnp.sum(q[c][lo:lo + SUB] * k[c][lo + s:lo + s + 1, :] * jnp.exp(d), axis=1, keepdims=True)
                acc[c][blk] = acc[c][blk] + col * v[c][lo + s:lo + s + 1, :]
    for c in cs:
        o = jnp.concatenate(acc[c], axis=0)
        y = o * lax.rsqrt(jnp.mean(o * o, axis=-1, keepdims=True) + EPS) * nw
        o_ref[rows[c], :] = (y * _sigmoid(g_ref[rows[c], :])).astype(BF16)


def _hgrn2(p, lb_table, norm_w, layer, rb):
    s = p.shape[0]
    nb = HG_HEADS
    col = lambda off: pl.BlockSpec((rb, HG_D), lambda h, r, off=off: (r, off + h))
    return pl.pallas_call(
        functools.partial(_hgrn2_kernel, layer=layer, nchunk=rb // CHUNK),
        grid=(HG_HEADS, s // rb),
        in_specs=[col(0), col(nb), col(2 * nb), col(3 * nb),
                  pl.BlockSpec((DEPTH, HG_D), lambda h, r: (0, h)),
                  pl.BlockSpec((None, 1, HG_D), lambda h, r: (layer // 2, 0, h))],
        out_specs=pl.BlockSpec((rb, HG_D), lambda h, r: (r, h)),
        out_shape=jax.ShapeDtypeStruct((s, HG_HEADS * HG_D), BF16),
        scratch_shapes=[pltpu.VMEM((HG_D, HG_D), F32)],
        compiler_params=_cparams(("parallel", "arbitrary")),
        name="hgrn2",
    )(p, p, p, p, lb_table, norm_w)


def _shift_lerp(x, prev_row, mu):
    rolled = pltpu.roll(x, 1, 0)
    shifted = jnp.where(_iota(x.shape, 0) == 0, prev_row, rolled)
    return x + (shifted - x) * mu


def _split_bf16(x):
    hi = x.astype(BF16)
    lo = (x - hi.astype(F32)).astype(BF16)
    return hi, lo


def _mm(a, b, mode, kind="nn"):
    dot = {"nn": _dot, "nt": _dot_nt, "tn": _dot_tn}[kind]
    if mode == "hi":
        return dot(a, b, HI)
    if mode == "bf16":
        return dot(a.astype(BF16), b.astype(BF16))
    ah, al = _split_bf16(a)
    bh, bl = _split_bf16(b)
    return dot(ah, bh) + (dot(ah, bl) + dot(al, bh))


def _cumsum_rows(tril_bf16, x):
    hi, lo = _split_bf16(x)
    lo2 = (x - hi.astype(F32) - lo.astype(F32)).astype(BF16)
    return _dot(tril_bf16, hi) + (_dot(tril_bf16, lo) + _dot(tril_bf16, lo2))


def _neumann_inverse(mats, mode):
    n = mats[0].shape[0]
    eye = (_iota((n, n), 0) == _iota((n, n), 1)).astype(F32)
    sq = lambda xs: [_mm(x, x, mode) for x in xs]
    mul = lambda xs, ys: [_mm(x, y, mode) for x, y in zip(xs, ys)]
    one_plus = lambda xs: [eye + x for x in xs]
    p1 = sq(mats)
    m01 = mul(one_plus(mats), one_plus(p1))
    p2 = sq(p1)
    p3 = sq(p2)
    m23 = mul(one_plus(p2), one_plus(p3))
    p4 = sq(p3)
    m03 = mul(m01, m23)
    p5 = sq(p4)
    m45 = mul(one_plus(p4), one_plus(p5))
    return mul(m03, m45)


RW_PREC = dict(score="bf16", neumann="bf16", apply="bf16", state="bf16")


def _rwkv7_kernel(r_ref, k_ref, v_ref, tail_ref,
                  mu_r_ref, mu_k_ref, mu_v_ref, mu_tail_ref,
                  w0_ref, a0_ref, w2_ref, a2_ref, g2_ref, kk_ref, ka_ref, rk_ref, lnw_ref, lnb_ref,
                  o_ref, st_ref, prev_ref, *, nchunk, npair):
    rb = r_ref.shape[0]
    pc = RW_PREC
    wide = npair * LANES

    @pl.when(pl.program_id(1) == 0)
    def _():
        st_ref[...] = jnp.zeros_like(st_ref)
        prev_ref[...] = jnp.zeros_like(prev_ref)

    lane = _iota((1, LANES), 1)
    head0 = lane < RW_N

    def head_sum(x):
        s0 = jnp.sum(jnp.where(head0, x, 0.0), axis=1, keepdims=True)
        s1 = jnp.sum(jnp.where(head0, 0.0, x), axis=1, keepdims=True)
        return jnp.where(head0, s0, s1)

    tw = tail_ref.shape[1]
    tail_raw = jnp.where(_iota((1, tw), 1) < RW_TAIL_COLS, tail_ref[...], 0.0)
    r_raw, k_raw, v_raw = r_ref[...], k_ref[...], v_ref[...]
    r_all = _shift_lerp(r_raw, prev_ref[0:1, 0:wide], mu_r_ref[...])
    k_all = _shift_lerp(k_raw, prev_ref[1:2, 0:wide], mu_k_ref[...])
    v_all = _shift_lerp(v_raw, prev_ref[2:3, 0:wide], mu_v_ref[...])
    tail = _shift_lerp(tail_raw, prev_ref[3:4, 0:tw], mu_tail_ref[...])
    wa, gl = tail[:, 0:LANES], tail[:, LANES:3 * LANES]
    prev_ref[0:1, 0:wide] = r_raw[rb - 1:rb]
    prev_ref[1:2, 0:wide] = k_raw[rb - 1:rb]
    prev_ref[2:3, 0:wide] = v_raw[rb - 1:rb]
    prev_ref[3:4, 0:tw] = tail_raw[rb - 1:rb]

    w_log = -_softplus(-(w0_ref[...] + _dot(jnp.tanh(wa), w2_ref[...]))) - 0.5
    lw_all = -jnp.exp(w_log)
    a_all = _sigmoid(a0_ref[...] + _dot(wa, a2_ref[...]))
    gate_all = _dot(_sigmoid(gl), g2_ref[...])
    kk_all = k_all * kk_ref[...]
    k_all = k_all * (1.0 + (a_all - 1.0) * ka_ref[...])
    rk_all = r_all * k_all * rk_ref[...]
    ps = range(npair)
    tile = lambda x, p: x[:, p * LANES:(p + 1) * LANES]
    r, k, v, a, lw = ([tile(x, p) for p in ps] for x in (r_all, k_all, v_all, a_all, lw_all))
    kk = [tile(kk_all, p) for p in ps]
    kk = [x / jnp.maximum(jnp.sqrt(head_sum(x * x)), 1e-12) for x in kk]
    bonus = [head_sum(tile(rk_all, p)) * v[p] for p in ps]

    head1 = jnp.logical_not(head0)
    bd = (_iota((LANES, LANES), 0) < RW_N) == (_iota((LANES, LANES), 1) < RW_N)
    gs = 2 * CHUNK
    ri, ci = _iota((gs, gs), 0), _iota((gs, gs), 1)
    same = (ri // CHUNK) == (ci // CHUNK)
    strict = jnp.logical_and(same, ri > ci)
    incl = jnp.logical_and(same, ri >= ci)
    tril = _tril(CHUNK).astype(BF16)
    head00 = jnp.concatenate([head0, head0], axis=1)

    def stack_heads(x, masked):
        return jnp.concatenate([jnp.where(head0, x, 0.0), jnp.where(head1, x, 0.0)] if masked else [x, x], axis=0)

    def unstack_heads(x):
        return jnp.where(head0 if x.shape[1] == LANES else head00, x[:CHUNK], x[CHUNK:])

    cs = range(nchunk)
    grp = [(p, c) for c in cs for p in ps]
    rows = [slice(c * CHUNK, (c + 1) * CHUNK) for c in cs]
    cut = lambda x: {(p, c): x[p][rows[c]] for p, c in grp}
    r, k, v, a, lw, kk = cut(r), cut(k), cut(v), cut(a), cut(lw), cut(kk)
    g = {e: _cumsum_rows(tril, lw[e]) for e in grp}
    ieg = {e: jnp.exp(-g[e]) for e in grp}
    rt = {e: r[e] * jnp.exp(g[e]) for e in grp}
    at = {e: -kk[e] * jnp.exp(g[e] - lw[e]) for e in grp}
    bt = {e: kk[e] * a[e] * ieg[e] for e in grp}
    kt = {e: k[e] * ieg[e] for e in grp}
    l_a = {e: stack_heads(at[e], True) for e in grp}
    l_r = {e: stack_heads(rt[e], True) for e in grp}
    r_b = {e: stack_heads(bt[e], True) for e in grp}
    r_k = {e: stack_heads(kt[e], True) for e in grp}
    a_ab = {e: jnp.where(strict, _mm(l_a[e], r_b[e], pc["score"], "nt"), 0.0) for e in grp}
    a_ak = {e: jnp.where(strict, _mm(l_a[e], r_k[e], pc["score"], "nt"), 0.0) for e in grp}
    rab = {e: jnp.where(incl, _mm(l_r[e], r_b[e], pc["score"], "nt"), 0.0) for e in grp}
    rak = {e: jnp.where(incl, _mm(l_r[e], r_k[e], pc["score"], "nt"), 0.0) for e in grp}
    v_st = {e: stack_heads(v[e], False) for e in grp}
    aakv = {e: _mm(a_ak[e], v_st[e], pc["apply"]) for e in grp}
    rakv = {e: unstack_heads(_mm(rak[e], v_st[e], pc["apply"])) for e in grp}
    t_inv = dict(zip(grp, _neumann_inverse([a_ab[e] for e in grp], pc["neumann"])))
    xs = {e: unstack_heads(_mm(t_inv[e], jnp.concatenate([stack_heads(at[e], False), aakv[e]], axis=1), pc["apply"]))
          for e in grp}
    ta = {e: xs[e][:, :LANES] for e in grp}
    tav = {e: xs[e][:, LANES:] for e in grp}
    egl = {e: jnp.exp(g[e][CHUNK - 1:CHUNK, :]) for e in grp}
    bte = {e: bt[e] * egl[e] for e in grp}
    m_mat = {e: jnp.where(bd, _mm(ta[e], bte[e], pc["state"], "tn"), 0.0) for e in grp}
    c_mat = {e: jnp.where(bd, _mm(jnp.concatenate([tav[e], v[e]], axis=0),
                                  jnp.concatenate([bte[e], kt[e] * egl[e]], axis=0), pc["state"], "tn"), 0.0)
             for e in grp}

    st = [st_ref[p] for p in ps]
    start = {}
    for c in cs:
        for p in ps:
            start[p, c] = st[p]
            st[p] = st[p] * egl[p, c] + _mm(st[p], m_mat[p, c], pc["state"]) + c_mat[p, c]
    for p in ps:
        st_ref[p] = st[p]
    u = {e: _mm(ta[e], start[e], pc["state"], "nt") + tav[e] for e in grp}
    o_in = {e: _mm(rt[e], start[e], pc["state"], "nt") for e in grp}
    y = {e: _mm(rab[e], stack_heads(u[e], True), pc["apply"]) for e in grp}
    o = {e: o_in[e] + (y[e][:CHUNK] + y[e][CHUNK:]) + rakv[e] for e in grp}

    for p in ps:
        ts = slice(p * LANES, (p + 1) * LANES)
        o_p = jnp.concatenate([o[p, c] for c in cs], axis=0)
        mu = head_sum(o_p) * (1.0 / RW_N)
        d = o_p - mu
        var = head_sum(d * d) * (1.0 / RW_N)
        y_p = d * lax.rsqrt(var + RW_GN_EPS) * lnw_ref[:, ts] + lnb_ref[:, ts]
        o_ref[:, ts] = ((y_p + bonus[p]) * gate_all[:, ts]).astype(BF16)


def _rwkv7_params(mu, w0, w2, a0, a2, g2, k_k, k_a, r_k, ln_w, ln_b):
    n = mu.shape[0]
    row = lambda x: x.reshape(n, 1, -1)
    assert 2 * w2.shape[1] == LANES and 2 * a2.shape[1] == LANES
    mu_p = jnp.pad(mu, ((0, 0), (0, 3 * RW_W + RW_TAIL_W - mu.shape[1])))
    return dict(mu=row(mu_p), w0=row(w0), a0=row(a0),
                w2=jnp.concatenate([w2, jnp.zeros_like(w2)], axis=1),
                a2=jnp.concatenate([jnp.zeros_like(a2), a2], axis=1),
                g2=jnp.pad(g2, ((0, 0), (0, 2 * LANES - g2.shape[1]), (0, 0))),
                k_k=row(k_k), k_a=row(k_a), r_k=row(r_k), ln_w=row(ln_w), ln_b=row(ln_b))


RW_PAIRS = 4


def _rwkv7(p, prm, j, rb):
    s = p.shape[0]
    wide = RW_PAIRS * LANES
    base = 4 * HG_HEADS * LANES // wide
    nb = RW_W // wide
    col = lambda off: pl.BlockSpec((rb, wide), lambda h, r, off=off: (r, off + h))
    par = pl.BlockSpec((None, 1, wide), lambda h, r: (j, 0, h))
    par_off = lambda off: pl.BlockSpec((None, 1, wide), lambda h, r, off=off: (j, 0, off + h))
    lora = lambda rows: pl.BlockSpec((None, rows, wide), lambda h, r: (j, 0, h))
    assert RW_TAIL % RW_TAIL_W == 0 and (3 * RW_W) % RW_TAIL_W == 0
    return pl.pallas_call(
        functools.partial(_rwkv7_kernel, nchunk=rb // CHUNK, npair=RW_PAIRS),
        grid=(nb, s // rb),
        in_specs=[col(base), col(base + nb), col(base + 2 * nb),
                  pl.BlockSpec((rb, RW_TAIL_W), lambda h, r: (r, RW_TAIL // RW_TAIL_W)),
                  par, par_off(nb), par_off(2 * nb),
                  pl.BlockSpec((None, 1, RW_TAIL_W), lambda h, r: (j, 0, 3 * RW_W // RW_TAIL_W)),
                  par, par, lora(LANES), lora(LANES), lora(2 * LANES),
                  par, par, par, par, par],
        out_specs=pl.BlockSpec((rb, wide), lambda h, r: (r, h)),
        out_shape=jax.ShapeDtypeStruct((s, RW_W), BF16),
        scratch_shapes=[pltpu.VMEM((RW_PAIRS, LANES, LANES), F32), pltpu.VMEM((8, max(wide, RW_TAIL_W)), F32)],
        compiler_params=_cparams(("parallel", "arbitrary")),
        name="rwkv7",
    )(p, p, p, p, prm["mu"], prm["mu"], prm["mu"], prm["mu"],
      prm["w0"], prm["a0"], prm["w2"], prm["a2"], prm["g2"],
      prm["k_k"], prm["k_a"], prm["r_k"], prm["ln_w"], prm["ln_b"])


def _lane_col(x, idx):
    return jnp.sum(jnp.where(_iota(x.shape, 1) == idx, x, 0.0), axis=1, keepdims=True)


def _transpose_rows(x):
    eye = (_iota((LANES, LANES), 0) == _iota((LANES, LANES), 1)).astype(BF16)
    hi, lo = _split_bf16(x)
    lo2 = (x - hi.astype(F32) - lo.astype(F32)).astype(BF16)
    return _dot_nt(eye, hi) + (_dot_nt(eye, lo) + _dot_nt(eye, lo2))


def _mlstm_kernel(q_ref, k_ref, v_ref, og_ref, sm_ref, bias_ref, nw_ref, o_ref, c_ref, n_ref, m_ref, *, nchunk):
    @pl.when(pl.program_id(0) == 0)
    def _():
        c_ref[...] = jnp.zeros_like(c_ref)
        n_ref[...] = jnp.zeros_like(n_ref)
        m_ref[...] = jnp.zeros_like(m_ref)

    incl = _tril(CHUNK)
    tril = incl.astype(BF16)
    lane = _iota((CHUNK, LANES), 1)
    is_f = jnp.logical_and(lane >= ML_HEADS, lane < 2 * ML_HEADS)
    scale = ML_DQK ** -0.5
    cs = range(nchunk)
    hs = range(ML_HEADS)
    rows = [slice(c * CHUNK, (c + 1) * CHUNK) for c in cs]
    ch = [(c, h) for c in cs for h in hs]

    pre = [sm_ref[r, :] + bias_ref[...] for r in rows]
    cap = [ML_CAP * jnp.tanh(x / ML_CAP) for x in pre]
    x = [jnp.where(is_f, _log_sigmoid(y), y) for y in cap]
    cum = [_cumsum_rows(tril, y) for y in x]
    x_t = [_transpose_rows(y) for y in x]
    cum_t = [_transpose_rows(y) for y in cum]

    q = {(c, h): q_ref[rows[c], h * ML_DQK:(h + 1) * ML_DQK] * scale for c, h in ch}
    k = {(c, h): k_ref[rows[c], h * ML_DQK:(h + 1) * ML_DQK] for c, h in ch}
    v = {(c, h): v_ref[rows[c], h * ML_DV:(h + 1) * ML_DV] for c, h in ch}
    b_col = {(c, h): _lane_col(cum[c], ML_HEADS + h) for c, h in ch}
    i_col = {(c, h): _lane_col(x[c], h) for c, h in ch}
    b_row = {(c, h): cum_t[c][ML_HEADS + h:ML_HEADS + h + 1, :] for c, h in ch}
    i_row = {(c, h): x_t[c][h:h + 1, :] for c, h in ch}
    qk = {e: _mm(q[e], k[e], "bf16", "nt") for e in ch}
    dmat = {e: jnp.where(incl, b_col[e] - b_row[e] + i_row[e], -jnp.inf) for e in ch}
    dmax = {e: jnp.max(dmat[e], axis=1, keepdims=True) for e in ch}
    b_last = {e: b_row[e][:, CHUNK - 1:CHUNK] for e in ch}
    src_row = {e: b_last[e] - b_row[e] + i_row[e] for e in ch}
    src_col = {e: b_last[e] - b_col[e] + i_col[e] for e in ch}
    src_max = {e: jnp.max(src_row[e], axis=1, keepdims=True) for e in ch}

    m_prev, m_new = {}, {}
    for h in hs:
        m = m_ref[h:h + 1, 0:1]
        for c in cs:
            m_prev[c, h] = m
            m = jnp.maximum(b_last[c, h] + m, src_max[c, h])
            m_new[c, h] = m
        m_ref[h:h + 1, :] = jnp.broadcast_to(m, (1, LANES))
    inter = {e: b_col[e] + m_prev[e] for e in ch}
    m_t = {e: jnp.maximum(inter[e], dmax[e]) for e in ch}
    w_inter = {e: jnp.exp(inter[e] - m_t[e]) for e in ch}
    pmat = {e: jnp.exp(dmat[e] - m_t[e]) * qk[e] for e in ch}
    num = {e: _mm(pmat[e], v[e], "bf16") for e in ch}
    den = {e: jnp.sum(pmat[e], axis=1, keepdims=True) for e in ch}
    decay = {e: jnp.exp(b_last[e] + m_prev[e] - m_new[e]) for e in ch}
    wk = {e: jnp.exp(src_col[e] - m_new[e]) * k[e] for e in ch}
    kv = {e: _mm(wk[e], v[e], "bf16", "tn") for e in ch}
    ksum = {e: jnp.sum(wk[e], axis=0, keepdims=True) for e in ch}

    c_start, n_start = {}, {}
    for h in hs:
        c_mat, n_row = c_ref[h], n_ref[h:h + 1, :]
        for c in cs:
            c_start[c, h], n_start[c, h] = c_mat, n_row
            c_mat = decay[c, h] * c_mat + kv[c, h]
            n_row = decay[c, h] * n_row + ksum[c, h]
        c_ref[h] = c_mat
        n_ref[h:h + 1, :] = n_row

    qc = {e: _mm(q[e], c_start[e], "bf16") for e in ch}
    vs = {h: slice(h * ML_DV, (h + 1) * ML_DV) for h in hs}
    qn = {e: jnp.sum(q[e] * n_start[e], axis=1, keepdims=True) for e in ch}
    num = {e: num[e] + w_inter[e] * qc[e] for e in ch}
    den = {e: den[e] + w_inter[e] * qn[e] for e in ch}
    h_out = {e: num[e] / jnp.maximum(jnp.abs(den[e]), jnp.exp(-m_t[e])) for e in ch}
    ms = {e: jnp.mean(h_out[e] * h_out[e], axis=-1, keepdims=True) for e in ch}
    gate = {(c, h): _sigmoid(og_ref[rows[c], vs[h]]) for c, h in ch}
    y = {(c, h): h_out[c, h] * lax.rsqrt(ms[c, h] + EPS) * nw_ref[:, vs[h]] * gate[c, h] for c, h in ch}
    for c, h in ch:
        o_ref[rows[c], vs[h]] = y[c, h].astype(BF16)


def _mlstm(p, bias_row, norm_w, j, rb):
    s = p.shape[0]
    nq = ML_HEADS * ML_DQK
    nv = ML_HEADS * ML_DV
    return pl.pallas_call(
        functools.partial(_mlstm_kernel, nchunk=rb // CHUNK),
        grid=(s // rb,),
        in_specs=[pl.BlockSpec((rb, nq), lambda r: (r, 0)),
                  pl.BlockSpec((rb, nq), lambda r: (r, 1)),
                  pl.BlockSpec((rb, nv), lambda r: (r, 1)),
                  pl.BlockSpec((rb, nv), lambda r: (r, 2)),
                  pl.BlockSpec((rb, LANES), lambda r: (r, OD_SMALL // LANES)),
                  pl.BlockSpec((None, 1, LANES), lambda r: (j, 0, 0)),
                  pl.BlockSpec((None, 1, nv), lambda r: (j, 0, 0))],
        out_specs=pl.BlockSpec((rb, nv), lambda r: (r, 0)),
        out_shape=jax.ShapeDtypeStruct((s, nv), BF16),
        scratch_shapes=[pltpu.VMEM((ML_HEADS, ML_DQK, ML_DV), F32),
                        pltpu.VMEM((8, ML_DQK), F32), pltpu.VMEM((8, LANES), F32)],
        compiler_params=_cparams(("arbitrary",)),
        name="mlstm",
    )(p, p, p, p, p, bias_row, norm_w)


def _ssd_kernel(z_ref, xbc_ref, sm_ref, cw_ref, cb_ref, bias_ref, negA_ref, dvec_ref, nw_ref, o_ref,
                st_ref, xin_s, xc_s, *, nchunk):
    rb = z_ref.shape[0]
    pad = 8

    @pl.when(pl.program_id(0) == 0)
    def _():
        st_ref[...] = jnp.zeros_like(st_ref)
        xin_s[rb:rb + pad, :] = jnp.zeros((pad, xin_s.shape[1]), F32)

    xin_s[0:pad, :] = xin_s[rb:rb + pad, :]
    xin_s[pad:rb + pad, :] = xbc_ref[...]
    xin = xin_s[...]
    acc = cb_ref[...] + xin[pad:, :] * cw_ref[MB_CONV - 1:MB_CONV, :]
    for j in range(MB_CONV - 1):
        acc = acc + pltpu.roll(xin, MB_CONV - 1 - j, 0)[pad:, :] * cw_ref[j:j + 1, :]
    xc_s[...] = acc * _sigmoid(acc)

    incl = _tril(CHUNK)
    tril = incl.astype(BF16)
    lane = _iota((1, LANES), 1)
    head0 = lane < MB_P
    sub0 = _iota((LANES, 1), 0) < MB_P
    hpg = MB_HEADS // MB_GROUPS
    gw = MB_DI // MB_GROUPS
    dt_lane0 = 2 * ML_HEADS
    npair = hpg // 2
    cs = range(nchunk)
    rows = [slice(c * CHUNK, (c + 1) * CHUNK) for c in cs]
    cg = [(c, g) for c in cs for g in range(MB_GROUPS)]
    cgj = [(c, g, j) for c, g in cg for j in range(npair)]

    dt = [_softplus(sm_ref[r, :] + bias_ref[...]) for r in rows]
    a_cum = [_cumsum_rows(tril, negA_ref[...] * x) for x in dt]
    a_t = [_transpose_rows(x) for x in a_cum]
    dt_t = [_transpose_rows(x) for x in dt]
    bm = {(c, g): xc_s[rows[c], MB_DI + g * MB_N:MB_DI + (g + 1) * MB_N] for c, g in cg}
    cm = {(c, g): xc_s[rows[c], MB_DI + (MB_GROUPS + g) * MB_N:MB_DI + (MB_GROUPS + g + 1) * MB_N] for c, g in cg}
    cb = {e: _mm(cm[e], bm[e], "bf16", "nt") for e in cg}
    xs = {(g, j): slice(g * gw + j * LANES, g * gw + (j + 1) * LANES) for g in range(MB_GROUPS) for j in range(npair)}
    x2 = {(c, g, j): xc_s[rows[c], xs[g, j]] for c, g, j in cgj}

    sc, ea, ws, el = {}, {}, {}, {}
    for c, g, j in cgj:
        for e in range(2):
            ln = dt_lane0 + g * hpg + 2 * j + e
            a_col = _lane_col(a_cum[c], ln)
            a_row = a_t[c][ln:ln + 1, :]
            a_last = a_row[:, CHUNK - 1:CHUNK]
            seg = jnp.exp(jnp.where(incl, a_col - a_row, -jnp.inf))
            sc[c, g, j, e] = seg * cb[c, g] * dt_t[c][ln:ln + 1, :]
            ea[c, g, j, e] = jnp.exp(a_col)
            ws[c, g, j, e] = jnp.exp(a_last - a_col) * _lane_col(dt[c], ln)
            el[c, g, j, e] = jnp.exp(a_last)
    y0 = {e: _mm(sc[e + (0,)], x2[e], "bf16") for e in cgj}
    y1 = {e: _mm(sc[e + (1,)], x2[e], "bf16") for e in cgj}
    kv = {(c, g, j): _mm(x2[c, g, j] * jnp.where(head0, ws[c, g, j, 0], ws[c, g, j, 1]), bm[c, g], "bf16", "tn")
          for c, g, j in cgj}

    start = {}
    for g in range(MB_GROUPS):
        for j in range(npair):
            si = g * npair + j
            st = st_ref[si]
            for c in cs:
                start[c, g, j] = st
                st = jnp.where(sub0, el[c, g, j, 0], el[c, g, j, 1]) * st + kv[c, g, j]
            st_ref[si] = st

    ycs = {(c, g, j): _mm(cm[c, g], start[c, g, j], "bf16", "nt") for c, g, j in cgj}
    y = {e: jnp.where(head0, y0[e], y1[e]) + jnp.where(head0, ea[e + (0,)], ea[e + (1,)]) * ycs[e]
         + x2[e] * dvec_ref[:, xs[e[1], e[2]]] for e in cgj}
    gs = {g: slice(g * gw, (g + 1) * gw) for g in range(MB_GROUPS)}
    zg = {(c, g): z_ref[rows[c], gs[g]] for c, g in cg}
    yg = {(c, g): jnp.concatenate([y[c, g, j] for j in range(npair)], axis=1) * (zg[c, g] * _sigmoid(zg[c, g]))
          for c, g in cg}
    ms = {e: jnp.mean(yg[e] * yg[e], axis=-1, keepdims=True) for e in cg}
    out = {(c, g): yg[c, g] * lax.rsqrt(ms[c, g] + EPS) * nw_ref[:, gs[g]] for c, g in cg}
    for c, g in cg:
        o_ref[rows[c], gs[g]] = out[c, g].astype(BF16)


def _ssd(p, conv_w, conv_b, bias_row, negA_row, d_row, norm_w, j, rb):
    s = p.shape[0]
    nx = conv_w.shape[2]
    prow = lambda c: pl.BlockSpec((None, 1, c), lambda r: (j, 0, 0))
    return pl.pallas_call(
        functools.partial(_ssd_kernel, nchunk=rb // CHUNK),
        grid=(s // rb,),
        in_specs=[pl.BlockSpec((rb, MB_DI), lambda r: (r, 3)),
                  pl.BlockSpec((rb, nx), lambda r: (r, OD_XBC // nx)),
                  pl.BlockSpec((rb, LANES), lambda r: (r, OD_SMALL // LANES)),
                  pl.BlockSpec((None, MB_CONV, nx), lambda r: (j, 0, 0)),
                  prow(nx), prow(LANES), prow(LANES), prow(MB_DI), prow(MB_DI)],
        out_specs=pl.BlockSpec((rb, MB_DI), lambda r: (r, 0)),
        out_shape=jax.ShapeDtypeStruct((s, MB_DI), BF16),
        scratch_shapes=[pltpu.VMEM((MB_HEADS // 2, 2 * MB_P, MB_N), F32),
                        pltpu.VMEM((rb + 8, nx), F32), pltpu.VMEM((rb, nx), F32)],
        compiler_params=_cparams(("arbitrary",)),
        name="ssd",
    )(p, p, p, conv_w, conv_b, bias_row, negA_row, d_row, norm_w)


def _rows3(x):
    return x.reshape(x.shape[0], 1, -1)


def _odd_in_weight(w):
    head_w = 2048 + 1024 + 1024 + 8 + 16
    return jnp.concatenate([w[..., 0:2048], w[..., 2056:3080], w[..., 3080:4104], w[..., 2048:2056], w[..., 5640:5656],
                            jnp.zeros(w.shape[:2] + (OD_XBC - head_w,), w.dtype), w[..., 4104:5640]],
                           axis=-1).astype(BF16)


def _small_rows(*xs):
    row = jnp.concatenate(xs, axis=1)
    return _rows3(jnp.pad(row, ((0, 0), (0, LANES - row.shape[1]))))


def kernel(x, norm_mix, norm_ffn, norm_final, w_in_even, w_out_even, hg_lb_table, hg_norm, rw_mu, rw_w0, rw_w2, rw_a0, rw_a2, rw_g2, rw_k_k, rw_k_a, rw_r_k, rw_ln_w, rw_ln_b, w_in_odd, w_out_odd, ml_i_bias, ml_f_bias, ml_norm, mb_conv_w, mb_conv_b, mb_dt_bias, mb_A_log, mb_D, mb_norm, ffn_w_up, ffn_w_down):
    b, s, d = x.shape
    assert b == 1 and d == D_MODEL and s % TM == 0
    rb_hg, rb_rw, rb_ml, rb_ssd = min(s, 256), min(s, 512), min(s, 256), min(s, 256)

    g_mix, g_ffn = _rows3(norm_mix), _rows3(norm_ffn)
    w_in_e, w_in_o = w_in_even.astype(BF16), _odd_in_weight(w_in_odd)
    w_out_e, w_out_o = w_out_even.astype(BF16), w_out_odd.astype(BF16)
    w_up, w_down = ffn_w_up.astype(BF16), ffn_w_down.astype(BF16)
    rw = _rwkv7_params(rw_mu, rw_w0, rw_w2, rw_a0, rw_a2, rw_g2, rw_k_k, rw_k_a, rw_r_k, rw_ln_w, rw_ln_b)
    zeros8 = jnp.zeros((ml_i_bias.shape[0], 2 * ML_HEADS), F32)
    ml_bias = _small_rows(ml_i_bias, ml_f_bias)
    dt_bias = _small_rows(zeros8, mb_dt_bias)
    neg_a = _small_rows(zeros8, -jnp.exp(mb_A_log.astype(F32)))
    d_rows = _rows3(jnp.repeat(mb_D, MB_P, axis=1))

    h = x.reshape(s, d)
    for layer in range(DEPTH):
        j = layer // 2
        if layer % 2 == 0:
            p = _norm_matmul(h, g_mix, layer, w_in_e, j)
            o_a = _hgrn2(p, hg_lb_table, _rows3(hg_norm), layer, rb_hg)
            o_b = _rwkv7(p, rw, j, rb_rw)
            h = _matmul2_residual(o_a, o_b, w_out_e, j, h)
        else:
            p = _norm_matmul(h, g_mix, layer, w_in_o, j)
            o_a = _mlstm(p, ml_bias, _rows3(ml_norm), j, rb_ml)
            o_b = _ssd(p, mb_conv_w, _rows3(mb_conv_b), dt_bias, neg_a, d_rows, _rows3(mb_norm), j, rb_ssd)
            h = _matmul2_residual(o_a, o_b, w_out_o, j, h)
        act = _norm_swiglu(h, g_ffn, w_up, layer)
        h = _matmul_residual(act, w_down, layer, h)
    return _rmsnorm(h, norm_final).reshape(b, s, d)
```

```python
import functools

import jax
import jax.numpy as jnp
from jax import lax
from jax.experimental import pallas as pl
from jax.experimental.pallas import tpu as pltpu

F32 = jnp.float32
BF16 = jnp.bfloat16
HI = lax.Precision.HIGHEST

D_MODEL = 2048
DEPTH = 4
CHUNK = 64
EPS = 1e-6
LANES = 128
SUB = 16

HG_HEADS, HG_D = 8, 128
RW_HEADS, RW_N, RW_W = 16, 64, 1024
RW_GN_EPS = 64e-5
ML_HEADS, ML_DQK, ML_DV = 4, 128, 256
ML_CAP = 15.0
MB_HEADS, MB_P, MB_N, MB_GROUPS, MB_CONV = 16, 64, 128, 2, 4
MB_DI = MB_HEADS * MB_P
D_FF = 5632

NP_ODD = 6144
RW_TAIL = 7168
RW_TAIL_COLS = 288
RW_TAIL_W = 512
OD_SMALL = 4096
OD_XBC = 4608

TM = 1024
TN = 512
VMEM_LIMIT = 56 * 1024 * 1024


def _cparams(sem):
    return pltpu.CompilerParams(dimension_semantics=sem, vmem_limit_bytes=VMEM_LIMIT)


def _dot(a, b, prec=None):
    return jnp.dot(a, b, preferred_element_type=F32, precision=prec)


def _dot_nt(a, b, prec=None):
    return lax.dot_general(a, b, (((1,), (1,)), ((), ())), preferred_element_type=F32, precision=prec)


def _dot_tn(a, b, prec=None):
    return lax.dot_general(a, b, (((0,), (0,)), ((), ())), preferred_element_type=F32, precision=prec)


def _sigmoid(x):
    return 1.0 / (1.0 + jnp.exp(-x))


def _log_sigmoid(x):
    return jnp.minimum(x, 0.0) - jnp.log1p(jnp.exp(-jnp.abs(x)))


def _softplus(x):
    return jnp.maximum(x, 0.0) + jnp.log1p(jnp.exp(-jnp.abs(x)))


def _iota(shape, dim):
    return lax.broadcasted_iota(jnp.int32, shape, dim)


def _tril(n, strict=False):
    r, c = _iota((n, n), 0), _iota((n, n), 1)
    return (r > c) if strict else (r >= c)


def _rms_rows(x, g):
    return (x * lax.rsqrt(jnp.mean(x * x, axis=-1, keepdims=True) + EPS)) * g


def _norm_mm_kernel(x_ref, g_ref, w_ref, o_ref, xn_ref):
    @pl.when(pl.program_id(1) == 0)
    def _():
        xn_ref[...] = _rms_rows(x_ref[...], g_ref[...]).astype(BF16)

    o_ref[...] = _dot(xn_ref[...], w_ref[...])


def _norm_matmul(x, g, gl, w, wl):
    s, k = x.shape
    n = w.shape[2]
    return pl.pallas_call(
        _norm_mm_kernel,
        grid=(s // TM, pl.cdiv(n, TN)),
        in_specs=[pl.BlockSpec((TM, k), lambda i, j: (i, 0)),
                  pl.BlockSpec((None, 1, k), lambda i, j: (gl, 0, 0)),
                  pl.BlockSpec((None, k, TN), lambda i, j: (wl, 0, j))],
        out_specs=pl.BlockSpec((TM, TN), lambda i, j: (i, j)),
        out_shape=jax.ShapeDtypeStruct((s, n), F32),
        scratch_shapes=[pltpu.VMEM((TM, k), BF16)],
        compiler_params=_cparams(("parallel", "arbitrary")),
        name="norm_in_proj",
    )(x, g, w)


def _norm_swiglu_kernel(x_ref, g_ref, wg_ref, wu_ref, o_ref, xn_ref):
    @pl.when(pl.program_id(1) == 0)
    def _():
        xn_ref[...] = _rms_rows(x_ref[...], g_ref[...]).astype(BF16)

    xn = xn_ref[...]
    gate = _dot(xn, wg_ref[...].astype(BF16))
    up = _dot(xn, wu_ref[...].astype(BF16))
    o_ref[...] = (gate * _sigmoid(gate) * up).astype(BF16)


def _norm_swiglu(x, g, w_up, layer):
    s, k = x.shape
    nj = D_FF // TN
    return pl.pallas_call(
        _norm_swiglu_kernel,
        grid=(s // TM, nj),
        in_specs=[pl.BlockSpec((TM, k), lambda i, j: (i, 0)),
                  pl.BlockSpec((None, 1, k), lambda i, j: (layer, 0, 0)),
                  pl.BlockSpec((None, k, TN), lambda i, j: (layer, 0, j)),
                  pl.BlockSpec((None, k, TN), lambda i, j: (layer, 0, j + nj))],
        out_specs=pl.BlockSpec((TM, TN), lambda i, j: (i, j)),
        out_shape=jax.ShapeDtypeStruct((s, D_FF), BF16),
        scratch_shapes=[pltpu.VMEM((TM, k), BF16)],
        compiler_params=_cparams(("parallel", "arbitrary")),
        name="norm_ffn_up",
    )(x, g, w_up, w_up)


def _mm_res_kernel(x_ref, w_ref, r_ref, o_ref):
    o_ref[...] = r_ref[...] + _dot(x_ref[...], w_ref[...].astype(BF16))


def _matmul_residual(x, w, wl, res, tn):
    s, k = x.shape
    n = w.shape[2]
    return pl.pallas_call(
        _mm_res_kernel,
        grid=(s // TM, n // tn),
        in_specs=[pl.BlockSpec((TM, k), lambda i, j: (i, 0)),
                  pl.BlockSpec((None, k, tn), lambda i, j: (wl, 0, j)),
                  pl.BlockSpec((TM, tn), lambda i, j: (i, j))],
        out_specs=pl.BlockSpec((TM, tn), lambda i, j: (i, j)),
        out_shape=jax.ShapeDtypeStruct((s, n), F32),
        compiler_params=_cparams(("parallel", "parallel")),
        name="proj_residual",
    )(x, w, res)


def _mm2_res_kernel(x1_ref, x2_ref, w1_ref, w2_ref, r_ref, o_ref):
    o_ref[...] = r_ref[...] + (_dot(x1_ref[...], w1_ref[...]) + _dot(x2_ref[...], w2_ref[...]))


def _matmul2_residual(x1, x2, w, wl, res):
    s, k = x1.shape
    n = w.shape[2]
    assert x2.shape == (s, k) and w.shape[1] == 2 * k
    return pl.pallas_call(
        _mm2_res_kernel,
        grid=(s // TM, n // TN),
        in_specs=[pl.BlockSpec((TM, k), lambda i, j: (i, 0)),
                  pl.BlockSpec((TM, k), lambda i, j: (i, 0)),
                  pl.BlockSpec((None, k, TN), lambda i, j: (wl, 0, j)),
                  pl.BlockSpec((None, k, TN), lambda i, j: (wl, 1, j)),
                  pl.BlockSpec((TM, TN), lambda i, j: (i, j))],
        out_specs=pl.BlockSpec((TM, TN), lambda i, j: (i, j)),
        out_shape=jax.ShapeDtypeStruct((s, n), F32),
        compiler_params=_cparams(("parallel", "parallel")),
        name="out_proj_residual",
    )(x1, x2, w, w, res)


def _rmsnorm_kernel(x_ref, g_ref, o_ref):
    o_ref[...] = _rms_rows(x_ref[...], g_ref[...])


def _rmsnorm(x, g):
    s, k = x.shape
    return pl.pallas_call(
        _rmsnorm_kernel,
        grid=(s // TM,),
        in_specs=[pl.BlockSpec((TM, k), lambda i: (i, 0)), pl.BlockSpec((1, k), lambda i: (0, 0))],
        out_specs=pl.BlockSpec((TM, k), lambda i: (i, 0)),
        out_shape=jax.ShapeDtypeStruct((s, k), F32),
        compiler_params=_cparams(("parallel",)),
        name="final_norm",
    )(x, g.reshape(1, k))


def _hgrn2_kernel(q_ref, f_ref, v_ref, g_ref, lbt_ref, nw_ref, o_ref, st_ref, *, layer, nchunk):
    @pl.when(pl.program_id(1) == 0)
    def _():
        st_ref[...] = jnp.zeros_like(st_ref)

    t = lbt_ref[...]
    e = jnp.exp(t - jnp.max(t, axis=0, keepdims=True))
    sm = e / jnp.sum(e, axis=0, keepdims=True)
    lb = jnp.zeros((1, HG_D), F32)
    for i in range(1, layer + 1):
        lb = lb + sm[i:i + 1, :]
    log_lb = jnp.log(lb)
    log_1m = jnp.log1p(-lb)
    nw = nw_ref[...]

    tril = _tril(CHUNK).astype(BF16)
    rowid = _iota((SUB, HG_D), 0)
    nsub = CHUNK // SUB
    cs = range(nchunk)
    rows = [slice(c * CHUNK, (c + 1) * CHUNK) for c in cs]

    fp = [f_ref[r, :] for r in rows]
    v = [v_ref[r, :] for r in rows]
    b2 = [log_1m + _log_sigmoid(x) for x in fp]
    log_f = [jnp.maximum(log_lb, x) + jnp.log1p(jnp.exp(-jnp.abs(log_lb - x))) for x in b2]
    k = [(1.0 - lb) * _sigmoid(-x) for x in fp]
    q = [x * _sigmoid(x) for x in (q_ref[r, :] for r in rows)]
    bc = [_cumsum_rows(tril, x) for x in log_f]
    b_last = [x[CHUNK - 1:CHUNK, :] for x in bc]
    kv = [_mm(v[c], k[c] * jnp.exp(b_last[c] - bc[c]), "bf16", "tn") for c in cs]
    st = st_ref[...]
    starts = []
    for c in cs:
        starts.append(st)
        st = jnp.exp(b_last[c]) * st + kv[c]
    st_ref[...] = st
    acc = [_mm(q[c] * jnp.exp(bc[c]), starts[c], "bf16", "nt") for c in cs]
    acc = [[a[blk * SUB:(blk + 1) * SUB] for blk in range(nsub)] for a in acc]
    cb = [(c, blk) for blk in range(1, nsub) for c in cs]
    ref_b = {(c, blk): bc[c][blk * SUB - 1:blk * SUB, :] for c, blk in cb}
    qx = {(c, blk): q[c][blk * SUB:(blk + 1) * SUB] * jnp.exp(bc[c][blk * SUB:(blk + 1) * SUB] - ref_b[c, blk])
          for c, blk in cb}
    kx = {(c, blk): k[c][0:blk * SUB] * jnp.exp(ref_b[c, blk] - bc[c][0:blk * SUB]) for c, blk in cb}
    sc = {e: _mm(qx[e], kx[e], "bf16", "nt") for e in cb}
    od = {(c, blk): _mm(sc[c, blk], v[c][0:blk * SUB], "bf16") for c, blk in cb}
    for c, blk in cb:
        acc[c][blk] = acc[c][blk] + od[c, blk]
    for s in range(SUB):
        for c in cs:
            for blk in range(nsub):
                lo = blk * SUB
                b_i = bc[c][lo:lo + SUB]
                d = b_i - bc[c][lo + s:lo + s + 1, :]
                if s > 0:
                    d = jnp.where(rowid >= s, d, -jnp.inf)
                col = jnp.sum(q[c][lo:lo + SUB] * k[c][lo + s:lo + s + 1, :] * jnp.exp(d), axis=1, keepdims=True)
                acc[c][blk] = acc[c][blk] + col * v[c][lo + s:lo + s + 1, :]
    for c in cs:
        o = jnp.concatenate(acc[c], axis=0)
        y = o * lax.rsqrt(jnp.mean(o * o, axis=-1, keepdims=True) + EPS) * nw
        o_ref[rows[c], :] = (y * _sigmoid(g_ref[rows[c], :])).astype(BF16)


def _hgrn2(p, lb_table, norm_w, layer, rb):
    s = p.shape[0]
    nb = HG_HEADS
    col = lambda off: pl.BlockSpec((rb, HG_D), lambda h, r, off=off: (r, off + h))
    return pl.pallas_call(
        functools.partial(_hgrn2_kernel, layer=layer, nchunk=rb // CHUNK),
        grid=(HG_HEADS, s // rb),
        in_specs=[col(0), col(nb), col(2 * nb), col(3 * nb),
                  pl.BlockSpec((DEPTH, HG_D), lambda h, r: (0, h)),
                  pl.BlockSpec((None, 1, HG_D), lambda h, r: (layer // 2, 0, h))],
        out_specs=pl.BlockSpec((rb, HG_D), lambda h, r: (r, h)),
        out_shape=jax.ShapeDtypeStruct((s, HG_HEADS * HG_D), BF16),
        scratch_shapes=[pltpu.VMEM((HG_D, HG_D), F32)],
        compiler_params=_cparams(("parallel", "arbitrary")),
        name="hgrn2",
    )(p, p, p, p, lb_table, norm_w)


def _shift_lerp(x, prev_row, mu):
    rolled = pltpu.roll(x, 1, 0)
    shifted = jnp.where(_iota(x.shape, 0) == 0, prev_row, rolled)
    return x + (shifted - x) * mu


def _split_bf16(x):
    hi = x.astype(BF16)
    lo = (x - hi.astype(F32)).astype(BF16)
    return hi, lo


def _mm(a, b, mode, kind="nn"):
    dot = {"nn": _dot, "nt": _dot_nt, "tn": _dot_tn}[kind]
    if mode == "hi":
        return dot(a, b, HI)
    if mode == "bf16":
        return dot(a.astype(BF16), b.astype(BF16))
    ah, al = _split_bf16(a)
    bh, bl = _split_bf16(b)
    return dot(ah, bh) + (dot(ah, bl) + dot(al, bh))


def _cumsum_rows(tril_bf16, x):
    hi, lo = _split_bf16(x)
    lo2 = (x - hi.astype(F32) - lo.astype(F32)).astype(BF16)
    return _dot(tril_bf16, hi) + (_dot(tril_bf16, lo) + _dot(tril_bf16, lo2))


def _neumann_inverse(mats, mode):
    n = mats[0].shape[0]
    eye = (_iota((n, n), 0) == _iota((n, n), 1)).astype(F32)
    p = [_mm(x, x, mode) for x in mats]
    m = [eye + x for x in mats]
    for _ in range(4):
        both = [_mm(x, jnp.concatenate([x, y], axis=1), mode) for x, y in zip(p, m)]
        m = [y + b[:, n:] for y, b in zip(m, both)]
        p = [b[:, :n] for b in both]
    return [y + _mm(x, y, mode) for x, y in zip(p, m)]


RW_PREC = dict(score="bf16", neumann="bf16", apply="bf16", state="bf16")


def _rwkv7_kernel(r_ref, k_ref, v_ref, tail_ref,
                  mu_r_ref, mu_k_ref, mu_v_ref, mu_tail_ref,
                  w0_ref, a0_ref, w2_ref, a2_ref, g2_ref, kk_ref, ka_ref, rk_ref, lnw_ref, lnb_ref,
                  o_ref, st_ref, prev_ref, *, nchunk, npair):
    rb = r_ref.shape[0]
    pc = RW_PREC
    wide = npair * LANES

    @pl.when(pl.program_id(1) == 0)
    def _():
        st_ref[...] = jnp.zeros_like(st_ref)
        prev_ref[...] = jnp.zeros_like(prev_ref)

    lane = _iota((1, LANES), 1)
    head0 = lane < RW_N

    def head_sum(x):
        s0 = jnp.sum(jnp.where(head0, x, 0.0), axis=1, keepdims=True)
        s1 = jnp.sum(jnp.where(head0, 0.0, x), axis=1, keepdims=True)
        return jnp.where(head0, s0, s1)

    tw = tail_ref.shape[1]
    tail_raw = jnp.where(_iota((1, tw), 1) < RW_TAIL_COLS, tail_ref[...], 0.0)
    r_raw, k_raw, v_raw = r_ref[...], k_ref[...], v_ref[...]
    r_all = _shift_lerp(r_raw, prev_ref[0:1, 0:wide], mu_r_ref[...])
    k_all = _shift_lerp(k_raw, prev_ref[1:2, 0:wide], mu_k_ref[...])
    v_all = _shift_lerp(v_raw, prev_ref[2:3, 0:wide], mu_v_ref[...])
    tail = _shift_lerp(tail_raw, prev_ref[3:4, 0:tw], mu_tail_ref[...])
    wa, gl = tail[:, 0:LANES], tail[:, LANES:3 * LANES]
    prev_ref[0:1, 0:wide] = r_raw[rb - 1:rb]
    prev_ref[1:2, 0:wide] = k_raw[rb - 1:rb]
    prev_ref[2:3, 0:wide] = v_raw[rb - 1:rb]
    prev_ref[3:4, 0:tw] = tail_raw[rb - 1:rb]

    w_log = -_softplus(-(w0_ref[...] + _dot(jnp.tanh(wa), w2_ref[...]))) - 0.5
    lw_all = -jnp.exp(w_log)
    a_all = _sigmoid(a0_ref[...] + _dot(wa, a2_ref[...]))
    gate_all = _dot(_sigmoid(gl), g2_ref[...])
    kk_all = k_all * kk_ref[...]
    k_all = k_all * (1.0 + (a_all - 1.0) * ka_ref[...])
    rk_all = r_all * k_all * rk_ref[...]
    ps = range(npair)
    tile = lambda x, p: x[:, p * LANES:(p + 1) * LANES]
    r, k, v, a, lw = ([tile(x, p) for p in ps] for x in (r_all, k_all, v_all, a_all, lw_all))
    kk = [tile(kk_all, p) for p in ps]
    kk = [x / jnp.maximum(jnp.sqrt(head_sum(x * x)), 1e-12) for x in kk]
    bonus = [head_sum(tile(rk_all, p)) * v[p] for p in ps]

    head1 = jnp.logical_not(head0)
    bd = (_iota((LANES, LANES), 0) < RW_N) == (_iota((LANES, LANES), 1) < RW_N)
    gs = 2 * CHUNK
    ri, ci = _iota((gs, gs), 0), _iota((gs, gs), 1)
    same = (ri // CHUNK) == (ci // CHUNK)
    strict = jnp.logical_and(same, ri > ci)
    incl = jnp.logical_and(same, ri >= ci)
    tril = _tril(CHUNK).astype(BF16)
    head00 = jnp.concatenate([head0, head0], axis=1)

    def stack_heads(x, masked):
        return jnp.concatenate([jnp.where(head0, x, 0.0), jnp.where(head1, x, 0.0)] if masked else [x, x], axis=0)

    def unstack_heads(x):
        return jnp.where(head0 if x.shape[1] == LANES else head00, x[:CHUNK], x[CHUNK:])

    cs = range(nchunk)
    grp = [(p, c) for c in cs for p in ps]
    rows = [slice(c * CHUNK, (c + 1) * CHUNK) for c in cs]
    cut = lambda x: {(p, c): x[p][rows[c]] for p, c in grp}
    r, k, v, a, lw, kk = cut(r), cut(k), cut(v), cut(a), cut(lw), cut(kk)
    g = {e: _cumsum_rows(tril, lw[e]) for e in grp}
    ieg = {e: jnp.exp(-g[e]) for e in grp}
    rt = {e: r[e] * jnp.exp(g[e]) for e in grp}
    at = {e: -kk[e] * jnp.exp(g[e] - lw[e]) for e in grp}
    bt = {e: kk[e] * a[e] * ieg[e] for e in grp}
    kt = {e: k[e] * ieg[e] for e in grp}
    l_a = {e: stack_heads(at[e], True) for e in grp}
    l_r = {e: stack_heads(rt[e], True) for e in grp}
    r_bk = {e: jnp.concatenate([stack_heads(bt[e], True), stack_heads(kt[e], True)], axis=0) for e in grp}
    s_a = {e: _mm(l_a[e], r_bk[e], pc["score"], "nt") for e in grp}
    s_r = {e: _mm(l_r[e], r_bk[e], pc["score"], "nt") for e in grp}
    a_ab = {e: jnp.where(strict, s_a[e][:, :gs], 0.0) for e in grp}
    a_ak = {e: jnp.where(strict, s_a[e][:, gs:], 0.0) for e in grp}
    rab = {e: jnp.where(incl, s_r[e][:, :gs], 0.0) for e in grp}
    rak = {e: jnp.where(incl, s_r[e][:, gs:], 0.0) for e in grp}
    v_st = {e: stack_heads(v[e], False) for e in grp}
    aakv = {e: _mm(a_ak[e], v_st[e], pc["apply"]) for e in grp}
    rakv = {e: unstack_heads(_mm(rak[e], v_st[e], pc["apply"])) for e in grp}
    t_inv = dict(zip(grp, _neumann_inverse([a_ab[e] for e in grp], pc["neumann"])))
    xs = {e: unstack_heads(_mm(t_inv[e], jnp.concatenate([stack_heads(at[e], False), aakv[e]], axis=1), pc["apply"]))
          for e in grp}
    ta = {e: xs[e][:, :LANES] for e in grp}
    tav = {e: xs[e][:, LANES:] for e in grp}
    egl = {e: jnp.exp(g[e][CHUNK - 1:CHUNK, :]) for e in grp}
    bte = {e: bt[e] * egl[e] for e in grp}
    m_mat = {e: jnp.where(bd, _mm(ta[e], bte[e], pc["state"], "tn"), 0.0) for e in grp}
    c_mat = {e: jnp.where(bd, _mm(jnp.concatenate([tav[e], v[e]], axis=0),
                                  jnp.concatenate([bte[e], kt[e] * egl[e]], axis=0), pc["state"], "tn"), 0.0)
             for e in grp}

    st = [st_ref[p] for p in ps]
    start = {}
    for c in cs:
        for p in ps:
            start[p, c] = st[p]
            st[p] = st[p] * egl[p, c] + _mm(st[p], m_mat[p, c], pc["state"]) + c_mat[p, c]
    for p in ps:
        st_ref[p] = st[p]
    u = {e: _mm(ta[e], start[e], pc["state"], "nt") + tav[e] for e in grp}
    o_in = {e: _mm(rt[e], start[e], pc["state"], "nt") for e in grp}
    y = {e: _mm(rab[e], stack_heads(u[e], True), pc["apply"]) for e in grp}
    o = {e: o_in[e] + (y[e][:CHUNK] + y[e][CHUNK:]) + rakv[e] for e in grp}

    for p in ps:
        ts = slice(p * LANES, (p + 1) * LANES)
        o_p = jnp.concatenate([o[p, c] for c in cs], axis=0)
        mu = head_sum(o_p) * (1.0 / RW_N)
        d = o_p - mu
        var = head_sum(d * d) * (1.0 / RW_N)
        y_p = d * lax.rsqrt(var + RW_GN_EPS) * lnw_ref[:, ts] + lnb_ref[:, ts]
        o_ref[:, ts] = ((y_p + bonus[p]) * gate_all[:, ts]).astype(BF16)


def _rwkv7_params(mu, w0, w2, a0, a2, g2, k_k, k_a, r_k, ln_w, ln_b):
    n = mu.shape[0]
    row = lambda x: x.reshape(n, 1, -1)
    assert 2 * w2.shape[1] == LANES and 2 * a2.shape[1] == LANES
    mu_p = jnp.pad(mu, ((0, 0), (0, 3 * RW_W + RW_TAIL_W - mu.shape[1])))
    return dict(mu=row(mu_p), w0=row(w0), a0=row(a0),
                w2=jnp.concatenate([w2, jnp.zeros_like(w2)], axis=1),
                a2=jnp.concatenate([jnp.zeros_like(a2), a2], axis=1),
                g2=jnp.pad(g2, ((0, 0), (0, 2 * LANES - g2.shape[1]), (0, 0))),
                k_k=row(k_k), k_a=row(k_a), r_k=row(r_k), ln_w=row(ln_w), ln_b=row(ln_b))


RW_PAIRS = 4


def _rwkv7(p, prm, j, rb):
    s = p.shape[0]
    wide = RW_PAIRS * LANES
    base = 4 * HG_HEADS * LANES // wide
    nb = RW_W // wide
    col = lambda off: pl.BlockSpec((rb, wide), lambda h, r, off=off: (r, off + h))
    par = pl.BlockSpec((None, 1, wide), lambda h, r: (j, 0, h))
    par_off = lambda off: pl.BlockSpec((None, 1, wide), lambda h, r, off=off: (j, 0, off + h))
    lora = lambda rows: pl.BlockSpec((None, rows, wide), lambda h, r: (j, 0, h))
    assert RW_TAIL % RW_TAIL_W == 0 and (3 * RW_W) % RW_TAIL_W == 0
    return pl.pallas_call(
        functools.partial(_rwkv7_kernel, nchunk=rb // CHUNK, npair=RW_PAIRS),
        grid=(nb, s // rb),
        in_specs=[col(base), col(base + nb), col(base + 2 * nb),
                  pl.BlockSpec((rb, RW_TAIL_W), lambda h, r: (r, RW_TAIL // RW_TAIL_W)),
                  par, par_off(nb), par_off(2 * nb),
                  pl.BlockSpec((None, 1, RW_TAIL_W), lambda h, r: (j, 0, 3 * RW_W // RW_TAIL_W)),
                  par, par, lora(LANES), lora(LANES), lora(2 * LANES),
                  par, par, par, par, par],
        out_specs=pl.BlockSpec((rb, wide), lambda h, r: (r, h)),
        out_shape=jax.ShapeDtypeStruct((s, RW_W), BF16),
        scratch_shapes=[pltpu.VMEM((RW_PAIRS, LANES, LANES), F32), pltpu.VMEM((8, max(wide, RW_TAIL_W)), F32)],
        compiler_params=_cparams(("parallel", "arbitrary")),
        name="rwkv7",
    )(p, p, p, p, prm["mu"], prm["mu"], prm["mu"], prm["mu"],
      prm["w0"], prm["a0"], prm["w2"], prm["a2"], prm["g2"],
      prm["k_k"], prm["k_a"], prm["r_k"], prm["ln_w"], prm["ln_b"])


def _lane_col(x, idx):
    return jnp.sum(jnp.where(_iota(x.shape, 1) == idx, x, 0.0), axis=1, keepdims=True)


def _transpose_rows(x):
    eye = (_iota((LANES, LANES), 0) == _iota((LANES, LANES), 1)).astype(BF16)
    hi, lo = _split_bf16(x)
    lo2 = (x - hi.astype(F32) - lo.astype(F32)).astype(BF16)
    return _dot_nt(eye, hi) + (_dot_nt(eye, lo) + _dot_nt(eye, lo2))


def _mlstm_kernel(q_ref, k_ref, v_ref, og_ref, sm_ref, bias_ref, nw_ref, o_ref, c_ref, n_ref, m_ref, *, nchunk):
    @pl.when(pl.program_id(0) == 0)
    def _():
        c_ref[...] = jnp.zeros_like(c_ref)
        n_ref[...] = jnp.zeros_like(n_ref)
        m_ref[...] = jnp.zeros_like(m_ref)

    incl = _tril(CHUNK)
    tril = incl.astype(BF16)
    lane = _iota((CHUNK, LANES), 1)
    is_f = jnp.logical_and(lane >= ML_HEADS, lane < 2 * ML_HEADS)
    scale = ML_DQK ** -0.5
    cs = range(nchunk)
    hs = range(ML_HEADS)
    rows = [slice(c * CHUNK, (c + 1) * CHUNK) for c in cs]
    ch = [(c, h) for c in cs for h in hs]

    pre = [sm_ref[r, :] + bias_ref[...] for r in rows]
    cap = [ML_CAP * jnp.tanh(x / ML_CAP) for x in pre]
    x = [jnp.where(is_f, _log_sigmoid(y), y) for y in cap]
    cum = [_cumsum_rows(tril, y) for y in x]
    x_t = [_transpose_rows(y) for y in x]
    cum_t = [_transpose_rows(y) for y in cum]

    q = {(c, h): q_ref[rows[c], h * ML_DQK:(h + 1) * ML_DQK] * scale for c, h in ch}
    k = {(c, h): k_ref[rows[c], h * ML_DQK:(h + 1) * ML_DQK] for c, h in ch}
    v = {(c, h): v_ref[rows[c], h * ML_DV:(h + 1) * ML_DV] for c, h in ch}
    b_col = {(c, h): _lane_col(cum[c], ML_HEADS + h) for c, h in ch}
    i_col = {(c, h): _lane_col(x[c], h) for c, h in ch}
    b_row = {(c, h): cum_t[c][ML_HEADS + h:ML_HEADS + h + 1, :] for c, h in ch}
    i_row = {(c, h): x_t[c][h:h + 1, :] for c, h in ch}
    qk = {e: _mm(q[e], k[e], "bf16", "nt") for e in ch}
    dmat = {e: jnp.where(incl, b_col[e] - b_row[e] + i_row[e], -jnp.inf) for e in ch}
    dmax = {e: jnp.max(dmat[e], axis=1, keepdims=True) for e in ch}
    b_last = {e: b_row[e][:, CHUNK - 1:CHUNK] for e in ch}
    src_row = {e: b_last[e] - b_row[e] + i_row[e] for e in ch}
    src_col = {e: b_last[e] - b_col[e] + i_col[e] for e in ch}
    src_max = {e: jnp.max(src_row[e], axis=1, keepdims=True) for e in ch}

    m_prev, m_new = {}, {}
    for h in hs:
        m = m_ref[h:h + 1, 0:1]
        for c in cs:
            m_prev[c, h] = m
            m = jnp.maximum(b_last[c, h] + m, src_max[c, h])
            m_new[c, h] = m
        m_ref[h:h + 1, :] = jnp.broadcast_to(m, (1, LANES))
    inter = {e: b_col[e] + m_prev[e] for e in ch}
    m_t = {e: jnp.maximum(inter[e], dmax[e]) for e in ch}
    w_inter = {e: jnp.exp(inter[e] - m_t[e]) for e in ch}
    pmat = {e: jnp.exp(dmat[e] - m_t[e]) * qk[e] for e in ch}
    num = {e: _mm(pmat[e], v[e], "bf16") for e in ch}
    den = {e: jnp.sum(pmat[e], axis=1, keepdims=True) for e in ch}
    decay = {e: jnp.exp(b_last[e] + m_prev[e] - m_new[e]) for e in ch}
    wk = {e: jnp.exp(src_col[e] - m_new[e]) * k[e] for e in ch}
    kv = {e: _mm(wk[e], v[e], "bf16", "tn") for e in ch}
    ksum = {e: jnp.sum(wk[e], axis=0, keepdims=True) for e in ch}

    c_start, n_start = {}, {}
    for h in hs:
        c_mat, n_row = c_ref[h], n_ref[h:h + 1, :]
        for c in cs:
            c_start[c, h], n_start[c, h] = c_mat, n_row
            c_mat = decay[c, h] * c_mat + kv[c, h]
            n_row = decay[c, h] * n_row + ksum[c, h]
        c_ref[h] = c_mat
        n_ref[h:h + 1, :] = n_row

    qc = {e: _mm(q[e], c_start[e], "bf16") for e in ch}
    vs = {h: slice(h * ML_DV, (h + 1) * ML_DV) for h in hs}
    qn = {e: jnp.sum(q[e] * n_start[e], axis=1, keepdims=True) for e in ch}
    num = {e: num[e] + w_inter[e] * qc[e] for e in ch}
    den = {e: den[e] + w_inter[e] * qn[e] for e in ch}
    h_out = {e: num[e] / jnp.maximum(jnp.abs(den[e]), jnp.exp(-m_t[e])) for e in ch}
    ms = {e: jnp.mean(h_out[e] * h_out[e], axis=-1, keepdims=True) for e in ch}
    gate = {(c, h): _sigmoid(og_ref[rows[c], vs[h]]) for c, h in ch}
    y = {(c, h): h_out[c, h] * lax.rsqrt(ms[c, h] + EPS) * nw_ref[:, vs[h]] * gate[c, h] for c, h in ch}
    for c, h in ch:
        o_ref[rows[c], vs[h]] = y[c, h].astype(BF16)


def _mlstm(p, bias_row, norm_w, j, rb):
    s = p.shape[0]
    nq = ML_HEADS * ML_DQK
    nv = ML_HEADS * ML_DV
    return pl.pallas_call(
        functools.partial(_mlstm_kernel, nchunk=rb // CHUNK),
        grid=(s // rb,),
        in_specs=[pl.BlockSpec((rb, nq), lambda r: (r, 0)),
                  pl.BlockSpec((rb, nq), lambda r: (r, 1)),
                  pl.BlockSpec((rb, nv), lambda r: (r, 1)),
                  pl.BlockSpec((rb, nv), lambda r: (r, 2)),
                  pl.BlockSpec((rb, LANES), lambda r: (r, OD_SMALL // LANES)),
                  pl.BlockSpec((None, 1, LANES), lambda r: (j, 0, 0)),
                  pl.BlockSpec((None, 1, nv), lambda r: (j, 0, 0))],
        out_specs=pl.BlockSpec((rb, nv), lambda r: (r, 0)),
        out_shape=jax.ShapeDtypeStruct((s, nv), BF16),
        scratch_shapes=[pltpu.VMEM((ML_HEADS, ML_DQK, ML_DV), F32),
                        pltpu.VMEM((8, ML_DQK), F32), pltpu.VMEM((8, LANES), F32)],
        compiler_params=_cparams(("arbitrary",)),
        name="mlstm",
    )(p, p, p, p, p, bias_row, norm_w)


def _ssd_kernel(z_ref, xbc_ref, sm_ref, cw_ref, cb_ref, bias_ref, negA_ref, dvec_ref, nw_ref, o_ref,
                st_ref, xin_s, xc_s, *, nchunk):
    rb = z_ref.shape[0]
    pad = 8

    @pl.when(pl.program_id(0) == 0)
    def _():
        st_ref[...] = jnp.zeros_like(st_ref)
        xin_s[rb:rb + pad, :] = jnp.zeros((pad, xin_s.shape[1]), F32)

    xin_s[0:pad, :] = xin_s[rb:rb + pad, :]
    xin_s[pad:rb + pad, :] = xbc_ref[...]
    xin = xin_s[...]
    acc = cb_ref[...] + xin[pad:, :] * cw_ref[MB_CONV - 1:MB_CONV, :]
    for j in range(MB_CONV - 1):
        acc = acc + pltpu.roll(xin, MB_CONV - 1 - j, 0)[pad:, :] * cw_ref[j:j + 1, :]
    xc_s[...] = acc * _sigmoid(acc)

    incl = _tril(CHUNK)
    tril = incl.astype(BF16)
    lane = _iota((1, LANES), 1)
    head0 = lane < MB_P
    sub0 = _iota((LANES, 1), 0) < MB_P
    hpg = MB_HEADS // MB_GROUPS
    gw = MB_DI // MB_GROUPS
    dt_lane0 = 2 * ML_HEADS
    npair = hpg // 2
    cs = range(nchunk)
    rows = [slice(c * CHUNK, (c + 1) * CHUNK) for c in cs]
    cg = [(c, g) for c in cs for g in range(MB_GROUPS)]
    cgj = [(c, g, j) for c, g in cg for j in range(npair)]

    dt = [_softplus(sm_ref[r, :] + bias_ref[...]) for r in rows]
    a_cum = [_cumsum_rows(tril, negA_ref[...] * x) for x in dt]
    a_t = [_transpose_rows(x) for x in a_cum]
    dt_t = [_transpose_rows(x) for x in dt]
    bm = {(c, g): xc_s[rows[c], MB_DI + g * MB_N:MB_DI + (g + 1) * MB_N] for c, g in cg}
    cm = {(c, g): xc_s[rows[c], MB_DI + (MB_GROUPS + g) * MB_N:MB_DI + (MB_GROUPS + g + 1) * MB_N] for c, g in cg}
    cb = {e: _mm(cm[e], bm[e], "bf16", "nt") for e in cg}
    xs = {(g, j): slice(g * gw + j * LANES, g * gw + (j + 1) * LANES) for g in range(MB_GROUPS) for j in range(npair)}
    x2 = {(c, g, j): xc_s[rows[c], xs[g, j]] for c, g, j in cgj}

    sc, ea, ws, el = {}, {}, {}, {}
    for c, g, j in cgj:
        for e in range(2):
            ln = dt_lane0 + g * hpg + 2 * j + e
            a_col = _lane_col(a_cum[c], ln)
            a_row = a_t[c][ln:ln + 1, :]
            a_last = a_row[:, CHUNK - 1:CHUNK]
            seg = jnp.exp(jnp.where(incl, a_col - a_row, -jnp.inf))
            sc[c, g, j, e] = seg * cb[c, g] * dt_t[c][ln:ln + 1, :]
            ea[c, g, j, e] = jnp.exp(a_col)
            ws[c, g, j, e] = jnp.exp(a_last - a_col) * _lane_col(dt[c], ln)
            el[c, g, j, e] = jnp.exp(a_last)
    y0 = {e: _mm(sc[e + (0,)], x2[e], "bf16") for e in cgj}
    y1 = {e: _mm(sc[e + (1,)], x2[e], "bf16") for e in cgj}
    kv = {(c, g, j): _mm(x2[c, g, j] * jnp.where(head0, ws[c, g, j, 0], ws[c, g, j, 1]), bm[c, g], "bf16", "tn")
          for c, g, j in cgj}

    start = {}
    for g in range(MB_GROUPS):
        for j in range(npair):
            si = g * npair + j
            st = st_ref[si]
            for c in cs:
                start[c, g, j] = st
                st = jnp.where(sub0, el[c, g, j, 0], el[c, g, j, 1]) * st + kv[c, g, j]
            st_ref[si] = st

    ycs = {(c, g, j): _mm(cm[c, g], start[c, g, j], "bf16", "nt") for c, g, j in cgj}
    y = {e: jnp.where(head0, y0[e], y1[e]) + jnp.where(head0, ea[e + (0,)], ea[e + (1,)]) * ycs[e]
         + x2[e] * dvec_ref[:, xs[e[1], e[2]]] for e in cgj}
    gs = {g: slice(g * gw, (g + 1) * gw) for g in range(MB_GROUPS)}
    zg = {(c, g): z_ref[rows[c], gs[g]] for c, g in cg}
    yg = {(c, g): jnp.concatenate([y[c, g, j] for j in range(npair)], axis=1) * (zg[c, g] * _sigmoid(zg[c, g]))
          for c, g in cg}
    ms = {e: jnp.mean(yg[e] * yg[e], axis=-1, keepdims=True) for e in cg}
    out = {(c, g): yg[c, g] * lax.rsqrt(ms[c, g] + EPS) * nw_ref[:, gs[g]] for c, g in cg}
    for c, g in cg:
        o_ref[rows[c], gs[g]] = out[c, g].astype(BF16)


def _ssd(p, conv_w, conv_b, bias_row, negA_row, d_row, norm_w, j, rb):
    s = p.shape[0]
    nx = conv_w.shape[2]
    prow = lambda c: pl.BlockSpec((None, 1, c), lambda r: (j, 0, 0))
    return pl.pallas_call(
        functools.partial(_ssd_kernel, nchunk=rb // CHUNK),
        grid=(s // rb,),
        in_specs=[pl.BlockSpec((rb, MB_DI), lambda r: (r, 3)),
                  pl.BlockSpec((rb, nx), lambda r: (r, OD_XBC // nx)),
                  pl.BlockSpec((rb, LANES), lambda r: (r, OD_SMALL // LANES)),
                  pl.BlockSpec((None, MB_CONV, nx), lambda r: (j, 0, 0)),
                  prow(nx), prow(LANES), prow(LANES), prow(MB_DI), prow(MB_DI)],
        out_specs=pl.BlockSpec((rb, MB_DI), lambda r: (r, 0)),
        out_shape=jax.ShapeDtypeStruct((s, MB_DI), BF16),
        scratch_shapes=[pltpu.VMEM((MB_HEADS // 2, 2 * MB_P, MB_N), F32),
                        pltpu.VMEM((rb + 8, nx), F32), pltpu.VMEM((rb, nx), F32)],
        compiler_params=_cparams(("arbitrary",)),
        name="ssd",
    )(p, p, p, conv_w, conv_b, bias_row, negA_row, d_row, norm_w)


def _rows3(x):
    return x.reshape(x.shape[0], 1, -1)


def _odd_in_weight(w):
    head_w = 2048 + 1024 + 1024 + 8 + 16
    return jnp.concatenate([w[..., 0:2048], w[..., 2056:3080], w[..., 3080:4104], w[..., 2048:2056], w[..., 5640:5656],
                            jnp.zeros(w.shape[:2] + (OD_XBC - head_w,), w.dtype), w[..., 4104:5640]],
                           axis=-1).astype(BF16)


def _small_rows(*xs):
    row = jnp.concatenate(xs, axis=1)
    return _rows3(jnp.pad(row, ((0, 0), (0, LANES - row.shape[1]))))


def kernel(x, norm_mix, norm_ffn, norm_final, w_in_even, w_out_even, hg_lb_table, hg_norm, rw_mu, rw_w0, rw_w2, rw_a0, rw_a2, rw_g2, rw_k_k, rw_k_a, rw_r_k, rw_ln_w, rw_ln_b, w_in_odd, w_out_odd, ml_i_bias, ml_f_bias, ml_norm, mb_conv_w, mb_conv_b, mb_dt_bias, mb_A_log, mb_D, mb_norm, ffn_w_up, ffn_w_down):
    b, s, d = x.shape
    assert b == 1 and d == D_MODEL and s % TM == 0
    rb_hg, rb_rw, rb_ml, rb_ssd = min(s, 256), min(s, 512), min(s, 256), min(s, 256)

    g_mix, g_ffn = _rows3(norm_mix), _rows3(norm_ffn)
    w_in_e, w_in_o = w_in_even.astype(BF16), _odd_in_weight(w_in_odd)
    w_out_e, w_out_o = w_out_even.astype(BF16), w_out_odd.astype(BF16)
    rw = _rwkv7_params(rw_mu, rw_w0, rw_w2, rw_a0, rw_a2, rw_g2, rw_k_k, rw_k_a, rw_r_k, rw_ln_w, rw_ln_b)
    zeros8 = jnp.zeros((ml_i_bias.shape[0], 2 * ML_HEADS), F32)
    ml_bias = _small_rows(ml_i_bias, ml_f_bias)
    dt_bias = _small_rows(zeros8, mb_dt_bias)
    neg_a = _small_rows(zeros8, -jnp.exp(mb_A_log.astype(F32)))
    d_rows = _rows3(jnp.repeat(mb_D, MB_P, axis=1))

    h = x.reshape(s, d)
    for layer in range(DEPTH):
        j = layer // 2
        if layer % 2 == 0:
            p = _norm_matmul(h, g_mix, layer, w_in_e, j)
            o_a = _hgrn2(p, hg_lb_table, _rows3(hg_norm), layer, rb_hg)
            o_b = _rwkv7(p, rw, j, rb_rw)
            h = _matmul2_residual(o_a, o_b, w_out_e, j, h)
        else:
            p = _norm_matmul(h, g_mix, layer, w_in_o, j)
            o_a = _mlstm(p, ml_bias, _rows3(ml_norm), j, rb_ml)
            o_b = _ssd(p, mb_conv_w, _rows3(mb_conv_b), dt_bias, neg_a, d_rows, _rows3(mb_norm), j, rb_ssd)
            h = _matmul2_residual(o_a, o_b, w_out_o, j, h)
        act = _norm_swiglu(h, g_ffn, ffn_w_up, layer)
        h = _matmul_residual(act, ffn_w_down, layer, h, TN // 2)
    return _rmsnorm(h, norm_final).reshape(b, s, d)
```

```python
import functools

import jax
import jax.numpy as jnp
from jax import lax
from jax.experimental import pallas as pl
from jax.experimental.pallas import tpu as pltpu

F32 = jnp.float32
BF16 = jnp.bfloat16
HI = lax.Precision.HIGHEST

D_MODEL = 2048
DEPTH = 4
CHUNK = 64
EPS = 1e-6
LANES = 128
SUB = 16

HG_HEADS, HG_D = 8, 128
RW_HEADS, RW_N, RW_W = 16, 64, 1024
RW_GN_EPS = 64e-5
ML_HEADS, ML_DQK, ML_DV = 4, 128, 256
ML_CAP = 15.0
MB_HEADS, MB_P, MB_N, MB_GROUPS, MB_CONV = 16, 64, 128, 2, 4
MB_DI = MB_HEADS * MB_P
D_FF = 5632

NP_ODD = 6144
RW_TAIL = 7168
RW_TAIL_COLS = 288
RW_TAIL_W = 512
OD_SMALL = 4096
OD_XBC = 4608

TM = 1024
TN = 512
VMEM_LIMIT = 56 * 1024 * 1024


def _cparams(sem):
    return pltpu.CompilerParams(dimension_semantics=sem, vmem_limit_bytes=VMEM_LIMIT)


def _dot(a, b, prec=None):
    return jnp.dot(a, b, preferred_element_type=F32, precision=prec)


def _dot_nt(a, b, prec=None):
    return lax.dot_general(a, b, (((1,), (1,)), ((), ())), preferred_element_type=F32, precision=prec)


def _dot_tn(a, b, prec=None):
    return lax.dot_general(a, b, (((0,), (0,)), ((), ())), preferred_element_type=F32, precision=prec)


def _sigmoid(x):
    return 1.0 / (1.0 + jnp.exp(-x))


def _log_sigmoid(x):
    return jnp.minimum(x, 0.0) - jnp.log1p(jnp.exp(-jnp.abs(x)))


def _softplus(x):
    return jnp.maximum(x, 0.0) + jnp.log1p(jnp.exp(-jnp.abs(x)))


def _iota(shape, dim):
    return lax.broadcasted_iota(jnp.int32, shape, dim)


def _tril(n, strict=False):
    r, c = _iota((n, n), 0), _iota((n, n), 1)
    return (r > c) if strict else (r >= c)


def _rms_rows(x, g):
    return (x * lax.rsqrt(jnp.mean(x * x, axis=-1, keepdims=True) + EPS)) * g


def _norm_mm_kernel(x_ref, g_ref, w_ref, o_ref, xn_ref):
    @pl.when(pl.program_id(1) == 0)
    def _():
        xn_ref[...] = _rms_rows(x_ref[...], g_ref[...]).astype(BF16)

    o_ref[...] = _dot(xn_ref[...], w_ref[...])


def _norm_matmul(x, g, gl, w, wl):
    s, k = x.shape
    n = w.shape[2]
    return pl.pallas_call(
        _norm_mm_kernel,
        grid=(s // TM, pl.cdiv(n, TN)),
        in_specs=[pl.BlockSpec((TM, k), lambda i, j: (i, 0)),
                  pl.BlockSpec((None, 1, k), lambda i, j: (gl, 0, 0)),
                  pl.BlockSpec((None, k, TN), lambda i, j: (wl, 0, j))],
        out_specs=pl.BlockSpec((TM, TN), lambda i, j: (i, j)),
        out_shape=jax.ShapeDtypeStruct((s, n), F32),
        scratch_shapes=[pltpu.VMEM((TM, k), BF16)],
        compiler_params=_cparams(("parallel", "arbitrary")),
        name="norm_in_proj",
    )(x, g, w)


def _norm_swiglu_kernel(x_ref, g_ref, wg_ref, wu_ref, o_ref, xn_ref):
    @pl.when(pl.program_id(1) == 0)
    def _():
        xn_ref[...] = _rms_rows(x_ref[...], g_ref[...]).astype(BF16)

    xn = xn_ref[...]
    gate = _dot(xn, wg_ref[...].astype(BF16))
    up = _dot(xn, wu_ref[...].astype(BF16))
    o_ref[...] = (gate * _sigmoid(gate) * up).astype(BF16)


def _norm_swiglu(x, g, w_up, layer):
    s, k = x.shape
    nj = D_FF // TN
    return pl.pallas_call(
        _norm_swiglu_kernel,
        grid=(s // TM, nj),
        in_specs=[pl.BlockSpec((TM, k), lambda i, j: (i, 0)),
                  pl.BlockSpec((None, 1, k), lambda i, j: (layer, 0, 0)),
                  pl.BlockSpec((None, k, TN), lambda i, j: (layer, 0, j)),
                  pl.BlockSpec((None, k, TN), lambda i, j: (layer, 0, j + nj))],
        out_specs=pl.BlockSpec((TM, TN), lambda i, j: (i, j)),
        out_shape=jax.ShapeDtypeStruct((s, D_FF), BF16),
        scratch_shapes=[pltpu.VMEM((TM, k), BF16)],
        compiler_params=_cparams(("parallel", "arbitrary")),
        name="norm_ffn_up",
    )(x, g, w_up, w_up)


def _mm_res_kernel(x_ref, w_ref, r_ref, o_ref):
    o_ref[...] = r_ref[...] + _dot(x_ref[...], w_ref[...].astype(BF16))


def _matmul_residual(x, w, wl, res, tn):
    s, k = x.shape
    n = w.shape[2]
    return pl.pallas_call(
        _mm_res_kernel,
        grid=(s // TM, n // tn),
        in_specs=[pl.BlockSpec((TM, k), lambda i, j: (i, 0)),
                  pl.BlockSpec((None, k, tn), lambda i, j: (wl, 0, j)),
                  pl.BlockSpec((TM, tn), lambda i, j: (i, j))],
        out_specs=pl.BlockSpec((TM, tn), lambda i, j: (i, j)),
        out_shape=jax.ShapeDtypeStruct((s, n), F32),
        compiler_params=_cparams(("parallel", "parallel")),
        name="proj_residual",
    )(x, w, res)


def _mm2_res_kernel(x1_ref, x2_ref, w1_ref, w2_ref, r_ref, o_ref):
    o_ref[...] = r_ref[...] + (_dot(x1_ref[...], w1_ref[...]) + _dot(x2_ref[...], w2_ref[...]))


def _matmul2_residual(x1, x2, w, wl, res):
    s, k = x1.shape
    n = w.shape[2]
    assert x2.shape == (s, k) and w.shape[1] == 2 * k
    tm = min(2 * TM, s)
    return pl.pallas_call(
        _mm2_res_kernel,
        grid=(s // tm, n // TN),
        in_specs=[pl.BlockSpec((tm, k), lambda i, j: (i, 0)),
                  pl.BlockSpec((tm, k), lambda i, j: (i, 0)),
                  pl.BlockSpec((None, k, TN), lambda i, j: (wl, 0, j)),
                  pl.BlockSpec((None, k, TN), lambda i, j: (wl, 1, j)),
                  pl.BlockSpec((tm, TN), lambda i, j: (i, j))],
        out_specs=pl.BlockSpec((tm, TN), lambda i, j: (i, j)),
        out_shape=jax.ShapeDtypeStruct((s, n), F32),
        compiler_params=_cparams(("parallel", "parallel")),
        name="out_proj_residual",
    )(x1, x2, w, w, res)


def _rmsnorm_kernel(x_ref, g_ref, o_ref):
    o_ref[...] = _rms_rows(x_ref[...], g_ref[...])


def _rmsnorm(x, g):
    s, k = x.shape
    return pl.pallas_call(
        _rmsnorm_kernel,
        grid=(s // TM,),
        in_specs=[pl.BlockSpec((TM, k), lambda i: (i, 0)), pl.BlockSpec((1, k), lambda i: (0, 0))],
        out_specs=pl.BlockSpec((TM, k), lambda i: (i, 0)),
        out_shape=jax.ShapeDtypeStruct((s, k), F32),
        compiler_params=_cparams(("parallel",)),
        name="final_norm",
    )(x, g.reshape(1, k))


def _hgrn2_kernel(q_ref, f_ref, v_ref, g_ref, lbt_ref, nw_ref, o_ref, st_ref, *, layer, nchunk):
    @pl.when(pl.program_id(1) == 0)
    def _():
        st_ref[...] = jnp.zeros_like(st_ref)

    t = lbt_ref[...]
    e = jnp.exp(t - jnp.max(t, axis=0, keepdims=True))
    sm = e / jnp.sum(e, axis=0, keepdims=True)
    lb = jnp.zeros((1, HG_D), F32)
    for i in range(1, layer + 1):
        lb = lb + sm[i:i + 1, :]
    log_lb = jnp.log(lb)
    log_1m = jnp.log1p(-lb)
    nw = nw_ref[...]

    tril = _tril(CHUNK).astype(BF16)
    rowid = _iota((SUB, HG_D), 0)
    nsub = CHUNK // SUB
    cs = range(nchunk)
    rows = [slice(c * CHUNK, (c + 1) * CHUNK) for c in cs]

    fp = [f_ref[r, :] for r in rows]
    v = [v_ref[r, :] for r in rows]
    b2 = [log_1m + _log_sigmoid(x) for x in fp]
    log_f = [jnp.maximum(log_lb, x) + jnp.log1p(jnp.exp(-jnp.abs(log_lb - x))) for x in b2]
    k = [(1.0 - lb) * _sigmoid(-x) for x in fp]
    q = [x * _sigmoid(x) for x in (q_ref[r, :] for r in rows)]
    bc = [_cumsum_rows(tril, x) for x in log_f]
    b_last = [x[CHUNK - 1:CHUNK, :] for x in bc]
    kv = [_mm(v[c], k[c] * jnp.exp(b_last[c] - bc[c]), "bf16", "tn") for c in cs]
    st = st_ref[...]
    starts = []
    for c in cs:
        starts.append(st)
        st = jnp.exp(b_last[c]) * st + kv[c]
    st_ref[...] = st
    acc = [_mm(q[c] * jnp.exp(bc[c]), starts[c], "bf16", "nt") for c in cs]
    acc = [[a[blk * SUB:(blk + 1) * SUB] for blk in range(nsub)] for a in acc]
    cb = [(c, blk) for blk in range(1, nsub) for c in cs]
    ref_b = {(c, blk): bc[c][blk * SUB - 1:blk * SUB, :] for c, blk in cb}
    qx = {(c, blk): q[c][blk * SUB:(blk + 1) * SUB] * jnp.exp(bc[c][blk * SUB:(blk + 1) * SUB] - ref_b[c, blk])
          for c, blk in cb}
    kx = {(c, blk): k[c][0:blk * SUB] * jnp.exp(ref_b[c, blk] - bc[c][0:blk * SUB]) for c, blk in cb}
    sc = {e: _mm(qx[e], kx[e], "bf16", "nt") for e in cb}
    od = {(c, blk): _mm(sc[c, blk], v[c][0:blk * SUB], "bf16") for c, blk in cb}
    for c, blk in cb:
        acc[c][blk] = acc[c][blk] + od[c, blk]
    for s in range(SUB):
        for c in cs:
            for blk in range(nsub):
                lo = blk * SUB
                b_i = bc[c][lo:lo + SUB]
                d = b_i - bc[c][lo + s:lo + s + 1, :]
                if s > 0:
                    d = jnp.where(rowid >= s, d, -jnp.inf)
                col = jnp.sum(q[c][lo:lo + SUB] * k[c][lo + s:lo + s + 1, :] * jnp.exp(d), axis=1, keepdims=True)
                acc[c][blk] = acc[c][blk] + col * v[c][lo + s:lo + s + 1, :]
    for c in cs:
        o = jnp.concatenate(acc[c], axis=0)
        y = o * lax.rsqrt(jnp.mean(o * o, axis=-1, keepdims=True) + EPS) * nw
        o_ref[rows[c], :] = (y * _sigmoid(g_ref[rows[c], :])).astype(BF16)


def _hgrn2(p, lb_table, norm_w, layer, rb):
    s = p.shape[0]
    nb = HG_HEADS
    col = lambda off: pl.BlockSpec((rb, HG_D), lambda h, r, off=off: (r, off + h))
    return pl.pallas_call(
        functools.partial(_hgrn2_kernel, layer=layer, nchunk=rb // CHUNK),
        grid=(HG_HEADS, s // rb),
        in_specs=[col(0), col(nb), col(2 * nb), col(3 * nb),
                  pl.BlockSpec((DEPTH, HG_D), lambda h, r: (0, h)),
                  pl.BlockSpec((None, 1, HG_D), lambda h, r: (layer // 2, 0, h))],
        out_specs=pl.BlockSpec((rb, HG_D), lambda h, r: (r, h)),
        out_shape=jax.ShapeDtypeStruct((s, HG_HEADS * HG_D), BF16),
        scratch_shapes=[pltpu.VMEM((HG_D, HG_D), F32)],
        compiler_params=_cparams(("parallel", "arbitrary")),
        name="hgrn2",
    )(p, p, p, p, lb_table, norm_w)


def _shift_lerp(x, prev_row, mu):
    rolled = pltpu.roll(x, 1, 0)
    shifted = jnp.where(_iota(x.shape, 0) == 0, prev_row, rolled)
    return x + (shifted - x) * mu


def _split_bf16(x):
    hi = x.astype(BF16)
    lo = (x - hi.astype(F32)).astype(BF16)
    return hi, lo


def _mm(a, b, mode, kind="nn"):
    dot = {"nn": _dot, "nt": _dot_nt, "tn": _dot_tn}[kind]
    if mode == "hi":
        return dot(a, b, HI)
    if mode == "bf16":
        return dot(a.astype(BF16), b.astype(BF16))
    ah, al = _split_bf16(a)
    bh, bl = _split_bf16(b)
    return dot(ah, bh) + (dot(ah, bl) + dot(al, bh))


def _cumsum_rows(tril_bf16, x):
    hi, lo = _split_bf16(x)
    lo2 = (x - hi.astype(F32) - lo.astype(F32)).astype(BF16)
    return _dot(tril_bf16, hi) + (_dot(tril_bf16, lo) + _dot(tril_bf16, lo2))


def _neumann_inverse(mats, mode):
    n = mats[0].shape[0]
    eye = (_iota((n, n), 0) == _iota((n, n), 1)).astype(F32)
    p = [_mm(x, x, mode) for x in mats]
    m = [eye + x for x in mats]
    for _ in range(4):
        both = [_mm(x, jnp.concatenate([x, y], axis=1), mode) for x, y in zip(p, m)]
        m = [y + b[:, n:] for y, b in zip(m, both)]
        p = [b[:, :n] for b in both]
    return [y + _mm(x, y, mode) for x, y in zip(p, m)]


RW_PREC = dict(score="bf16", neumann="bf16", apply="bf16", state="bf16")


def _rwkv7_kernel(r_ref, k_ref, v_ref, tail_ref,
                  mu_r_ref, mu_k_ref, mu_v_ref, mu_tail_ref,
                  w0_ref, a0_ref, w2_ref, a2_ref, g2_ref, kk_ref, ka_ref, rk_ref, lnw_ref, lnb_ref,
                  o_ref, st_ref, prev_ref, *, nchunk, npair):
    rb = r_ref.shape[0]
    pc = RW_PREC
    wide = npair * LANES

    @pl.when(pl.program_id(1) == 0)
    def _():
        st_ref[...] = jnp.zeros_like(st_ref)
        prev_ref[...] = jnp.zeros_like(prev_ref)

    lane = _iota((1, LANES), 1)
    head0 = lane < RW_N

    def head_sum(x):
        s0 = jnp.sum(jnp.where(head0, x, 0.0), axis=1, keepdims=True)
        s1 = jnp.sum(jnp.where(head0, 0.0, x), axis=1, keepdims=True)
        return jnp.where(head0, s0, s1)

    tw = tail_ref.shape[1]
    tail_raw = jnp.where(_iota((1, tw), 1) < RW_TAIL_COLS, tail_ref[...], 0.0)
    tail = _shift_lerp(tail_raw, prev_ref[3:4, 0:tw], mu_tail_ref[...])
    prev_ref[3:4, 0:tw] = tail_raw[rb - 1:rb]
    wa_t = jnp.tanh(tail[:, 0:LANES]).astype(BF16)
    wa_b = tail[:, 0:LANES].astype(BF16)
    gl_s = _sigmoid(tail[:, LANES:3 * LANES]).astype(BF16)
    ps = range(npair)

    def tile_inputs(p):
        ts = slice(p * LANES, (p + 1) * LANES)
        r_raw, k_raw, v_raw = r_ref[:, ts], k_ref[:, ts], v_ref[:, ts]
        r_p = _shift_lerp(r_raw, prev_ref[0:1, ts], mu_r_ref[:, ts])
        k_p = _shift_lerp(k_raw, prev_ref[1:2, ts], mu_k_ref[:, ts])
        v_p = _shift_lerp(v_raw, prev_ref[2:3, ts], mu_v_ref[:, ts])
        prev_ref[0:1, ts] = r_raw[rb - 1:rb]
        prev_ref[1:2, ts] = k_raw[rb - 1:rb]
        prev_ref[2:3, ts] = v_raw[rb - 1:rb]
        w_log = -_softplus(-(w0_ref[:, ts] + _dot(wa_t, w2_ref[:, ts].astype(BF16)))) - 0.5
        lw_p = -jnp.exp(w_log)
        a_p = _sigmoid(a0_ref[:, ts] + _dot(wa_b, a2_ref[:, ts].astype(BF16)))
        gate_p = _dot(gl_s, g2_ref[:, ts].astype(BF16))
        kk_p = k_p * kk_ref[:, ts]
        kk_p = kk_p / jnp.maximum(jnp.sqrt(head_sum(kk_p * kk_p)), 1e-12)
        k_p = k_p * (1.0 + (a_p - 1.0) * ka_ref[:, ts])
        bonus_p = head_sum(r_p * k_p * rk_ref[:, ts]) * v_p
        return r_p, k_p, v_p, a_p, lw_p, kk_p, bonus_p, gate_p

    head1 = jnp.logical_not(head0)
    bd = (_iota((LANES, LANES), 0) < RW_N) == (_iota((LANES, LANES), 1) < RW_N)
    gs = 2 * CHUNK
    ri, ci = _iota((gs, gs), 0), _iota((gs, gs), 1)
    same = (ri // CHUNK) == (ci // CHUNK)
    strict = jnp.logical_and(same, ri > ci)
    incl = jnp.logical_and(same, ri >= ci)
    tril = _tril(CHUNK).astype(BF16)
    head00 = jnp.concatenate([head0, head0], axis=1)

    def stack_heads(x, masked):
        return jnp.concatenate([jnp.where(head0, x, 0.0), jnp.where(head1, x, 0.0)] if masked else [x, x], axis=0)

    def unstack_heads(x):
        return jnp.where(head0 if x.shape[1] == LANES else head00, x[:CHUNK], x[CHUNK:])

    cs = range(nchunk)
    grp = [(p, c) for c in cs for p in ps]
    rows = [slice(c * CHUNK, (c + 1) * CHUNK) for c in cs]
    v, g, rt, at, bt, kt, a_ab, rab, aakv, rakv, bonus, gate = ({} for _ in range(12))
    for p in ps:
        r_p, k_p, v_p, a_p, lw_p, kk_p, bonus[p], gate[p] = tile_inputs(p)
        mine = [(p, c) for c in cs]
        cut = lambda x: {(p, c): x[rows[c]] for c in cs}
        r_c, k_c, a_c, lw_c, kk_c = cut(r_p), cut(k_p), cut(a_p), cut(lw_p), cut(kk_p)
        v.update(cut(v_p))
        g.update({e: _cumsum_rows(tril, lw_c[e]) for e in mine})
        ieg = {e: jnp.exp(-g[e]) for e in mine}
        rt.update({e: r_c[e] * jnp.exp(g[e]) for e in mine})
        at.update({e: -kk_c[e] * jnp.exp(g[e] - lw_c[e]) for e in mine})
        bt.update({e: kk_c[e] * a_c[e] * ieg[e] for e in mine})
        kt.update({e: k_c[e] * ieg[e] for e in mine})
        l_a = {e: stack_heads(at[e], True) for e in mine}
        l_r = {e: stack_heads(rt[e], True) for e in mine}
        r_bk = {e: jnp.concatenate([stack_heads(bt[e], True), stack_heads(kt[e], True)], axis=0) for e in mine}
        s_a = {e: _mm(l_a[e], r_bk[e], pc["score"], "nt") for e in mine}
        s_r = {e: _mm(l_r[e], r_bk[e], pc["score"], "nt") for e in mine}
        a_ab.update({e: jnp.where(strict, s_a[e][:, :gs], 0.0) for e in mine})
        rab.update({e: jnp.where(incl, s_r[e][:, :gs], 0.0) for e in mine})
        v_st = {e: stack_heads(v[e], False) for e in mine}
        aakv.update({e: _mm(jnp.where(strict, s_a[e][:, gs:], 0.0), v_st[e], pc["apply"]) for e in mine})
        rakv.update({e: unstack_heads(_mm(jnp.where(incl, s_r[e][:, gs:], 0.0), v_st[e], pc["apply"])) for e in mine})
    t_inv = dict(zip(grp, _neumann_inverse([a_ab[e] for e in grp], pc["neumann"])))
    xs = {e: unstack_heads(_mm(t_inv[e], jnp.concatenate([stack_heads(at[e], False), aakv[e]], axis=1), pc["apply"]))
          for e in grp}
    ta = {e: xs[e][:, :LANES] for e in grp}
    tav = {e: xs[e][:, LANES:] for e in grp}
    egl = {e: jnp.exp(g[e][CHUNK - 1:CHUNK, :]) for e in grp}
    bte = {e: bt[e] * egl[e] for e in grp}
    m_mat = {e: jnp.where(bd, _mm(ta[e], bte[e], pc["state"], "tn"), 0.0) for e in grp}
    c_mat = {e: jnp.where(bd, _mm(jnp.concatenate([tav[e], v[e]], axis=0),
                                  jnp.concatenate([bte[e], kt[e] * egl[e]], axis=0), pc["state"], "tn"), 0.0)
             for e in grp}

    st = [st_ref[p] for p in ps]
    start = {}
    for c in cs:
        for p in ps:
            start[p, c] = st[p]
            st[p] = st[p] * egl[p, c] + _mm(st[p], m_mat[p, c], pc["state"]) + c_mat[p, c]
    for p in ps:
        st_ref[p] = st[p]
    u = {e: _mm(ta[e], start[e], pc["state"], "nt") + tav[e] for e in grp}
    o_in = {e: _mm(rt[e], start[e], pc["state"], "nt") for e in grp}
    y = {e: _mm(rab[e], stack_heads(u[e], True), pc["apply"]) for e in grp}
    o = {e: o_in[e] + (y[e][:CHUNK] + y[e][CHUNK:]) + rakv[e] for e in grp}

    for p in ps:
        ts = slice(p * LANES, (p + 1) * LANES)
        o_p = jnp.concatenate([o[p, c] for c in cs], axis=0)
        mu = head_sum(o_p) * (1.0 / RW_N)
        d = o_p - mu
        var = head_sum(d * d) * (1.0 / RW_N)
        y_p = d * lax.rsqrt(var + RW_GN_EPS) * lnw_ref[:, ts] + lnb_ref[:, ts]
        o_ref[:, ts] = ((y_p + bonus[p]) * gate[p]).astype(BF16)


def _rwkv7_params(mu, w0, w2, a0, a2, g2, k_k, k_a, r_k, ln_w, ln_b):
    n = mu.shape[0]
    row = lambda x: x.reshape(n, 1, -1)
    assert 2 * w2.shape[1] == LANES and 2 * a2.shape[1] == LANES
    mu_p = jnp.pad(mu, ((0, 0), (0, 3 * RW_W + RW_TAIL_W - mu.shape[1])))
    return dict(mu=row(mu_p), w0=row(w0), a0=row(a0),
                w2=jnp.concatenate([w2, jnp.zeros_like(w2)], axis=1),
                a2=jnp.concatenate([jnp.zeros_like(a2), a2], axis=1),
                g2=jnp.pad(g2, ((0, 0), (0, 2 * LANES - g2.shape[1]), (0, 0))),
                k_k=row(k_k), k_a=row(k_a), r_k=row(r_k), ln_w=row(ln_w), ln_b=row(ln_b))


RW_PAIRS = 4


def _rwkv7(p, prm, j, rb):
    s = p.shape[0]
    wide = RW_PAIRS * LANES
    base = 4 * HG_HEADS * LANES // wide
    nb = RW_W // wide
    col = lambda off: pl.BlockSpec((rb, wide), lambda h, r, off=off: (r, off + h))
    par = pl.BlockSpec((None, 1, wide), lambda h, r: (j, 0, h))
    par_off = lambda off: pl.BlockSpec((None, 1, wide), lambda h, r, off=off: (j, 0, off + h))
    lora = lambda rows: pl.BlockSpec((None, rows, wide), lambda h, r: (j, 0, h))
    assert RW_TAIL % RW_TAIL_W == 0 and (3 * RW_W) % RW_TAIL_W == 0
    return pl.pallas_call(
        functools.partial(_rwkv7_kernel, nchunk=rb // CHUNK, npair=RW_PAIRS),
        grid=(nb, s // rb),
        in_specs=[col(base), col(base + nb), col(base + 2 * nb),
                  pl.BlockSpec((rb, RW_TAIL_W), lambda h, r: (r, RW_TAIL // RW_TAIL_W)),
                  par, par_off(nb), par_off(2 * nb),
                  pl.BlockSpec((None, 1, RW_TAIL_W), lambda h, r: (j, 0, 3 * RW_W // RW_TAIL_W)),
                  par, par, lora(LANES), lora(LANES), lora(2 * LANES),
                  par, par, par, par, par],
        out_specs=pl.BlockSpec((rb, wide), lambda h, r: (r, h)),
        out_shape=jax.ShapeDtypeStruct((s, RW_W), BF16),
        scratch_shapes=[pltpu.VMEM((RW_PAIRS, LANES, LANES), F32), pltpu.VMEM((8, max(wide, RW_TAIL_W)), F32)],
        compiler_params=_cparams(("parallel", "arbitrary")),
        name="rwkv7",
    )(p, p, p, p, prm["mu"], prm["mu"], prm["mu"], prm["mu"],
      prm["w0"], prm["a0"], prm["w2"], prm["a2"], prm["g2"],
      prm["k_k"], prm["k_a"], prm["r_k"], prm["ln_w"], prm["ln_b"])


def _lane_col(x, idx):
    return jnp.sum(jnp.where(_iota(x.shape, 1) == idx, x, 0.0), axis=1, keepdims=True)


def _transpose_rows(x):
    eye = (_iota((LANES, LANES), 0) == _iota((LANES, LANES), 1)).astype(BF16)
    hi, lo = _split_bf16(x)
    lo2 = (x - hi.astype(F32) - lo.astype(F32)).astype(BF16)
    return _dot_nt(eye, hi) + (_dot_nt(eye, lo) + _dot_nt(eye, lo2))


def _mlstm_kernel(q_ref, k_ref, v_ref, og_ref, sm_ref, bias_ref, nw_ref, o_ref, c_ref, n_ref, m_ref, *, nchunk):
    @pl.when(pl.program_id(0) == 0)
    def _():
        c_ref[...] = jnp.zeros_like(c_ref)
        n_ref[...] = jnp.zeros_like(n_ref)
        m_ref[...] = jnp.zeros_like(m_ref)

    incl = _tril(CHUNK)
    tril = incl.astype(BF16)
    lane = _iota((CHUNK, LANES), 1)
    is_f = jnp.logical_and(lane >= ML_HEADS, lane < 2 * ML_HEADS)
    scale = ML_DQK ** -0.5
    cs = range(nchunk)
    hs = range(ML_HEADS)
    rows = [slice(c * CHUNK, (c + 1) * CHUNK) for c in cs]
    ch = [(c, h) for c in cs for h in hs]

    pre = [sm_ref[r, :] + bias_ref[...] for r in rows]
    cap = [ML_CAP * jnp.tanh(x / ML_CAP) for x in pre]
    x = [jnp.where(is_f, _log_sigmoid(y), y) for y in cap]
    cum = [_cumsum_rows(tril, y) for y in x]
    x_t = [_transpose_rows(y) for y in x]
    cum_t = [_transpose_rows(y) for y in cum]

    q = {(c, h): q_ref[rows[c], h * ML_DQK:(h + 1) * ML_DQK] * scale for c, h in ch}
    k = {(c, h): k_ref[rows[c], h * ML_DQK:(h + 1) * ML_DQK] for c, h in ch}
    v = {(c, h): v_ref[rows[c], h * ML_DV:(h + 1) * ML_DV] for c, h in ch}
    b_col = {(c, h): _lane_col(cum[c], ML_HEADS + h) for c, h in ch}
    i_col = {(c, h): _lane_col(x[c], h) for c, h in ch}
    b_row = {(c, h): cum_t[c][ML_HEADS + h:ML_HEADS + h + 1, :] for c, h in ch}
    i_row = {(c, h): x_t[c][h:h + 1, :] for c, h in ch}
    qk = {e: _mm(q[e], k[e], "bf16", "nt") for e in ch}
    dmat = {e: jnp.where(incl, b_col[e] - b_row[e] + i_row[e], -jnp.inf) for e in ch}
    dmax = {e: jnp.max(dmat[e], axis=1, keepdims=True) for e in ch}
    b_last = {e: b_row[e][:, CHUNK - 1:CHUNK] for e in ch}
    src_row = {e: b_last[e] - b_row[e] + i_row[e] for e in ch}
    src_col = {e: b_last[e] - b_col[e] + i_col[e] for e in ch}
    src_max = {e: jnp.max(src_row[e], axis=1, keepdims=True) for e in ch}

    m_prev, m_new = {}, {}
    for h in hs:
        m = m_ref[h:h + 1, 0:1]
        for c in cs:
            m_prev[c, h] = m
            m = jnp.maximum(b_last[c, h] + m, src_max[c, h])
            m_new[c, h] = m
        m_ref[h:h + 1, :] = jnp.broadcast_to(m, (1, LANES))
    inter = {e: b_col[e] + m_prev[e] for e in ch}
    m_t = {e: jnp.maximum(inter[e], dmax[e]) for e in ch}
    w_inter = {e: jnp.exp(inter[e] - m_t[e]) for e in ch}
    pmat = {e: jnp.exp(dmat[e] - m_t[e]) * qk[e] for e in ch}
    num = {e: _mm(pmat[e], v[e], "bf16") for e in ch}
    den = {e: jnp.sum(pmat[e], axis=1, keepdims=True) for e in ch}
    decay = {e: jnp.exp(b_last[e] + m_prev[e] - m_new[e]) for e in ch}
    wk = {e: jnp.exp(src_col[e] - m_new[e]) * k[e] for e in ch}
    kv = {e: _mm(wk[e], v[e], "bf16", "tn") for e in ch}
    ksum = {e: jnp.sum(wk[e], axis=0, keepdims=True) for e in ch}

    c_start, n_start = {}, {}
    for h in hs:
        c_mat, n_row = c_ref[h], n_ref[h:h + 1, :]
        for c in cs:
            c_start[c, h], n_start[c, h] = c_mat, n_row
            c_mat = decay[c, h] * c_mat + kv[c, h]
            n_row = decay[c, h] * n_row + ksum[c, h]
        c_ref[h] = c_mat
        n_ref[h:h + 1, :] = n_row

    qc = {e: _mm(q[e], c_start[e], "bf16") for e in ch}
    vs = {h: slice(h * ML_DV, (h + 1) * ML_DV) for h in hs}
    qn = {e: jnp.sum(q[e] * n_start[e], axis=1, keepdims=True) for e in ch}
    num = {e: num[e] + w_inter[e] * qc[e] for e in ch}
    den = {e: den[e] + w_inter[e] * qn[e] for e in ch}
    h_out = {e: num[e] / jnp.maximum(jnp.abs(den[e]), jnp.exp(-m_t[e])) for e in ch}
    ms = {e: jnp.mean(h_out[e] * h_out[e], axis=-1, keepdims=True) for e in ch}
    gate = {(c, h): _sigmoid(og_ref[rows[c], vs[h]]) for c, h in ch}
    y = {(c, h): h_out[c, h] * lax.rsqrt(ms[c, h] + EPS) * nw_ref[:, vs[h]] * gate[c, h] for c, h in ch}
    for c, h in ch:
        o_ref[rows[c], vs[h]] = y[c, h].astype(BF16)


def _mlstm(p, bias_row, norm_w, j, rb):
    s = p.shape[0]
    nq = ML_HEADS * ML_DQK
    nv = ML_HEADS * ML_DV
    return pl.pallas_call(
        functools.partial(_mlstm_kernel, nchunk=rb // CHUNK),
        grid=(s // rb,),
        in_specs=[pl.BlockSpec((rb, nq), lambda r: (r, 0)),
                  pl.BlockSpec((rb, nq), lambda r: (r, 1)),
                  pl.BlockSpec((rb, nv), lambda r: (r, 1)),
                  pl.BlockSpec((rb, nv), lambda r: (r, 2)),
                  pl.BlockSpec((rb, LANES), lambda r: (r, OD_SMALL // LANES)),
                  pl.BlockSpec((None, 1, LANES), lambda r: (j, 0, 0)),
                  pl.BlockSpec((None, 1, nv), lambda r: (j, 0, 0))],
        out_specs=pl.BlockSpec((rb, nv), lambda r: (r, 0)),
        out_shape=jax.ShapeDtypeStruct((s, nv), BF16),
        scratch_shapes=[pltpu.VMEM((ML_HEADS, ML_DQK, ML_DV), F32),
                        pltpu.VMEM((8, ML_DQK), F32), pltpu.VMEM((8, LANES), F32)],
        compiler_params=_cparams(("arbitrary",)),
        name="mlstm",
    )(p, p, p, p, p, bias_row, norm_w)


def _ssd_kernel(z_ref, xbc_ref, sm_ref, cw_ref, cb_ref, bias_ref, negA_ref, dvec_ref, nw_ref, o_ref,
                st_ref, xin_s, xc_s, *, nchunk):
    rb = z_ref.shape[0]
    pad = 8

    @pl.when(pl.program_id(0) == 0)
    def _():
        st_ref[...] = jnp.zeros_like(st_ref)
        xin_s[rb:rb + pad, :] = jnp.zeros((pad, xin_s.shape[1]), F32)

    xin_s[0:pad, :] = xin_s[rb:rb + pad, :]
    xin_s[pad:rb + pad, :] = xbc_ref[...]
    xin = xin_s[...]
    acc = cb_ref[...] + xin[pad:, :] * cw_ref[MB_CONV - 1:MB_CONV, :]
    for j in range(MB_CONV - 1):
        acc = acc + pltpu.roll(xin, MB_CONV - 1 - j, 0)[pad:, :] * cw_ref[j:j + 1, :]
    xc_s[...] = acc * _sigmoid(acc)

    incl = _tril(CHUNK)
    tril = incl.astype(BF16)
    lane = _iota((1, LANES), 1)
    head0 = lane < MB_P
    sub0 = _iota((LANES, 1), 0) < MB_P
    hpg = MB_HEADS // MB_GROUPS
    gw = MB_DI // MB_GROUPS
    dt_lane0 = 2 * ML_HEADS
    npair = hpg // 2
    cs = range(nchunk)
    rows = [slice(c * CHUNK, (c + 1) * CHUNK) for c in cs]
    cg = [(c, g) for c in cs for g in range(MB_GROUPS)]
    cgj = [(c, g, j) for c, g in cg for j in range(npair)]

    dt = [_softplus(sm_ref[r, :] + bias_ref[...]) for r in rows]
    a_cum = [_cumsum_rows(tril, negA_ref[...] * x) for x in dt]
    a_t = [_transpose_rows(x) for x in a_cum]
    dt_t = [_transpose_rows(x) for x in dt]
    bm = {(c, g): xc_s[rows[c], MB_DI + g * MB_N:MB_DI + (g + 1) * MB_N] for c, g in cg}
    cm = {(c, g): xc_s[rows[c], MB_DI + (MB_GROUPS + g) * MB_N:MB_DI + (MB_GROUPS + g + 1) * MB_N] for c, g in cg}
    cb = {e: _mm(cm[e], bm[e], "bf16", "nt") for e in cg}
    xs = {(g, j): slice(g * gw + j * LANES, g * gw + (j + 1) * LANES) for g in range(MB_GROUPS) for j in range(npair)}
    x2 = {(c, g, j): xc_s[rows[c], xs[g, j]] for c, g, j in cgj}

    sc, ea, ws, el = {}, {}, {}, {}
    for c, g, j in cgj:
        for e in range(2):
            ln = dt_lane0 + g * hpg + 2 * j + e
            a_col = _lane_col(a_cum[c], ln)
            a_row = a_t[c][ln:ln + 1, :]
            a_last = a_row[:, CHUNK - 1:CHUNK]
            seg = jnp.exp(jnp.where(incl, a_col - a_row, -jnp.inf))
            sc[c, g, j, e] = seg * cb[c, g] * dt_t[c][ln:ln + 1, :]
            ea[c, g, j, e] = jnp.exp(a_col)
            ws[c, g, j, e] = jnp.exp(a_last - a_col) * _lane_col(dt[c], ln)
            el[c, g, j, e] = jnp.exp(a_last)
    y0 = {e: _mm(sc[e + (0,)], x2[e], "bf16") for e in cgj}
    y1 = {e: _mm(sc[e + (1,)], x2[e], "bf16") for e in cgj}
    kv = {(c, g, j): _mm(x2[c, g, j] * jnp.where(head0, ws[c, g, j, 0], ws[c, g, j, 1]), bm[c, g], "bf16", "tn")
          for c, g, j in cgj}

    start = {}
    for g in range(MB_GROUPS):
        for j in range(npair):
            si = g * npair + j
            st = st_ref[si]
            for c in cs:
                start[c, g, j] = st
                st = jnp.where(sub0, el[c, g, j, 0], el[c, g, j, 1]) * st + kv[c, g, j]
            st_ref[si] = st

    ycs = {(c, g, j): _mm(cm[c, g], start[c, g, j], "bf16", "nt") for c, g, j in cgj}
    y = {e: jnp.where(head0, y0[e], y1[e]) + jnp.where(head0, ea[e + (0,)], ea[e + (1,)]) * ycs[e]
         + x2[e] * dvec_ref[:, xs[e[1], e[2]]] for e in cgj}
    gs = {g: slice(g * gw, (g + 1) * gw) for g in range(MB_GROUPS)}
    zg = {(c, g): z_ref[rows[c], gs[g]] for c, g in cg}
    yg = {(c, g): jnp.concatenate([y[c, g, j] for j in range(npair)], axis=1) * (zg[c, g] * _sigmoid(zg[c, g]))
          for c, g in cg}
    ms = {e: jnp.mean(yg[e] * yg[e], axis=-1, keepdims=True) for e in cg}
    out = {(c, g): yg[c, g] * lax.rsqrt(ms[c, g] + EPS) * nw_ref[:, gs[g]] for c, g in cg}
    for c, g in cg:
        o_ref[rows[c], gs[g]] = out[c, g].astype(BF16)


def _ssd(p, conv_w, conv_b, bias_row, negA_row, d_row, norm_w, j, rb):
    s = p.shape[0]
    nx = conv_w.shape[2]
    prow = lambda c: pl.BlockSpec((None, 1, c), lambda r: (j, 0, 0))
    return pl.pallas_call(
        functools.partial(_ssd_kernel, nchunk=rb // CHUNK),
        grid=(s // rb,),
        in_specs=[pl.BlockSpec((rb, MB_DI), lambda r: (r, 3)),
                  pl.BlockSpec((rb, nx), lambda r: (r, OD_XBC // nx)),
                  pl.BlockSpec((rb, LANES), lambda r: (r, OD_SMALL // LANES)),
                  pl.BlockSpec((None, MB_CONV, nx), lambda r: (j, 0, 0)),
                  prow(nx), prow(LANES), prow(LANES), prow(MB_DI), prow(MB_DI)],
        out_specs=pl.BlockSpec((rb, MB_DI), lambda r: (r, 0)),
        out_shape=jax.ShapeDtypeStruct((s, MB_DI), BF16),
        scratch_shapes=[pltpu.VMEM((MB_HEADS // 2, 2 * MB_P, MB_N), F32),
                        pltpu.VMEM((rb + 8, nx), F32), pltpu.VMEM((rb, nx), F32)],
        compiler_params=_cparams(("arbitrary",)),
        name="ssd",
    )(p, p, p, conv_w, conv_b, bias_row, negA_row, d_row, norm_w)


def _rows3(x):
    return x.reshape(x.shape[0], 1, -1)


def _odd_weight_kernel(w_ref, o_ref):
    w = w_ref[...]
    head_w = 2048 + 1024 + 1024 + 8 + 16
    o_ref[...] = jnp.concatenate([w[:, 0:2048], w[:, 2056:3080], w[:, 3080:4104], w[:, 2048:2056], w[:, 5640:5656],
                                  jnp.zeros((w.shape[0], OD_XBC - head_w), w.dtype), w[:, 4104:5640]],
                                 axis=1).astype(BF16)


def _odd_in_weight(w):
    n, k, cols = w.shape
    tk = 256
    return pl.pallas_call(
        _odd_weight_kernel,
        grid=(n, k // tk),
        in_specs=[pl.BlockSpec((None, tk, cols), lambda l, i: (l, i, 0))],
        out_specs=pl.BlockSpec((None, tk, NP_ODD), lambda l, i: (l, i, 0)),
        out_shape=jax.ShapeDtypeStruct((n, k, NP_ODD), BF16),
        compiler_params=_cparams(("parallel", "parallel")),
        name="odd_weight_layout",
    )(w)


def _round_bf16_kernel(w_ref, o_ref):
    o_ref[...] = w_ref[...].astype(BF16)


def _round_bf16(w):
    n, k, cols = w.shape
    tk = 256
    spec = pl.BlockSpec((None, tk, cols), lambda l, i: (l, i, 0))
    return pl.pallas_call(
        _round_bf16_kernel,
        grid=(n, k // tk),
        in_specs=[spec],
        out_specs=spec,
        out_shape=jax.ShapeDtypeStruct(w.shape, BF16),
        compiler_params=_cparams(("parallel", "parallel")),
        name="weight_round",
    )(w)


def _small_rows(*xs):
    row = jnp.concatenate(xs, axis=1)
    return _rows3(jnp.pad(row, ((0, 0), (0, LANES - row.shape[1]))))


def kernel(x, norm_mix, norm_ffn, norm_final, w_in_even, w_out_even, hg_lb_table, hg_norm, rw_mu, rw_w0, rw_w2, rw_a0, rw_a2, rw_g2, rw_k_k, rw_k_a, rw_r_k, rw_ln_w, rw_ln_b, w_in_odd, w_out_odd, ml_i_bias, ml_f_bias, ml_norm, mb_conv_w, mb_conv_b, mb_dt_bias, mb_A_log, mb_D, mb_norm, ffn_w_up, ffn_w_down):
    b, s, d = x.shape
    assert b == 1 and d == D_MODEL and s % TM == 0
    rb_hg, rb_rw, rb_ml, rb_ssd = min(s, 256), min(s, 512), min(s, 256), min(s, 256)

    g_mix, g_ffn = _rows3(norm_mix), _rows3(norm_ffn)
    w_in_e, w_in_o = _round_bf16(w_in_even), _odd_in_weight(w_in_odd)
    w_out_e, w_out_o = w_out_even.astype(BF16), w_out_odd.astype(BF16)
    w_down = ffn_w_down.astype(BF16)
    rw = _rwkv7_params(rw_mu, rw_w0, rw_w2, rw_a0, rw_a2, rw_g2, rw_k_k, rw_k_a, rw_r_k, rw_ln_w, rw_ln_b)
    zeros8 = jnp.zeros((ml_i_bias.shape[0], 2 * ML_HEADS), F32)
    ml_bias = _small_rows(ml_i_bias, ml_f_bias)
    dt_bias = _small_rows(zeros8, mb_dt_bias)
    neg_a = _small_rows(zeros8, -jnp.exp(mb_A_log.astype(F32)))
    d_rows = _rows3(jnp.repeat(mb_D, MB_P, axis=1))

    h = x.reshape(s, d)
    for layer in range(DEPTH):
        j = layer // 2
        if layer % 2 == 0:
            p = _norm_matmul(h, g_mix, layer, w_in_e, j)
            o_a = _hgrn2(p, hg_lb_table, _rows3(hg_norm), layer, rb_hg)
            o_b = _rwkv7(p, rw, j, rb_rw)
            h = _matmul2_residual(o_a, o_b, w_out_e, j, h)
        else:
            p = _norm_matmul(h, g_mix, layer, w_in_o, j)
            o_a = _mlstm(p, ml_bias, _rows3(ml_norm), j, rb_ml)
            o_b = _ssd(p, mb_conv_w, _rows3(mb_conv_b), dt_bias, neg_a, d_rows, _rows3(mb_norm), j, rb_ssd)
            h = _matmul2_residual(o_a, o_b, w_out_o, j, h)
        act = _norm_swiglu(h, g_ffn, ffn_w_up, layer)
        h = _matmul_residual(act, w_down, layer, h, TN)
    return _rmsnorm(h, norm_final).reshape(b, s, d)
```

```python
import functools

import jax
import jax.numpy as jnp
from jax import lax
from jax.experimental import pallas as pl
from jax.experimental.pallas import tpu as pltpu

F32 = jnp.float32
BF16 = jnp.bfloat16
HI = lax.Precision.HIGHEST

D_MODEL = 2048
DEPTH = 4
CHUNK = 64
EPS = 1e-6
LANES = 128
SUB = 16

HG_HEADS, HG_D = 8, 128
RW_HEADS, RW_N, RW_W = 16, 64, 1024
RW_GN_EPS = 64e-5
ML_HEADS, ML_DQK, ML_DV = 4, 128, 256
ML_CAP = 15.0
MB_HEADS, MB_P, MB_N, MB_GROUPS, MB_CONV = 16, 64, 128, 2, 4
MB_DI = MB_HEADS * MB_P
D_FF = 5632

NP_ODD = 6144
RW_TAIL = 7168
RW_TAIL_COLS = 288
RW_TAIL_W = 512
OD_SMALL = 4096
OD_XBC = 4608

TM = 1024
TN = 512
VMEM_LIMIT = 56 * 1024 * 1024


def _cparams(sem):
    return pltpu.CompilerParams(dimension_semantics=sem, vmem_limit_bytes=VMEM_LIMIT)


def _dot(a, b, prec=None):
    return jnp.dot(a, b, preferred_element_type=F32, precision=prec)


def _dot_nt(a, b, prec=None):
    return lax.dot_general(a, b, (((1,), (1,)), ((), ())), preferred_element_type=F32, precision=prec)


def _dot_tn(a, b, prec=None):
    return lax.dot_general(a, b, (((0,), (0,)), ((), ())), preferred_element_type=F32, precision=prec)


def _sigmoid(x):
    return 1.0 / (1.0 + jnp.exp(-x))


def _log_sigmoid(x):
    return jnp.minimum(x, 0.0) - jnp.log1p(jnp.exp(-jnp.abs(x)))


def _softplus(x):
    return jnp.maximum(x, 0.0) + jnp.log1p(jnp.exp(-jnp.abs(x)))


def _iota(shape, dim):
    return lax.broadcasted_iota(jnp.int32, shape, dim)


def _tril(n, strict=False):
    r, c = _iota((n, n), 0), _iota((n, n), 1)
    return (r > c) if strict else (r >= c)


def _rms_rows(x, g):
    return (x * lax.rsqrt(jnp.mean(x * x, axis=-1, keepdims=True) + EPS)) * g


def _norm_mm_kernel(x_ref, g_ref, w_ref, o_ref, xn_ref):
    @pl.when(pl.program_id(1) == 0)
    def _():
        xn_ref[...] = _rms_rows(x_ref[...], g_ref[...]).astype(BF16)

    o_ref[...] = _dot_nt(xn_ref[...], w_ref[...])


def _norm_matmul(x, g, gl, wt, wl):
    s, k = x.shape
    n = wt.shape[1]
    w = wt
    return pl.pallas_call(
        _norm_mm_kernel,
        grid=(s // TM, pl.cdiv(n, TN)),
        in_specs=[pl.BlockSpec((TM, k), lambda i, j: (i, 0)),
                  pl.BlockSpec((None, 1, k), lambda i, j: (gl, 0, 0)),
                  pl.BlockSpec((None, TN, k), lambda i, j: (wl, j, 0))],
        out_specs=pl.BlockSpec((TM, TN), lambda i, j: (i, j)),
        out_shape=jax.ShapeDtypeStruct((s, n), F32),
        scratch_shapes=[pltpu.VMEM((TM, k), BF16)],
        compiler_params=_cparams(("parallel", "arbitrary")),
        name="norm_in_proj",
    )(x, g, w)


def _norm_swiglu_kernel(x_ref, g_ref, wg_ref, wu_ref, o_ref, xn_ref):
    @pl.when(pl.program_id(1) == 0)
    def _():
        xn_ref[...] = _rms_rows(x_ref[...], g_ref[...]).astype(BF16)

    xn = xn_ref[...]
    gate = _dot(xn, wg_ref[...].astype(BF16))
    up = _dot(xn, wu_ref[...].astype(BF16))
    o_ref[...] = (gate * _sigmoid(gate) * up).astype(BF16)


def _norm_swiglu(x, g, w_up, layer):
    s, k = x.shape
    nj = D_FF // TN
    return pl.pallas_call(
        _norm_swiglu_kernel,
        grid=(s // TM, nj),
        in_specs=[pl.BlockSpec((TM, k), lambda i, j: (i, 0)),
                  pl.BlockSpec((None, 1, k), lambda i, j: (layer, 0, 0)),
                  pl.BlockSpec((None, k, TN), lambda i, j: (layer, 0, j)),
                  pl.BlockSpec((None, k, TN), lambda i, j: (layer, 0, j + nj))],
        out_specs=pl.BlockSpec((TM, TN), lambda i, j: (i, j)),
        out_shape=jax.ShapeDtypeStruct((s, D_FF), BF16),
        scratch_shapes=[pltpu.VMEM((TM, k), BF16)],
        compiler_params=_cparams(("parallel", "arbitrary")),
        name="norm_ffn_up",
    )(x, g, w_up, w_up)


def _mm_res_kernel(x_ref, w_ref, r_ref, o_ref):
    o_ref[...] = r_ref[...] + _dot(x_ref[...], w_ref[...].astype(BF16))


def _matmul_residual(x, w, wl, res, tn):
    s, k = x.shape
    n = w.shape[2]
    return pl.pallas_call(
        _mm_res_kernel,
        grid=(s // TM, n // tn),
        in_specs=[pl.BlockSpec((TM, k), lambda i, j: (i, 0)),
                  pl.BlockSpec((None, k, tn), lambda i, j: (wl, 0, j)),
                  pl.BlockSpec((TM, tn), lambda i, j: (i, j))],
        out_specs=pl.BlockSpec((TM, tn), lambda i, j: (i, j)),
        out_shape=jax.ShapeDtypeStruct((s, n), F32),
        compiler_params=_cparams(("parallel", "parallel")),
        name="proj_residual",
    )(x, w, res)


def _mm2_res_kernel(x1_ref, x2_ref, w1_ref, w2_ref, r_ref, o_ref):
    o_ref[...] = r_ref[...] + (_dot(x1_ref[...], w1_ref[...]) + _dot(x2_ref[...], w2_ref[...]))


def _matmul2_residual(x1, x2, w, wl, res):
    s, k = x1.shape
    n = w.shape[2]
    assert x2.shape == (s, k) and w.shape[1] == 2 * k
    tm = min(2 * TM, s)
    return pl.pallas_call(
        _mm2_res_kernel,
        grid=(s // tm, n // TN),
        in_specs=[pl.BlockSpec((tm, k), lambda i, j: (i, 0)),
                  pl.BlockSpec((tm, k), lambda i, j: (i, 0)),
                  pl.BlockSpec((None, k, TN), lambda i, j: (wl, 0, j)),
                  pl.BlockSpec((None, k, TN), lambda i, j: (wl, 1, j)),
                  pl.BlockSpec((tm, TN), lambda i, j: (i, j))],
        out_specs=pl.BlockSpec((tm, TN), lambda i, j: (i, j)),
        out_shape=jax.ShapeDtypeStruct((s, n), F32),
        compiler_params=_cparams(("parallel", "parallel")),
        name="out_proj_residual",
    )(x1, x2, w, w, res)


def _rmsnorm_kernel(x_ref, g_ref, o_ref):
    o_ref[...] = _rms_rows(x_ref[...], g_ref[...])


def _rmsnorm(x, g):
    s, k = x.shape
    return pl.pallas_call(
        _rmsnorm_kernel,
        grid=(s // TM,),
        in_specs=[pl.BlockSpec((TM, k), lambda i: (i, 0)), pl.BlockSpec((1, k), lambda i: (0, 0))],
        out_specs=pl.BlockSpec((TM, k), lambda i: (i, 0)),
        out_shape=jax.ShapeDtypeStruct((s, k), F32),
        compiler_params=_cparams(("parallel",)),
        name="final_norm",
    )(x, g.reshape(1, k))


def _hgrn2_kernel(q_ref, f_ref, v_ref, g_ref, lbt_ref, nw_ref, o_ref, st_ref, *, layer, nchunk):
    @pl.when(pl.program_id(1) == 0)
    def _():
        st_ref[...] = jnp.zeros_like(st_ref)

    t = lbt_ref[...]
    e = jnp.exp(t - jnp.max(t, axis=0, keepdims=True))
    sm = e / jnp.sum(e, axis=0, keepdims=True)
    lb = jnp.zeros((1, HG_D), F32)
    for i in range(1, layer + 1):
        lb = lb + sm[i:i + 1, :]
    log_lb = jnp.log(lb)
    log_1m = jnp.log1p(-lb)
    nw = nw_ref[...]

    tril = _tril(CHUNK).astype(BF16)
    rowid = _iota((SUB, HG_D), 0)
    nsub = CHUNK // SUB
    cs = range(nchunk)
    rows = [slice(c * CHUNK, (c + 1) * CHUNK) for c in cs]

    fp = [f_ref[r, :] for r in rows]
    v = [v_ref[r, :] for r in rows]
    b2 = [log_1m + _log_sigmoid(x) for x in fp]
    log_f = [jnp.maximum(log_lb, x) + jnp.log1p(jnp.exp(-jnp.abs(log_lb - x))) for x in b2]
    k = [(1.0 - lb) * _sigmoid(-x) for x in fp]
    q = [x * _sigmoid(x) for x in (q_ref[r, :] for r in rows)]
    bc = [_cumsum_rows(tril, x) for x in log_f]
    b_last = [x[CHUNK - 1:CHUNK, :] for x in bc]
    kv = [_mm(v[c], k[c] * jnp.exp(b_last[c] - bc[c]), "bf16", "tn") for c in cs]
    st = st_ref[...]
    starts = []
    for c in cs:
        starts.append(st)
        st = jnp.exp(b_last[c]) * st + kv[c]
    st_ref[...] = st
    acc = [_mm(q[c] * jnp.exp(bc[c]), starts[c], "bf16", "nt") for c in cs]
    acc = [[a[blk * SUB:(blk + 1) * SUB] for blk in range(nsub)] for a in acc]
    cb = [(c, blk) for blk in range(1, nsub) for c in cs]
    ref_b = {(c, blk): bc[c][blk * SUB - 1:blk * SUB, :] for c, blk in cb}
    qx = {(c, blk): q[c][blk * SUB:(blk + 1) * SUB] * jnp.exp(bc[c][blk * SUB:(blk + 1) * SUB] - ref_b[c, blk])
          for c, blk in cb}
    kx = {(c, blk): k[c][0:blk * SUB] * jnp.exp(ref_b[c, blk] - bc[c][0:blk * SUB]) for c, blk in cb}
    sc = {e: _mm(qx[e], kx[e], "bf16", "nt") for e in cb}
    od = {(c, blk): _mm(sc[c, blk], v[c][0:blk * SUB], "bf16") for c, blk in cb}
    for c, blk in cb:
        acc[c][blk] = acc[c][blk] + od[c, blk]
    for s in range(SUB):
        for c in cs:
            for blk in range(nsub):
                lo = blk * SUB
                b_i = bc[c][lo:lo + SUB]
                d = b_i - bc[c][lo + s:lo + s + 1, :]
                if s > 0:
                    d = jnp.where(rowid >= s, d, -jnp.inf)
                col = jnp.sum(q[c][lo:lo + SUB] * k[c][lo + s:lo + s + 1, :] * jnp.exp(d), axis=1, keepdims=True)
                acc[c][blk] = acc[c][blk] + col * v[c][lo + s:lo + s + 1, :]
    for c in cs:
        o = jnp.concatenate(acc[c], axis=0)
        y = o * lax.rsqrt(jnp.mean(o * o, axis=-1, keepdims=True) + EPS) * nw
        o_ref[rows[c], :] = (y * _sigmoid(g_ref[rows[c], :])).astype(BF16)


def _hgrn2(p, lb_table, norm_w, layer, rb):
    s = p.shape[0]
    nb = HG_HEADS
    col = lambda off: pl.BlockSpec((rb, HG_D), lambda h, r, off=off: (r, off + h))
    return pl.pallas_call(
        functools.partial(_hgrn2_kernel, layer=layer, nchunk=rb // CHUNK),
        grid=(HG_HEADS, s // rb),
        in_specs=[col(0), col(nb), col(2 * nb), col(3 * nb),
                  pl.BlockSpec((DEPTH, HG_D), lambda h, r: (0, h)),
                  pl.BlockSpec((None, 1, HG_D), lambda h, r: (layer // 2, 0, h))],
        out_specs=pl.BlockSpec((rb, HG_D), lambda h, r: (r, h)),
        out_shape=jax.ShapeDtypeStruct((s, HG_HEADS * HG_D), BF16),
        scratch_shapes=[pltpu.VMEM((HG_D, HG_D), F32)],
        compiler_params=_cparams(("parallel", "arbitrary")),
        name="hgrn2",
    )(p, p, p, p, lb_table, norm_w)


def _shift_lerp(x, prev_row, mu):
    rolled = pltpu.roll(x, 1, 0)
    shifted = jnp.where(_iota(x.shape, 0) == 0, prev_row, rolled)
    return x + (shifted - x) * mu


def _split_bf16(x):
    hi = x.astype(BF16)
    lo = (x - hi.astype(F32)).astype(BF16)
    return hi, lo


def _mm(a, b, mode, kind="nn"):
    dot = {"nn": _dot, "nt": _dot_nt, "tn": _dot_tn}[kind]
    if mode == "hi":
        return dot(a, b, HI)
    if mode == "bf16":
        return dot(a.astype(BF16), b.astype(BF16))
    ah, al = _split_bf16(a)
    bh, bl = _split_bf16(b)
    return dot(ah, bh) + (dot(ah, bl) + dot(al, bh))


def _cumsum_rows(tril_bf16, x):
    hi, lo = _split_bf16(x)
    lo2 = (x - hi.astype(F32) - lo.astype(F32)).astype(BF16)
    return _dot(tril_bf16, hi) + (_dot(tril_bf16, lo) + _dot(tril_bf16, lo2))


def _neumann_inverse(mats, mode):
    n = mats[0].shape[0]
    eye = (_iota((n, n), 0) == _iota((n, n), 1)).astype(F32)
    p = [_mm(x, x, mode) for x in mats]
    m = [eye + x for x in mats]
    for _ in range(4):
        both = [_mm(x, jnp.concatenate([x, y], axis=1), mode) for x, y in zip(p, m)]
        m = [y + b[:, n:] for y, b in zip(m, both)]
        p = [b[:, :n] for b in both]
    return [y + _mm(x, y, mode) for x, y in zip(p, m)]


RW_PREC = dict(score="bf16", neumann="bf16", apply="bf16", state="bf16")


def _rwkv7_kernel(r_ref, k_ref, v_ref, tail_ref,
                  mu_r_ref, mu_k_ref, mu_v_ref, mu_tail_ref,
                  w0_ref, a0_ref, w2_ref, a2_ref, g2_ref, kk_ref, ka_ref, rk_ref, lnw_ref, lnb_ref,
                  o_ref, st_ref, prev_ref, *, nchunk, npair):
    rb = r_ref.shape[0]
    pc = RW_PREC
    wide = npair * LANES

    @pl.when(pl.program_id(1) == 0)
    def _():
        st_ref[...] = jnp.zeros_like(st_ref)
        prev_ref[...] = jnp.zeros_like(prev_ref)

    lane = _iota((1, LANES), 1)
    head0 = lane < RW_N

    def head_sum(x):
        s0 = jnp.sum(jnp.where(head0, x, 0.0), axis=1, keepdims=True)
        s1 = jnp.sum(jnp.where(head0, 0.0, x), axis=1, keepdims=True)
        return jnp.where(head0, s0, s1)

    tw = tail_ref.shape[1]
    tail_raw = jnp.where(_iota((1, tw), 1) < RW_TAIL_COLS, tail_ref[...], 0.0)
    tail = _shift_lerp(tail_raw, prev_ref[3:4, 0:tw], mu_tail_ref[...])
    prev_ref[3:4, 0:tw] = tail_raw[rb - 1:rb]
    wa_t = jnp.tanh(tail[:, 0:LANES]).astype(BF16)
    wa_b = tail[:, 0:LANES].astype(BF16)
    gl_s = _sigmoid(tail[:, LANES:3 * LANES]).astype(BF16)
    ps = range(npair)

    def tile_inputs(p):
        ts = slice(p * LANES, (p + 1) * LANES)
        r_raw, k_raw, v_raw = r_ref[:, ts], k_ref[:, ts], v_ref[:, ts]
        r_p = _shift_lerp(r_raw, prev_ref[0:1, ts], mu_r_ref[:, ts])
        k_p = _shift_lerp(k_raw, prev_ref[1:2, ts], mu_k_ref[:, ts])
        v_p = _shift_lerp(v_raw, prev_ref[2:3, ts], mu_v_ref[:, ts])
        prev_ref[0:1, ts] = r_raw[rb - 1:rb]
        prev_ref[1:2, ts] = k_raw[rb - 1:rb]
        prev_ref[2:3, ts] = v_raw[rb - 1:rb]
        w_log = -_softplus(-(w0_ref[:, ts] + _dot(wa_t, w2_ref[:, ts].astype(BF16)))) - 0.5
        lw_p = -jnp.exp(w_log)
        a_p = _sigmoid(a0_ref[:, ts] + _dot(wa_b, a2_ref[:, ts].astype(BF16)))
        gate_p = _dot(gl_s, g2_ref[:, ts].astype(BF16))
        kk_p = k_p * kk_ref[:, ts]
        kk_p = kk_p / jnp.maximum(jnp.sqrt(head_sum(kk_p * kk_p)), 1e-12)
        k_p = k_p * (1.0 + (a_p - 1.0) * ka_ref[:, ts])
        bonus_p = head_sum(r_p * k_p * rk_ref[:, ts]) * v_p
        return r_p, k_p, v_p, a_p, lw_p, kk_p, bonus_p, gate_p

    head1 = jnp.logical_not(head0)
    bd = (_iota((LANES, LANES), 0) < RW_N) == (_iota((LANES, LANES), 1) < RW_N)
    gs = 2 * CHUNK
    ri, ci = _iota((gs, gs), 0), _iota((gs, gs), 1)
    same = (ri // CHUNK) == (ci // CHUNK)
    strict = jnp.logical_and(same, ri > ci)
    incl = jnp.logical_and(same, ri >= ci)
    tril = _tril(CHUNK).astype(BF16)
    head00 = jnp.concatenate([head0, head0], axis=1)

    def stack_heads(x, masked):
        return jnp.concatenate([jnp.where(head0, x, 0.0), jnp.where(head1, x, 0.0)] if masked else [x, x], axis=0)

    def unstack_heads(x):
        return jnp.where(head0 if x.shape[1] == LANES else head00, x[:CHUNK], x[CHUNK:])

    cs = range(nchunk)
    grp = [(p, c) for c in cs for p in ps]
    rows = [slice(c * CHUNK, (c + 1) * CHUNK) for c in cs]
    v, g, rt, at, bt, kt, a_ab, rab, aakv, rakv, bonus, gate = ({} for _ in range(12))
    for p in ps:
        r_p, k_p, v_p, a_p, lw_p, kk_p, bonus[p], gate[p] = tile_inputs(p)
        mine = [(p, c) for c in cs]
        cut = lambda x: {(p, c): x[rows[c]] for c in cs}
        r_c, k_c, a_c, lw_c, kk_c = cut(r_p), cut(k_p), cut(a_p), cut(lw_p), cut(kk_p)
        v.update(cut(v_p))
        g.update({e: _cumsum_rows(tril, lw_c[e]) for e in mine})
        ieg = {e: jnp.exp(-g[e]) for e in mine}
        rt.update({e: r_c[e] * jnp.exp(g[e]) for e in mine})
        at.update({e: -kk_c[e] * jnp.exp(g[e] - lw_c[e]) for e in mine})
        bt.update({e: kk_c[e] * a_c[e] * ieg[e] for e in mine})
        kt.update({e: k_c[e] * ieg[e] for e in mine})
        l_a = {e: stack_heads(at[e], True) for e in mine}
        l_r = {e: stack_heads(rt[e], True) for e in mine}
        r_bk = {e: jnp.concatenate([stack_heads(bt[e], True), stack_heads(kt[e], True)], axis=0) for e in mine}
        s_a = {e: _mm(l_a[e], r_bk[e], pc["score"], "nt") for e in mine}
        s_r = {e: _mm(l_r[e], r_bk[e], pc["score"], "nt") for e in mine}
        a_ab.update({e: jnp.where(strict, s_a[e][:, :gs], 0.0) for e in mine})
        rab.update({e: jnp.where(incl, s_r[e][:, :gs], 0.0) for e in mine})
        v_st = {e: stack_heads(v[e], False) for e in mine}
        aakv.update({e: _mm(jnp.where(strict, s_a[e][:, gs:], 0.0), v_st[e], pc["apply"]) for e in mine})
        rakv.update({e: unstack_heads(_mm(jnp.where(incl, s_r[e][:, gs:], 0.0), v_st[e], pc["apply"])) for e in mine})
    t_inv = dict(zip(grp, _neumann_inverse([a_ab[e] for e in grp], pc["neumann"])))
    xs = {e: unstack_heads(_mm(t_inv[e], jnp.concatenate([stack_heads(at[e], False), aakv[e]], axis=1), pc["apply"]))
          for e in grp}
    ta = {e: xs[e][:, :LANES] for e in grp}
    tav = {e: xs[e][:, LANES:] for e in grp}
    egl = {e: jnp.exp(g[e][CHUNK - 1:CHUNK, :]) for e in grp}
    bte = {e: bt[e] * egl[e] for e in grp}
    m_mat = {e: jnp.where(bd, _mm(ta[e], bte[e], pc["state"], "tn"), 0.0) for e in grp}
    c_mat = {e: jnp.where(bd, _mm(jnp.concatenate([tav[e], v[e]], axis=0),
                                  jnp.concatenate([bte[e], kt[e] * egl[e]], axis=0), pc["state"], "tn"), 0.0)
             for e in grp}

    st = [st_ref[p] for p in ps]
    start = {}
    for c in cs:
        for p in ps:
            start[p, c] = st[p]
            st[p] = st[p] * egl[p, c] + _mm(st[p], m_mat[p, c], pc["state"]) + c_mat[p, c]
    for p in ps:
        st_ref[p] = st[p]
    u = {e: _mm(ta[e], start[e], pc["state"], "nt") + tav[e] for e in grp}
    o_in = {e: _mm(rt[e], start[e], pc["state"], "nt") for e in grp}
    y = {e: _mm(rab[e], stack_heads(u[e], True), pc["apply"]) for e in grp}
    o = {e: o_in[e] + (y[e][:CHUNK] + y[e][CHUNK:]) + rakv[e] for e in grp}

    for p in ps:
        ts = slice(p * LANES, (p + 1) * LANES)
        o_p = jnp.concatenate([o[p, c] for c in cs], axis=0)
        mu = head_sum(o_p) * (1.0 / RW_N)
        d = o_p - mu
        var = head_sum(d * d) * (1.0 / RW_N)
        y_p = d * lax.rsqrt(var + RW_GN_EPS) * lnw_ref[:, ts] + lnb_ref[:, ts]
        o_ref[:, ts] = ((y_p + bonus[p]) * gate[p]).astype(BF16)


def _rwkv7_params(mu, w0, w2, a0, a2, g2, k_k, k_a, r_k, ln_w, ln_b):
    n = mu.shape[0]
    row = lambda x: x.reshape(n, 1, -1)
    assert 2 * w2.shape[1] == LANES and 2 * a2.shape[1] == LANES
    mu_p = jnp.pad(mu, ((0, 0), (0, 3 * RW_W + RW_TAIL_W - mu.shape[1])))
    return dict(mu=row(mu_p), w0=row(w0), a0=row(a0),
                w2=jnp.concatenate([w2, jnp.zeros_like(w2)], axis=1),
                a2=jnp.concatenate([jnp.zeros_like(a2), a2], axis=1),
                g2=jnp.pad(g2, ((0, 0), (0, 2 * LANES - g2.shape[1]), (0, 0))),
                k_k=row(k_k), k_a=row(k_a), r_k=row(r_k), ln_w=row(ln_w), ln_b=row(ln_b))


RW_PAIRS = 4


def _rwkv7(p, prm, j, rb):
    s = p.shape[0]
    wide = RW_PAIRS * LANES
    base = 4 * HG_HEADS * LANES // wide
    nb = RW_W // wide
    col = lambda off: pl.BlockSpec((rb, wide), lambda h, r, off=off: (r, off + h))
    par = pl.BlockSpec((None, 1, wide), lambda h, r: (j, 0, h))
    par_off = lambda off: pl.BlockSpec((None, 1, wide), lambda h, r, off=off: (j, 0, off + h))
    lora = lambda rows: pl.BlockSpec((None, rows, wide), lambda h, r: (j, 0, h))
    assert RW_TAIL % RW_TAIL_W == 0 and (3 * RW_W) % RW_TAIL_W == 0
    return pl.pallas_call(
        functools.partial(_rwkv7_kernel, nchunk=rb // CHUNK, npair=RW_PAIRS),
        grid=(nb, s // rb),
        in_specs=[col(base), col(base + nb), col(base + 2 * nb),
                  pl.BlockSpec((rb, RW_TAIL_W), lambda h, r: (r, RW_TAIL // RW_TAIL_W)),
                  par, par_off(nb), par_off(2 * nb),
                  pl.BlockSpec((None, 1, RW_TAIL_W), lambda h, r: (j, 0, 3 * RW_W // RW_TAIL_W)),
                  par, par, lora(LANES), lora(LANES), lora(2 * LANES),
                  par, par, par, par, par],
        out_specs=pl.BlockSpec((rb, wide), lambda h, r: (r, h)),
        out_shape=jax.ShapeDtypeStruct((s, RW_W), BF16),
        scratch_shapes=[pltpu.VMEM((RW_PAIRS, LANES, LANES), F32), pltpu.VMEM((8, max(wide, RW_TAIL_W)), F32)],
        compiler_params=_cparams(("parallel", "arbitrary")),
        name="rwkv7",
    )(p, p, p, p, prm["mu"], prm["mu"], prm["mu"], prm["mu"],
      prm["w0"], prm["a0"], prm["w2"], prm["a2"], prm["g2"],
      prm["k_k"], prm["k_a"], prm["r_k"], prm["ln_w"], prm["ln_b"])


def _lane_col(x, idx):
    return jnp.sum(jnp.where(_iota(x.shape, 1) == idx, x, 0.0), axis=1, keepdims=True)


def _transpose_rows(x):
    eye = (_iota((LANES, LANES), 0) == _iota((LANES, LANES), 1)).astype(BF16)
    hi, lo = _split_bf16(x)
    lo2 = (x - hi.astype(F32) - lo.astype(F32)).astype(BF16)
    return _dot_nt(eye, hi) + (_dot_nt(eye, lo) + _dot_nt(eye, lo2))


def _mlstm_kernel(q_ref, k_ref, v_ref, og_ref, sm_ref, bias_ref, nw_ref, o_ref, c_ref, n_ref, m_ref, *, nchunk):
    @pl.when(pl.program_id(0) == 0)
    def _():
        c_ref[...] = jnp.zeros_like(c_ref)
        n_ref[...] = jnp.zeros_like(n_ref)
        m_ref[...] = jnp.zeros_like(m_ref)

    incl = _tril(CHUNK)
    tril = incl.astype(BF16)
    lane = _iota((CHUNK, LANES), 1)
    is_f = jnp.logical_and(lane >= ML_HEADS, lane < 2 * ML_HEADS)
    scale = ML_DQK ** -0.5
    cs = range(nchunk)
    hs = range(ML_HEADS)
    rows = [slice(c * CHUNK, (c + 1) * CHUNK) for c in cs]
    ch = [(c, h) for c in cs for h in hs]

    pre = [sm_ref[r, :] + bias_ref[...] for r in rows]
    cap = [ML_CAP * jnp.tanh(x / ML_CAP) for x in pre]
    x = [jnp.where(is_f, _log_sigmoid(y), y) for y in cap]
    cum = [_cumsum_rows(tril, y) for y in x]
    x_t = [_transpose_rows(y) for y in x]
    cum_t = [_transpose_rows(y) for y in cum]

    q = {(c, h): q_ref[rows[c], h * ML_DQK:(h + 1) * ML_DQK] * scale for c, h in ch}
    k = {(c, h): k_ref[rows[c], h * ML_DQK:(h + 1) * ML_DQK] for c, h in ch}
    v = {(c, h): v_ref[rows[c], h * ML_DV:(h + 1) * ML_DV] for c, h in ch}
    b_col = {(c, h): _lane_col(cum[c], ML_HEADS + h) for c, h in ch}
    i_col = {(c, h): _lane_col(x[c], h) for c, h in ch}
    b_row = {(c, h): cum_t[c][ML_HEADS + h:ML_HEADS + h + 1, :] for c, h in ch}
    i_row = {(c, h): x_t[c][h:h + 1, :] for c, h in ch}
    qk = {e: _mm(q[e], k[e], "bf16", "nt") for e in ch}
    dmat = {e: jnp.where(incl, b_col[e] - b_row[e] + i_row[e], -jnp.inf) for e in ch}
    dmax = {e: jnp.max(dmat[e], axis=1, keepdims=True) for e in ch}
    b_last = {e: b_row[e][:, CHUNK - 1:CHUNK] for e in ch}
    src_row = {e: b_last[e] - b_row[e] + i_row[e] for e in ch}
    src_col = {e: b_last[e] - b_col[e] + i_col[e] for e in ch}
    src_max = {e: jnp.max(src_row[e], axis=1, keepdims=True) for e in ch}

    m_prev, m_new = {}, {}
    for h in hs:
        m = m_ref[h:h + 1, 0:1]
        for c in cs:
            m_prev[c, h] = m
            m = jnp.maximum(b_last[c, h] + m, src_max[c, h])
            m_new[c, h] = m
        m_ref[h:h + 1, :] = jnp.broadcast_to(m, (1, LANES))
    inter = {e: b_col[e] + m_prev[e] for e in ch}
    m_t = {e: jnp.maximum(inter[e], dmax[e]) for e in ch}
    w_inter = {e: jnp.exp(inter[e] - m_t[e]) for e in ch}
    pmat = {e: jnp.exp(dmat[e] - m_t[e]) * qk[e] for e in ch}
    num = {e: _mm(pmat[e], v[e], "bf16") for e in ch}
    den = {e: jnp.sum(pmat[e], axis=1, keepdims=True) for e in ch}
    decay = {e: jnp.exp(b_last[e] + m_prev[e] - m_new[e]) for e in ch}
    wk = {e: jnp.exp(src_col[e] - m_new[e]) * k[e] for e in ch}
    kv = {e: _mm(wk[e], v[e], "bf16", "tn") for e in ch}
    ksum = {e: jnp.sum(wk[e], axis=0, keepdims=True) for e in ch}

    c_start, n_start = {}, {}
    for h in hs:
        c_mat, n_row = c_ref[h], n_ref[h:h + 1, :]
        for c in cs:
            c_start[c, h], n_start[c, h] = c_mat, n_row
            c_mat = decay[c, h] * c_mat + kv[c, h]
            n_row = decay[c, h] * n_row + ksum[c, h]
        c_ref[h] = c_mat
        n_ref[h:h + 1, :] = n_row

    qc = {e: _mm(q[e], c_start[e], "bf16") for e in ch}
    vs = {h: slice(h * ML_DV, (h + 1) * ML_DV) for h in hs}
    qn = {e: jnp.sum(q[e] * n_start[e], axis=1, keepdims=True) for e in ch}
    num = {e: num[e] + w_inter[e] * qc[e] for e in ch}
    den = {e: den[e] + w_inter[e] * qn[e] for e in ch}
    h_out = {e: num[e] / jnp.maximum(jnp.abs(den[e]), jnp.exp(-m_t[e])) for e in ch}
    ms = {e: jnp.mean(h_out[e] * h_out[e], axis=-1, keepdims=True) for e in ch}
    gate = {(c, h): _sigmoid(og_ref[rows[c], vs[h]]) for c, h in ch}
    y = {(c, h): h_out[c, h] * lax.rsqrt(ms[c, h] + EPS) * nw_ref[:, vs[h]] * gate[c, h] for c, h in ch}
    for c, h in ch:
        o_ref[rows[c], vs[h]] = y[c, h].astype(BF16)


def _mlstm(p, bias_row, norm_w, j, rb):
    s = p.shape[0]
    nq = ML_HEADS * ML_DQK
    nv = ML_HEADS * ML_DV
    return pl.pallas_call(
        functools.partial(_mlstm_kernel, nchunk=rb // CHUNK),
        grid=(s // rb,),
        in_specs=[pl.BlockSpec((rb, nq), lambda r: (r, 0)),
                  pl.BlockSpec((rb, nq), lambda r: (r, 1)),
                  pl.BlockSpec((rb, nv), lambda r: (r, 1)),
                  pl.BlockSpec((rb, nv), lambda r: (r, 2)),
                  pl.BlockSpec((rb, LANES), lambda r: (r, OD_SMALL // LANES)),
                  pl.BlockSpec((None, 1, LANES), lambda r: (j, 0, 0)),
                  pl.BlockSpec((None, 1, nv), lambda r: (j, 0, 0))],
        out_specs=pl.BlockSpec((rb, nv), lambda r: (r, 0)),
        out_shape=jax.ShapeDtypeStruct((s, nv), BF16),
        scratch_shapes=[pltpu.VMEM((ML_HEADS, ML_DQK, ML_DV), F32),
                        pltpu.VMEM((8, ML_DQK), F32), pltpu.VMEM((8, LANES), F32)],
        compiler_params=_cparams(("arbitrary",)),
        name="mlstm",
    )(p, p, p, p, p, bias_row, norm_w)


def _ssd_kernel(z_ref, xbc_ref, sm_ref, cw_ref, cb_ref, bias_ref, negA_ref, dvec_ref, nw_ref, o_ref,
                st_ref, xin_s, xc_s, *, nchunk):
    rb = z_ref.shape[0]
    pad = 8

    @pl.when(pl.program_id(0) == 0)
    def _():
        st_ref[...] = jnp.zeros_like(st_ref)
        xin_s[rb:rb + pad, :] = jnp.zeros((pad, xin_s.shape[1]), F32)

    xin_s[0:pad, :] = xin_s[rb:rb + pad, :]
    xin_s[pad:rb + pad, :] = xbc_ref[...]
    xin = xin_s[...]
    acc = cb_ref[...] + xin[pad:, :] * cw_ref[MB_CONV - 1:MB_CONV, :]
    for j in range(MB_CONV - 1):
        acc = acc + pltpu.roll(xin, MB_CONV - 1 - j, 0)[pad:, :] * cw_ref[j:j + 1, :]
    xc_s[...] = acc * _sigmoid(acc)

    incl = _tril(CHUNK)
    tril = incl.astype(BF16)
    lane = _iota((1, LANES), 1)
    head0 = lane < MB_P
    sub0 = _iota((LANES, 1), 0) < MB_P
    hpg = MB_HEADS // MB_GROUPS
    gw = MB_DI // MB_GROUPS
    dt_lane0 = 2 * ML_HEADS
    npair = hpg // 2
    cs = range(nchunk)
    rows = [slice(c * CHUNK, (c + 1) * CHUNK) for c in cs]
    cg = [(c, g) for c in cs for g in range(MB_GROUPS)]
    cgj = [(c, g, j) for c, g in cg for j in range(npair)]

    dt = [_softplus(sm_ref[r, :] + bias_ref[...]) for r in rows]
    a_cum = [_cumsum_rows(tril, negA_ref[...] * x) for x in dt]
    a_t = [_transpose_rows(x) for x in a_cum]
    dt_t = [_transpose_rows(x) for x in dt]
    bm = {(c, g): xc_s[rows[c], MB_DI + g * MB_N:MB_DI + (g + 1) * MB_N] for c, g in cg}
    cm = {(c, g): xc_s[rows[c], MB_DI + (MB_GROUPS + g) * MB_N:MB_DI + (MB_GROUPS + g + 1) * MB_N] for c, g in cg}
    cb = {e: _mm(cm[e], bm[e], "bf16", "nt") for e in cg}
    xs = {(g, j): slice(g * gw + j * LANES, g * gw + (j + 1) * LANES) for g in range(MB_GROUPS) for j in range(npair)}
    x2 = {(c, g, j): xc_s[rows[c], xs[g, j]] for c, g, j in cgj}

    sc, ea, ws, el = {}, {}, {}, {}
    for c, g, j in cgj:
        for e in range(2):
            ln = dt_lane0 + g * hpg + 2 * j + e
            a_col = _lane_col(a_cum[c], ln)
            a_row = a_t[c][ln:ln + 1, :]
            a_last = a_row[:, CHUNK - 1:CHUNK]
            seg = jnp.exp(jnp.where(incl, a_col - a_row, -jnp.inf))
            sc[c, g, j, e] = seg * cb[c, g] * dt_t[c][ln:ln + 1, :]
            ea[c, g, j, e] = jnp.exp(a_col)
            ws[c, g, j, e] = jnp.exp(a_last - a_col) * _lane_col(dt[c], ln)
            el[c, g, j, e] = jnp.exp(a_last)
    y0 = {e: _mm(sc[e + (0,)], x2[e], "bf16") for e in cgj}
    y1 = {e: _mm(sc[e + (1,)], x2[e], "bf16") for e in cgj}
    kv = {(c, g, j): _mm(x2[c, g, j] * jnp.where(head0, ws[c, g, j, 0], ws[c, g, j, 1]), bm[c, g], "bf16", "tn")
          for c, g, j in cgj}

    start = {}
    for g in range(MB_GROUPS):
        for j in range(npair):
            si = g * npair + j
            st = st_ref[si]
            for c in cs:
                start[c, g, j] = st
                st = jnp.where(sub0, el[c, g, j, 0], el[c, g, j, 1]) * st + kv[c, g, j]
            st_ref[si] = st

    ycs = {(c, g, j): _mm(cm[c, g], start[c, g, j], "bf16", "nt") for c, g, j in cgj}
    y = {e: jnp.where(head0, y0[e], y1[e]) + jnp.where(head0, ea[e + (0,)], ea[e + (1,)]) * ycs[e]
         + x2[e] * dvec_ref[:, xs[e[1], e[2]]] for e in cgj}
    gs = {g: slice(g * gw, (g + 1) * gw) for g in range(MB_GROUPS)}
    zg = {(c, g): z_ref[rows[c], gs[g]] for c, g in cg}
    yg = {(c, g): jnp.concatenate([y[c, g, j] for j in range(npair)], axis=1) * (zg[c, g] * _sigmoid(zg[c, g]))
          for c, g in cg}
    ms = {e: jnp.mean(yg[e] * yg[e], axis=-1, keepdims=True) for e in cg}
    out = {(c, g): yg[c, g] * lax.rsqrt(ms[c, g] + EPS) * nw_ref[:, gs[g]] for c, g in cg}
    for c, g in cg:
        o_ref[rows[c], gs[g]] = out[c, g].astype(BF16)


def _ssd(p, conv_w, conv_b, bias_row, negA_row, d_row, norm_w, j, rb):
    s = p.shape[0]
    nx = conv_w.shape[2]
    prow = lambda c: pl.BlockSpec((None, 1, c), lambda r: (j, 0, 0))
    return pl.pallas_call(
        functools.partial(_ssd_kernel, nchunk=rb // CHUNK),
        grid=(s // rb,),
        in_specs=[pl.BlockSpec((rb, MB_DI), lambda r: (r, 3)),
                  pl.BlockSpec((rb, nx), lambda r: (r, OD_XBC // nx)),
                  pl.BlockSpec((rb, LANES), lambda r: (r, OD_SMALL // LANES)),
                  pl.BlockSpec((None, MB_CONV, nx), lambda r: (j, 0, 0)),
                  prow(nx), prow(LANES), prow(LANES), prow(MB_DI), prow(MB_DI)],
        out_specs=pl.BlockSpec((rb, MB_DI), lambda r: (r, 0)),
        out_shape=jax.ShapeDtypeStruct((s, MB_DI), BF16),
        scratch_shapes=[pltpu.VMEM((MB_HEADS // 2, 2 * MB_P, MB_N), F32),
                        pltpu.VMEM((rb + 8, nx), F32), pltpu.VMEM((rb, nx), F32)],
        compiler_params=_cparams(("arbitrary",)),
        name="ssd",
    )(p, p, p, conv_w, conv_b, bias_row, negA_row, d_row, norm_w)


def _rows3(x):
    return x.reshape(x.shape[0], 1, -1)


def _odd_in_weight(w):
    wt = jnp.swapaxes(w, 1, 2)
    head_w = 2048 + 1024 + 1024 + 8 + 16
    return jnp.concatenate([wt[:, 0:2048], wt[:, 2056:3080], wt[:, 3080:4104], wt[:, 2048:2056], wt[:, 5640:5656],
                            jnp.zeros((wt.shape[0], OD_XBC - head_w, wt.shape[2]), wt.dtype), wt[:, 4104:5640]],
                           axis=1).astype(BF16)


def _small_rows(*xs):
    row = jnp.concatenate(xs, axis=1)
    return _rows3(jnp.pad(row, ((0, 0), (0, LANES - row.shape[1]))))


def kernel(x, norm_mix, norm_ffn, norm_final, w_in_even, w_out_even, hg_lb_table, hg_norm, rw_mu, rw_w0, rw_w2, rw_a0, rw_a2, rw_g2, rw_k_k, rw_k_a, rw_r_k, rw_ln_w, rw_ln_b, w_in_odd, w_out_odd, ml_i_bias, ml_f_bias, ml_norm, mb_conv_w, mb_conv_b, mb_dt_bias, mb_A_log, mb_D, mb_norm, ffn_w_up, ffn_w_down):
    b, s, d = x.shape
    assert b == 1 and d == D_MODEL and s % TM == 0
    rb_hg, rb_rw, rb_ml, rb_ssd = min(s, 256), min(s, 512), min(s, 256), min(s, 256)

    g_mix, g_ffn = _rows3(norm_mix), _rows3(norm_ffn)
    w_in_e, w_in_o = jnp.swapaxes(w_in_even, 1, 2).astype(BF16), _odd_in_weight(w_in_odd)
    w_out_e, w_out_o = w_out_even.astype(BF16), w_out_odd.astype(BF16)
    w_down = ffn_w_down.astype(BF16)
    rw = _rwkv7_params(rw_mu, rw_w0, rw_w2, rw_a0, rw_a2, rw_g2, rw_k_k, rw_k_a, rw_r_k, rw_ln_w, rw_ln_b)
    zeros8 = jnp.zeros((ml_i_bias.shape[0], 2 * ML_HEADS), F32)
    ml_bias = _small_rows(ml_i_bias, ml_f_bias)
    dt_bias = _small_rows(zeros8, mb_dt_bias)
    neg_a = _small_rows(zeros8, -jnp.exp(mb_A_log.astype(F32)))
    d_rows = _rows3(jnp.repeat(mb_D, MB_P, axis=1))

    h = x.reshape(s, d)
    for layer in range(DEPTH):
        j = layer // 2
        if layer % 2 == 0:
            p = _norm_matmul(h, g_mix, layer, w_in_e, j)
            o_a = _hgrn2(p, hg_lb_table, _rows3(hg_norm), layer, rb_hg)
            o_b = _rwkv7(p, rw, j, rb_rw)
            h = _matmul2_residual(o_a, o_b, w_out_e, j, h)
        else:
            p = _norm_matmul(h, g_mix, layer, w_in_o, j)
            o_a = _mlstm(p, ml_bias, _rows3(ml_norm), j, rb_ml)
            o_b = _ssd(p, mb_conv_w, _rows3(mb_conv_b), dt_bias, neg_a, d_rows, _rows3(mb_norm), j, rb_ssd)
            h = _matmul2_residual(o_a, o_b, w_out_o, j, h)
        act = _norm_swiglu(h, g_ffn, ffn_w_up, layer)
        h = _matmul_residual(act, w_down, layer, h, TN)
    return _rmsnorm(h, norm_final).reshape(b, s, d)
```

```python
import functools

import jax
import jax.numpy as jnp
from jax import lax
from jax.experimental import pallas as pl
from jax.experimental.pallas import tpu as pltpu

F32 = jnp.float32
BF16 = jnp.bfloat16
HI = lax.Precision.HIGHEST

D_MODEL = 2048
DEPTH = 4
CHUNK = 64
EPS = 1e-6
LANES = 128
SUB = 16

HG_HEADS, HG_D = 8, 128
RW_HEADS, RW_N, RW_W = 16, 64, 1024
RW_GN_EPS = 64e-5
ML_HEADS, ML_DQK, ML_DV = 4, 128, 256
ML_CAP = 15.0
MB_HEADS, MB_P, MB_N, MB_GROUPS, MB_CONV = 16, 64, 128, 2, 4
MB_DI = MB_HEADS * MB_P
D_FF = 5632

NP_ODD = 6144
RW_TAIL = 7168
RW_TAIL_COLS = 288
RW_TAIL_W = 512
OD_SMALL = 4096
OD_XBC = 4608

TM = 1024
TN = 512
TN_IN = 768
VMEM_LIMIT = 56 * 1024 * 1024


def _cparams(sem):
    return pltpu.CompilerParams(dimension_semantics=sem, vmem_limit_bytes=VMEM_LIMIT)


def _dot(a, b, prec=None):
    return jnp.dot(a, b, preferred_element_type=F32, precision=prec)


def _dot_nt(a, b, prec=None):
    return lax.dot_general(a, b, (((1,), (1,)), ((), ())), preferred_element_type=F32, precision=prec)


def _dot_tn(a, b, prec=None):
    return lax.dot_general(a, b, (((0,), (0,)), ((), ())), preferred_element_type=F32, precision=prec)


def _sigmoid(x):
    return 1.0 / (1.0 + jnp.exp(-x))


def _log_sigmoid(x):
    return jnp.minimum(x, 0.0) - jnp.log1p(jnp.exp(-jnp.abs(x)))


def _softplus(x):
    return jnp.maximum(x, 0.0) + jnp.log1p(jnp.exp(-jnp.abs(x)))


def _iota(shape, dim):
    return lax.broadcasted_iota(jnp.int32, shape, dim)


def _tril(n, strict=False):
    r, c = _iota((n, n), 0), _iota((n, n), 1)
    return (r > c) if strict else (r >= c)


def _rms_rows(x, g):
    return (x * lax.rsqrt(jnp.mean(x * x, axis=-1, keepdims=True) + EPS)) * g


def _norm_mm_kernel(x_ref, g_ref, w_ref, o_ref, xn_ref):
    @pl.when(pl.program_id(1) == 0)
    def _():
        xn_ref[...] = _rms_rows(x_ref[...], g_ref[...]).astype(BF16)

    o_ref[...] = _dot_nt(xn_ref[...], w_ref[...])


def _norm_matmul(x, g, gl, wt, wl):
    s, k = x.shape
    n = wt.shape[1]
    tn = TN_IN
    return pl.pallas_call(
        _norm_mm_kernel,
        grid=(s // TM, pl.cdiv(n, tn)),
        in_specs=[pl.BlockSpec((TM, k), lambda i, j: (i, 0)),
                  pl.BlockSpec((None, 1, k), lambda i, j: (gl, 0, 0)),
                  pl.BlockSpec((None, tn, k), lambda i, j: (wl, j, 0))],
        out_specs=pl.BlockSpec((TM, tn), lambda i, j: (i, j)),
        out_shape=jax.ShapeDtypeStruct((s, n), F32),
        scratch_shapes=[pltpu.VMEM((TM, k), BF16)],
        compiler_params=_cparams(("parallel", "arbitrary")),
        name="norm_in_proj",
    )(x, g, wt)


def _norm_swiglu_kernel(x_ref, g_ref, wg_ref, wu_ref, o_ref, xn_ref):
    @pl.when(pl.program_id(1) == 0)
    def _():
        xn_ref[...] = _rms_rows(x_ref[...], g_ref[...]).astype(BF16)

    xn = xn_ref[...]
    gate = _dot(xn, wg_ref[...].astype(BF16))
    up = _dot(xn, wu_ref[...].astype(BF16))
    o_ref[...] = (gate * _sigmoid(gate) * up).astype(BF16)


def _norm_swiglu(x, g, w_up, layer):
    s, k = x.shape
    nj = D_FF // TN
    return pl.pallas_call(
        _norm_swiglu_kernel,
        grid=(s // TM, nj),
        in_specs=[pl.BlockSpec((TM, k), lambda i, j: (i, 0)),
                  pl.BlockSpec((None, 1, k), lambda i, j: (layer, 0, 0)),
                  pl.BlockSpec((None, k, TN), lambda i, j: (layer, 0, j)),
                  pl.BlockSpec((None, k, TN), lambda i, j: (layer, 0, j + nj))],
        out_specs=pl.BlockSpec((TM, TN), lambda i, j: (i, j)),
        out_shape=jax.ShapeDtypeStruct((s, D_FF), BF16),
        scratch_shapes=[pltpu.VMEM((TM, k), BF16)],
        compiler_params=_cparams(("parallel", "arbitrary")),
        name="norm_ffn_up",
    )(x, g, w_up, w_up)


def _mm_res_kernel(x_ref, w_ref, r_ref, o_ref):
    o_ref[...] = r_ref[...] + _dot(x_ref[...], w_ref[...].astype(BF16))


def _matmul_residual(x, w, wl, res, tn):
    s, k = x.shape
    n = w.shape[2]
    return pl.pallas_call(
        _mm_res_kernel,
        grid=(s // TM, n // tn),
        in_specs=[pl.BlockSpec((TM, k), lambda i, j: (i, 0)),
                  pl.BlockSpec((None, k, tn), lambda i, j: (wl, 0, j)),
                  pl.BlockSpec((TM, tn), lambda i, j: (i, j))],
        out_specs=pl.BlockSpec((TM, tn), lambda i, j: (i, j)),
        out_shape=jax.ShapeDtypeStruct((s, n), F32),
        compiler_params=_cparams(("parallel", "parallel")),
        name="proj_residual",
    )(x, w, res)


def _mm2_res_kernel(x1_ref, x2_ref, w1_ref, w2_ref, r_ref, o_ref):
    o_ref[...] = r_ref[...] + (_dot(x1_ref[...], w1_ref[...]) + _dot(x2_ref[...], w2_ref[...]))


def _matmul2_residual(x1, x2, w, wl, res):
    s, k = x1.shape
    n = w.shape[2]
    assert x2.shape == (s, k) and w.shape[1] == 2 * k
    tm = min(2 * TM, s)
    return pl.pallas_call(
        _mm2_res_kernel,
        grid=(s // tm, n // TN),
        in_specs=[pl.BlockSpec((tm, k), lambda i, j: (i, 0)),
                  pl.BlockSpec((tm, k), lambda i, j: (i, 0)),
                  pl.BlockSpec((None, k, TN), lambda i, j: (wl, 0, j)),
                  pl.BlockSpec((None, k, TN), lambda i, j: (wl, 1, j)),
                  pl.BlockSpec((tm, TN), lambda i, j: (i, j))],
        out_specs=pl.BlockSpec((tm, TN), lambda i, j: (i, j)),
        out_shape=jax.ShapeDtypeStruct((s, n), F32),
        compiler_params=_cparams(("parallel", "parallel")),
        name="out_proj_residual",
    )(x1, x2, w, w, res)


def _rmsnorm_kernel(x_ref, g_ref, o_ref):
    o_ref[...] = _rms_rows(x_ref[...], g_ref[...])


def _rmsnorm(x, g):
    s, k = x.shape
    return pl.pallas_call(
        _rmsnorm_kernel,
        grid=(s // TM,),
        in_specs=[pl.BlockSpec((TM, k), lambda i: (i, 0)), pl.BlockSpec((1, k), lambda i: (0, 0))],
        out_specs=pl.BlockSpec((TM, k), lambda i: (i, 0)),
        out_shape=jax.ShapeDtypeStruct((s, k), F32),
        compiler_params=_cparams(("parallel",)),
        name="final_norm",
    )(x, g.reshape(1, k))


def _hgrn2_kernel(q_ref, f_ref, v_ref, g_ref, lbt_ref, nw_ref, o_ref, st_ref, *, layer, nchunk):
    @pl.when(pl.program_id(1) == 0)
    def _():
        st_ref[...] = jnp.zeros_like(st_ref)

    t = lbt_ref[...]
    e = jnp.exp(t - jnp.max(t, axis=0, keepdims=True))
    sm = e / jnp.sum(e, axis=0, keepdims=True)
    lb = jnp.zeros((1, HG_D), F32)
    for i in range(1, layer + 1):
        lb = lb + sm[i:i + 1, :]
    log_lb = jnp.log(lb)
    log_1m = jnp.log1p(-lb)
    nw = nw_ref[...]

    tril = _tril(CHUNK).astype(BF16)
    rowid = _iota((SUB, HG_D), 0)
    nsub = CHUNK // SUB
    cs = range(nchunk)
    rows = [slice(c * CHUNK, (c + 1) * CHUNK) for c in cs]

    fp = [f_ref[r, :] for r in rows]
    v = [v_ref[r, :] for r in rows]
    b2 = [log_1m + _log_sigmoid(x) for x in fp]
    log_f = [jnp.maximum(log_lb, x) + jnp.log1p(jnp.exp(-jnp.abs(log_lb - x))) for x in b2]
    k = [(1.0 - lb) * _sigmoid(-x) for x in fp]
    q = [x * _sigmoid(x) for x in (q_ref[r, :] for r in rows)]
    bc = [_cumsum_rows(tril, x) for x in log_f]
    b_last = [x[CHUNK - 1:CHUNK, :] for x in bc]
    kv = [_mm(v[c], k[c] * jnp.exp(b_last[c] - bc[c]), "bf16", "tn") for c in cs]
    st = st_ref[...]
    starts = []
    for c in cs:
        starts.append(st)
        st = jnp.exp(b_last[c]) * st + kv[c]
    st_ref[...] = st
    acc = [_mm(q[c] * jnp.exp(bc[c]), starts[c], "bf16", "nt") for c in cs]
    acc = [[a[blk * SUB:(blk + 1) * SUB] for blk in range(nsub)] for a in acc]
    cb = [(c, blk) for blk in range(1, nsub) for c in cs]
    ref_b = {(c, blk): bc[c][blk * SUB - 1:blk * SUB, :] for c, blk in cb}
    qx = {(c, blk): q[c][blk * SUB:(blk + 1) * SUB] * jnp.exp(bc[c][blk * SUB:(blk + 1) * SUB] - ref_b[c, blk])
          for c, blk in cb}
    kx = {(c, blk): k[c][0:blk * SUB] * jnp.exp(ref_b[c, blk] - bc[c][0:blk * SUB]) for c, blk in cb}
    sc = {e: _mm(qx[e], kx[e], "bf16", "nt") for e in cb}
    od = {(c, blk): _mm(sc[c, blk], v[c][0:blk * SUB], "bf16") for c, blk in cb}
    for c, blk in cb:
        acc[c][blk] = acc[c][blk] + od[c, blk]
    tile8 = 8
    acc = [[[a[t0:t0 + tile8] for t0 in range(0, SUB, tile8)] for a in blks] for blks in acc]
    for s in range(SUB):
        first = s // tile8
        for c in cs:
            for blk in range(nsub):
                lo = blk * SUB
                b_s, k_s, v_s = (x[c][lo + s:lo + s + 1, :] for x in (bc, k, v))
                for ti in range(first, SUB // tile8):
                    t0 = lo + ti * tile8
                    d = bc[c][t0:t0 + tile8] - b_s
                    if ti == first and s % tile8 > 0:
                        d = jnp.where(rowid[:tile8] >= s % tile8, d, -jnp.inf)
                    col = jnp.sum(q[c][t0:t0 + tile8] * k_s * jnp.exp(d), axis=1, keepdims=True)
                    acc[c][blk][ti] = acc[c][blk][ti] + col * v_s
    for c in cs:
        o = jnp.concatenate([piece for blk_acc in acc[c] for piece in blk_acc], axis=0)
        y = o * lax.rsqrt(jnp.mean(o * o, axis=-1, keepdims=True) + EPS) * nw
        o_ref[rows[c], :] = (y * _sigmoid(g_ref[rows[c], :])).astype(BF16)


def _hgrn2(p, lb_table, norm_w, layer, rb):
    s = p.shape[0]
    nb = HG_HEADS
    col = lambda off: pl.BlockSpec((rb, HG_D), lambda h, r, off=off: (r, off + h))
    return pl.pallas_call(
        functools.partial(_hgrn2_kernel, layer=layer, nchunk=rb // CHUNK),
        grid=(HG_HEADS, s // rb),
        in_specs=[col(0), col(nb), col(2 * nb), col(3 * nb),
                  pl.BlockSpec((DEPTH, HG_D), lambda h, r: (0, h)),
                  pl.BlockSpec((None, 1, HG_D), lambda h, r: (layer // 2, 0, h))],
        out_specs=pl.BlockSpec((rb, HG_D), lambda h, r: (r, h)),
        out_shape=jax.ShapeDtypeStruct((s, HG_HEADS * HG_D), BF16),
        scratch_shapes=[pltpu.VMEM((HG_D, HG_D), F32)],
        compiler_params=_cparams(("parallel", "arbitrary")),
        name="hgrn2",
    )(p, p, p, p, lb_table, norm_w)


def _shift_lerp(x, prev_row, mu):
    rolled = pltpu.roll(x, 1, 0)
    shifted = jnp.where(_iota(x.shape, 0) == 0, prev_row, rolled)
    return x + (shifted - x) * mu


def _split_bf16(x):
    hi = x.astype(BF16)
    lo = (x - hi.astype(F32)).astype(BF16)
    return hi, lo


def _mm(a, b, mode, kind="nn"):
    dot = {"nn": _dot, "nt": _dot_nt, "tn": _dot_tn}[kind]
    if mode == "hi":
        return dot(a, b, HI)
    if mode == "bf16":
        return dot(a.astype(BF16), b.astype(BF16))
    ah, al = _split_bf16(a)
    bh, bl = _split_bf16(b)
    return dot(ah, bh) + (dot(ah, bl) + dot(al, bh))


def _cumsum_rows(tril_bf16, x):
    hi, lo = _split_bf16(x)
    lo2 = (x - hi.astype(F32) - lo.astype(F32)).astype(BF16)
    return _dot(tril_bf16, hi) + (_dot(tril_bf16, lo) + _dot(tril_bf16, lo2))


def _neumann_inverse(mats, mode):
    n = mats[0].shape[0]
    eye = (_iota((n, n), 0) == _iota((n, n), 1)).astype(F32)
    p = [_mm(x, x, mode) for x in mats]
    m = [eye + x for x in mats]
    for _ in range(4):
        both = [_mm(x, jnp.concatenate([x, y], axis=1), mode) for x, y in zip(p, m)]
        m = [y + b[:, n:] for y, b in zip(m, both)]
        p = [b[:, :n] for b in both]
    return [y + _mm(x, y, mode) for x, y in zip(p, m)]


RW_PREC = dict(score="bf16", neumann="bf16", apply="bf16", state="bf16")


def _rwkv7_kernel(r_ref, k_ref, v_ref, tail_ref,
                  mu_r_ref, mu_k_ref, mu_v_ref, mu_tail_ref,
                  w0_ref, a0_ref, w2_ref, a2_ref, g2_ref, kk_ref, ka_ref, rk_ref, lnw_ref, lnb_ref,
                  o_ref, st_ref, prev_ref, *, nchunk, npair):
    rb = r_ref.shape[0]
    pc = RW_PREC
    wide = npair * LANES

    @pl.when(pl.program_id(1) == 0)
    def _():
        st_ref[...] = jnp.zeros_like(st_ref)
        prev_ref[...] = jnp.zeros_like(prev_ref)

    lane = _iota((1, LANES), 1)
    head0 = lane < RW_N

    def head_sum(x):
        s0 = jnp.sum(jnp.where(head0, x, 0.0), axis=1, keepdims=True)
        s1 = jnp.sum(jnp.where(head0, 0.0, x), axis=1, keepdims=True)
        return jnp.where(head0, s0, s1)

    tw = tail_ref.shape[1]
    tail_raw = jnp.where(_iota((1, tw), 1) < RW_TAIL_COLS, tail_ref[...], 0.0)
    tail = _shift_lerp(tail_raw, prev_ref[3:4, 0:tw], mu_tail_ref[...])
    prev_ref[3:4, 0:tw] = tail_raw[rb - 1:rb]
    wa_t = jnp.tanh(tail[:, 0:LANES]).astype(BF16)
    wa_b = tail[:, 0:LANES].astype(BF16)
    gl_s = _sigmoid(tail[:, LANES:3 * LANES]).astype(BF16)
    ps = range(npair)

    def tile_inputs(p):
        ts = slice(p * LANES, (p + 1) * LANES)
        r_raw, k_raw, v_raw = r_ref[:, ts], k_ref[:, ts], v_ref[:, ts]
        r_p = _shift_lerp(r_raw, prev_ref[0:1, ts], mu_r_ref[:, ts])
        k_p = _shift_lerp(k_raw, prev_ref[1:2, ts], mu_k_ref[:, ts])
        v_p = _shift_lerp(v_raw, prev_ref[2:3, ts], mu_v_ref[:, ts])
        prev_ref[0:1, ts] = r_raw[rb - 1:rb]
        prev_ref[1:2, ts] = k_raw[rb - 1:rb]
        prev_ref[2:3, ts] = v_raw[rb - 1:rb]
        w_log = -_softplus(-(w0_ref[:, ts] + _dot(wa_t, w2_ref[:, ts].astype(BF16)))) - 0.5
        lw_p = -jnp.exp(w_log)
        a_p = _sigmoid(a0_ref[:, ts] + _dot(wa_b, a2_ref[:, ts].astype(BF16)))
        gate_p = _dot(gl_s, g2_ref[:, ts].astype(BF16))
        kk_p = k_p * kk_ref[:, ts]
        kk_p = kk_p / jnp.maximum(jnp.sqrt(head_sum(kk_p * kk_p)), 1e-12)
        k_p = k_p * (1.0 + (a_p - 1.0) * ka_ref[:, ts])
        bonus_p = head_sum(r_p * k_p * rk_ref[:, ts]) * v_p
        return r_p, k_p, v_p, a_p, lw_p, kk_p, bonus_p, gate_p

    head1 = jnp.logical_not(head0)
    bd = (_iota((LANES, LANES), 0) < RW_N) == (_iota((LANES, LANES), 1) < RW_N)
    gs = 2 * CHUNK
    ri, ci = _iota((gs, gs), 0), _iota((gs, gs), 1)
    same = (ri // CHUNK) == (ci // CHUNK)
    strict = jnp.logical_and(same, ri > ci)
    incl = jnp.logical_and(same, ri >= ci)
    tril = _tril(CHUNK).astype(BF16)
    head00 = jnp.concatenate([head0, head0], axis=1)

    def stack_heads(x, masked):
        return jnp.concatenate([jnp.where(head0, x, 0.0), jnp.where(head1, x, 0.0)] if masked else [x, x], axis=0)

    def unstack_heads(x):
        return jnp.where(head0 if x.shape[1] == LANES else head00, x[:CHUNK], x[CHUNK:])

    cs = range(nchunk)
    grp = [(p, c) for c in cs for p in ps]
    rows = [slice(c * CHUNK, (c + 1) * CHUNK) for c in cs]
    v, g, rt, at, bt, kt, a_ab, rab, aakv, rakv, bonus, gate = ({} for _ in range(12))
    for p in ps:
        r_p, k_p, v_p, a_p, lw_p, kk_p, bonus[p], gate[p] = tile_inputs(p)
        mine = [(p, c) for c in cs]
        cut = lambda x: {(p, c): x[rows[c]] for c in cs}
        r_c, k_c, a_c, lw_c, kk_c = cut(r_p), cut(k_p), cut(a_p), cut(lw_p), cut(kk_p)
        v.update(cut(v_p))
        g.update({e: _cumsum_rows(tril, lw_c[e]) for e in mine})
        ieg = {e: jnp.exp(-g[e]) for e in mine}
        rt.update({e: r_c[e] * jnp.exp(g[e]) for e in mine})
        at.update({e: -kk_c[e] * jnp.exp(g[e] - lw_c[e]) for e in mine})
        bt.update({e: kk_c[e] * a_c[e] * ieg[e] for e in mine})
        kt.update({e: k_c[e] * ieg[e] for e in mine})
        l_a = {e: stack_heads(at[e], True) for e in mine}
        l_r = {e: stack_heads(rt[e], True) for e in mine}
        r_bk = {e: jnp.concatenate([stack_heads(bt[e], True), stack_heads(kt[e], True)], axis=0) for e in mine}
        s_a = {e: _mm(l_a[e], r_bk[e], pc["score"], "nt") for e in mine}
        s_r = {e: _mm(l_r[e], r_bk[e], pc["score"], "nt") for e in mine}
        a_ab.update({e: jnp.where(strict, s_a[e][:, :gs], 0.0) for e in mine})
        rab.update({e: jnp.where(incl, s_r[e][:, :gs], 0.0) for e in mine})
        v_st = {e: stack_heads(v[e], False) for e in mine}
        aakv.update({e: _mm(jnp.where(strict, s_a[e][:, gs:], 0.0), v_st[e], pc["apply"]) for e in mine})
        rakv.update({e: unstack_heads(_mm(jnp.where(incl, s_r[e][:, gs:], 0.0), v_st[e], pc["apply"])) for e in mine})
    t_inv = dict(zip(grp, _neumann_inverse([a_ab[e] for e in grp], pc["neumann"])))
    xs = {e: unstack_heads(_mm(t_inv[e], jnp.concatenate([stack_heads(at[e], False), aakv[e]], axis=1), pc["apply"]))
          for e in grp}
    ta = {e: xs[e][:, :LANES] for e in grp}
    tav = {e: xs[e][:, LANES:] for e in grp}
    egl = {e: jnp.exp(g[e][CHUNK - 1:CHUNK, :]) for e in grp}
    bte = {e: bt[e] * egl[e] for e in grp}
    m_mat = {e: jnp.where(bd, _mm(ta[e], bte[e], pc["state"], "tn"), 0.0) for e in grp}
    c_mat = {e: jnp.where(bd, _mm(jnp.concatenate([tav[e], v[e]], axis=0),
                                  jnp.concatenate([bte[e], kt[e] * egl[e]], axis=0), pc["state"], "tn"), 0.0)
             for e in grp}

    st = [st_ref[p] for p in ps]
    start = {}
    for c in cs:
        for p in ps:
            start[p, c] = st[p]
            st[p] = st[p] * egl[p, c] + _mm(st[p], m_mat[p, c], pc["state"]) + c_mat[p, c]
    for p in ps:
        st_ref[p] = st[p]
    u = {e: _mm(ta[e], start[e], pc["state"], "nt") + tav[e] for e in grp}
    o_in = {e: _mm(rt[e], start[e], pc["state"], "nt") for e in grp}
    y = {e: _mm(rab[e], stack_heads(u[e], True), pc["apply"]) for e in grp}
    o = {e: o_in[e] + (y[e][:CHUNK] + y[e][CHUNK:]) + rakv[e] for e in grp}

    for p in ps:
        ts = slice(p * LANES, (p + 1) * LANES)
        o_p = jnp.concatenate([o[p, c] for c in cs], axis=0)
        mu = head_sum(o_p) * (1.0 / RW_N)
        d = o_p - mu
        var = head_sum(d * d) * (1.0 / RW_N)
        y_p = d * lax.rsqrt(var + RW_GN_EPS) * lnw_ref[:, ts] + lnb_ref[:, ts]
        o_ref[:, ts] = ((y_p + bonus[p]) * gate[p]).astype(BF16)


def _rwkv7_params(mu, w0, w2, a0, a2, g2, k_k, k_a, r_k, ln_w, ln_b):
    n = mu.shape[0]
    row = lambda x: x.reshape(n, 1, -1)
    assert 2 * w2.shape[1] == LANES and 2 * a2.shape[1] == LANES
    mu_p = jnp.pad(mu, ((0, 0), (0, 3 * RW_W + RW_TAIL_W - mu.shape[1])))
    return dict(mu=row(mu_p), w0=row(w0), a0=row(a0),
                w2=jnp.concatenate([w2, jnp.zeros_like(w2)], axis=1),
                a2=jnp.concatenate([jnp.zeros_like(a2), a2], axis=1),
                g2=jnp.pad(g2, ((0, 0), (0, 2 * LANES - g2.shape[1]), (0, 0))),
                k_k=row(k_k), k_a=row(k_a), r_k=row(r_k), ln_w=row(ln_w), ln_b=row(ln_b))


RW_PAIRS = 4


def _rwkv7(p, prm, j, rb):
    s = p.shape[0]
    wide = RW_PAIRS * LANES
    base = 4 * HG_HEADS * LANES // wide
    nb = RW_W // wide
    col = lambda off: pl.BlockSpec((rb, wide), lambda h, r, off=off: (r, off + h))
    par = pl.BlockSpec((None, 1, wide), lambda h, r: (j, 0, h))
    par_off = lambda off: pl.BlockSpec((None, 1, wide), lambda h, r, off=off: (j, 0, off + h))
    lora = lambda rows: pl.BlockSpec((None, rows, wide), lambda h, r: (j, 0, h))
    assert RW_TAIL % RW_TAIL_W == 0 and (3 * RW_W) % RW_TAIL_W == 0
    return pl.pallas_call(
        functools.partial(_rwkv7_kernel, nchunk=rb // CHUNK, npair=RW_PAIRS),
        grid=(nb, s // rb),
        in_specs=[col(base), col(base + nb), col(base + 2 * nb),
                  pl.BlockSpec((rb, RW_TAIL_W), lambda h, r: (r, RW_TAIL // RW_TAIL_W)),
                  par, par_off(nb), par_off(2 * nb),
                  pl.BlockSpec((None, 1, RW_TAIL_W), lambda h, r: (j, 0, 3 * RW_W // RW_TAIL_W)),
                  par, par, lora(LANES), lora(LANES), lora(2 * LANES),
                  par, par, par, par, par],
        out_specs=pl.BlockSpec((rb, wide), lambda h, r: (r, h)),
        out_shape=jax.ShapeDtypeStruct((s, RW_W), BF16),
        scratch_shapes=[pltpu.VMEM((RW_PAIRS, LANES, LANES), F32), pltpu.VMEM((8, max(wide, RW_TAIL_W)), F32)],
        compiler_params=_cparams(("parallel", "arbitrary")),
        name="rwkv7",
    )(p, p, p, p, prm["mu"], prm["mu"], prm["mu"], prm["mu"],
      prm["w0"], prm["a0"], prm["w2"], prm["a2"], prm["g2"],
      prm["k_k"], prm["k_a"], prm["r_k"], prm["ln_w"], prm["ln_b"])


def _lane_col(x, idx):
    return jnp.sum(jnp.where(_iota(x.shape, 1) == idx, x, 0.0), axis=1, keepdims=True)


def _transpose_rows(x):
    eye = (_iota((LANES, LANES), 0) == _iota((LANES, LANES), 1)).astype(BF16)
    hi, lo = _split_bf16(x)
    lo2 = (x - hi.astype(F32) - lo.astype(F32)).astype(BF16)
    return _dot_nt(eye, hi) + (_dot_nt(eye, lo) + _dot_nt(eye, lo2))


def _mlstm_kernel(q_ref, k_ref, v_ref, og_ref, sm_ref, bias_ref, nw_ref, o_ref, c_ref, n_ref, m_ref, *, nchunk):
    @pl.when(pl.program_id(0) == 0)
    def _():
        c_ref[...] = jnp.zeros_like(c_ref)
        n_ref[...] = jnp.zeros_like(n_ref)
        m_ref[...] = jnp.zeros_like(m_ref)

    incl = _tril(CHUNK)
    tril = incl.astype(BF16)
    lane = _iota((CHUNK, LANES), 1)
    is_f = jnp.logical_and(lane >= ML_HEADS, lane < 2 * ML_HEADS)
    scale = ML_DQK ** -0.5
    cs = range(nchunk)
    hs = range(ML_HEADS)
    rows = [slice(c * CHUNK, (c + 1) * CHUNK) for c in cs]
    ch = [(c, h) for c in cs for h in hs]

    pre = [sm_ref[r, :] + bias_ref[...] for r in rows]
    cap = [ML_CAP * jnp.tanh(x / ML_CAP) for x in pre]
    x = [jnp.where(is_f, _log_sigmoid(y), y) for y in cap]
    cum = [_cumsum_rows(tril, y) for y in x]
    x_t = [_transpose_rows(y) for y in x]
    cum_t = [_transpose_rows(y) for y in cum]

    q = {(c, h): q_ref[rows[c], h * ML_DQK:(h + 1) * ML_DQK] * scale for c, h in ch}
    k = {(c, h): k_ref[rows[c], h * ML_DQK:(h + 1) * ML_DQK] for c, h in ch}
    v = {(c, h): v_ref[rows[c], h * ML_DV:(h + 1) * ML_DV] for c, h in ch}
    b_col = {(c, h): _lane_col(cum[c], ML_HEADS + h) for c, h in ch}
    i_col = {(c, h): _lane_col(x[c], h) for c, h in ch}
    b_row = {(c, h): cum_t[c][ML_HEADS + h:ML_HEADS + h + 1, :] for c, h in ch}
    i_row = {(c, h): x_t[c][h:h + 1, :] for c, h in ch}
    qk = {e: _mm(q[e], k[e], "bf16", "nt") for e in ch}
    dmat = {e: jnp.where(incl, b_col[e] - b_row[e] + i_row[e], -jnp.inf) for e in ch}
    dmax = {e: jnp.max(dmat[e], axis=1, keepdims=True) for e in ch}
    b_last = {e: b_row[e][:, CHUNK - 1:CHUNK] for e in ch}
    src_row = {e: b_last[e] - b_row[e] + i_row[e] for e in ch}
    src_col = {e: b_last[e] - b_col[e] + i_col[e] for e in ch}
    src_max = {e: jnp.max(src_row[e], axis=1, keepdims=True) for e in ch}

    m_prev, m_new = {}, {}
    for h in hs:
        m = m_ref[h:h + 1, 0:1]
        for c in cs:
            m_prev[c, h] = m
            m = jnp.maximum(b_last[c, h] + m, src_max[c, h])
            m_new[c, h] = m
        m_ref[h:h + 1, :] = jnp.broadcast_to(m, (1, LANES))
    inter = {e: b_col[e] + m_prev[e] for e in ch}
    m_t = {e: jnp.maximum(inter[e], dmax[e]) for e in ch}
    w_inter = {e: jnp.exp(inter[e] - m_t[e]) for e in ch}
    pmat = {e: jnp.exp(dmat[e] - m_t[e]) * qk[e] for e in ch}
    num = {e: _mm(pmat[e], v[e], "bf16") for e in ch}
    den = {e: jnp.sum(pmat[e], axis=1, keepdims=True) for e in ch}
    decay = {e: jnp.exp(b_last[e] + m_prev[e] - m_new[e]) for e in ch}
    wk = {e: jnp.exp(src_col[e] - m_new[e]) * k[e] for e in ch}
    kv = {e: _mm(wk[e], v[e], "bf16", "tn") for e in ch}
    ksum = {e: jnp.sum(wk[e], axis=0, keepdims=True) for e in ch}

    c_start, n_start = {}, {}
    for h in hs:
        c_mat, n_row = c_ref[h], n_ref[h:h + 1, :]
        for c in cs:
            c_start[c, h], n_start[c, h] = c_mat, n_row
            c_mat = decay[c, h] * c_mat + kv[c, h]
            n_row = decay[c, h] * n_row + ksum[c, h]
        c_ref[h] = c_mat
        n_ref[h:h + 1, :] = n_row

    qc = {e: _mm(q[e], c_start[e], "bf16") for e in ch}
    vs = {h: slice(h * ML_DV, (h + 1) * ML_DV) for h in hs}
    qn = {e: jnp.sum(q[e] * n_start[e], axis=1, keepdims=True) for e in ch}
    num = {e: num[e] + w_inter[e] * qc[e] for e in ch}
    den = {e: den[e] + w_inter[e] * qn[e] for e in ch}
    h_out = {e: num[e] / jnp.maximum(jnp.abs(den[e]), jnp.exp(-m_t[e])) for e in ch}
    ms = {e: jnp.mean(h_out[e] * h_out[e], axis=-1, keepdims=True) for e in ch}
    gate = {(c, h): _sigmoid(og_ref[rows[c], vs[h]]) for c, h in ch}
    y = {(c, h): h_out[c, h] * lax.rsqrt(ms[c, h] + EPS) * nw_ref[:, vs[h]] * gate[c, h] for c, h in ch}
    for c, h in ch:
        o_ref[rows[c], vs[h]] = y[c, h].astype(BF16)


def _mlstm(p, bias_row, norm_w, j, rb):
    s = p.shape[0]
    nq = ML_HEADS * ML_DQK
    nv = ML_HEADS * ML_DV
    return pl.pallas_call(
        functools.partial(_mlstm_kernel, nchunk=rb // CHUNK),
        grid=(s // rb,),
        in_specs=[pl.BlockSpec((rb, nq), lambda r: (r, 0)),
                  pl.BlockSpec((rb, nq), lambda r: (r, 1)),
                  pl.BlockSpec((rb, nv), lambda r: (r, 1)),
                  pl.BlockSpec((rb, nv), lambda r: (r, 2)),
                  pl.BlockSpec((rb, LANES), lambda r: (r, OD_SMALL // LANES)),
                  pl.BlockSpec((None, 1, LANES), lambda r: (j, 0, 0)),
                  pl.BlockSpec((None, 1, nv), lambda r: (j, 0, 0))],
        out_specs=pl.BlockSpec((rb, nv), lambda r: (r, 0)),
        out_shape=jax.ShapeDtypeStruct((s, nv), BF16),
        scratch_shapes=[pltpu.VMEM((ML_HEADS, ML_DQK, ML_DV), F32),
                        pltpu.VMEM((8, ML_DQK), F32), pltpu.VMEM((8, LANES), F32)],
        compiler_params=_cparams(("arbitrary",)),
        name="mlstm",
    )(p, p, p, p, p, bias_row, norm_w)


def _ssd_kernel(z_ref, xbc_ref, sm_ref, cw_ref, cb_ref, bias_ref, negA_ref, dvec_ref, nw_ref, o_ref,
                st_ref, xin_s, xc_s, *, nchunk):
    rb = z_ref.shape[0]
    pad = 8

    @pl.when(pl.program_id(0) == 0)
    def _():
        st_ref[...] = jnp.zeros_like(st_ref)
        xin_s[rb:rb + pad, :] = jnp.zeros((pad, xin_s.shape[1]), F32)

    xin_s[0:pad, :] = xin_s[rb:rb + pad, :]
    xin_s[pad:rb + pad, :] = xbc_ref[...]
    acc = cb_ref[...] + xin_s[pad:rb + pad, :] * cw_ref[MB_CONV - 1:MB_CONV, :]
    for j in range(MB_CONV - 1):
        lo = pad - (MB_CONV - 1 - j)
        acc = acc + xin_s[lo:lo + rb, :] * cw_ref[j:j + 1, :]
    xc_s[...] = acc * _sigmoid(acc)

    incl = _tril(CHUNK)
    tril = incl.astype(BF16)
    lane = _iota((1, LANES), 1)
    head0 = lane < MB_P
    sub0 = _iota((LANES, 1), 0) < MB_P
    hpg = MB_HEADS // MB_GROUPS
    gw = MB_DI // MB_GROUPS
    dt_lane0 = 2 * ML_HEADS
    npair = hpg // 2
    cs = range(nchunk)
    rows = [slice(c * CHUNK, (c + 1) * CHUNK) for c in cs]
    cg = [(c, g) for c in cs for g in range(MB_GROUPS)]
    cgj = [(c, g, j) for c, g in cg for j in range(npair)]

    dt = [_softplus(sm_ref[r, :] + bias_ref[...]) for r in rows]
    a_cum = [_cumsum_rows(tril, negA_ref[...] * x) for x in dt]
    a_t = [_transpose_rows(x) for x in a_cum]
    dt_t = [_transpose_rows(x) for x in dt]
    bm = {(c, g): xc_s[rows[c], MB_DI + g * MB_N:MB_DI + (g + 1) * MB_N] for c, g in cg}
    cm = {(c, g): xc_s[rows[c], MB_DI + (MB_GROUPS + g) * MB_N:MB_DI + (MB_GROUPS + g + 1) * MB_N] for c, g in cg}
    cb = {e: _mm(cm[e], bm[e], "bf16", "nt") for e in cg}
    xs = {(g, j): slice(g * gw + j * LANES, g * gw + (j + 1) * LANES) for g in range(MB_GROUPS) for j in range(npair)}
    x2 = {(c, g, j): xc_s[rows[c], xs[g, j]] for c, g, j in cgj}

    sc, ea, ws, el = {}, {}, {}, {}
    for c, g, j in cgj:
        for e in range(2):
            ln = dt_lane0 + g * hpg + 2 * j + e
            a_col = _lane_col(a_cum[c], ln)
            a_row = a_t[c][ln:ln + 1, :]
            a_last = a_row[:, CHUNK - 1:CHUNK]
            seg = jnp.exp(jnp.where(incl, a_col - a_row, -jnp.inf))
            sc[c, g, j, e] = seg * cb[c, g] * dt_t[c][ln:ln + 1, :]
            ea[c, g, j, e] = jnp.exp(a_col)
            ws[c, g, j, e] = jnp.exp(a_last - a_col) * _lane_col(dt[c], ln)
            el[c, g, j, e] = jnp.exp(a_last)
    y0 = {e: _mm(sc[e + (0,)], x2[e], "bf16") for e in cgj}
    y1 = {e: _mm(sc[e + (1,)], x2[e], "bf16") for e in cgj}
    kv = {(c, g, j): _mm(x2[c, g, j] * jnp.where(head0, ws[c, g, j, 0], ws[c, g, j, 1]), bm[c, g], "bf16", "tn")
          for c, g, j in cgj}

    start = {}
    for g in range(MB_GROUPS):
        for j in range(npair):
            si = g * npair + j
            st = st_ref[si]
            for c in cs:
                start[c, g, j] = st
                st = jnp.where(sub0, el[c, g, j, 0], el[c, g, j, 1]) * st + kv[c, g, j]
            st_ref[si] = st

    ycs = {(c, g, j): _mm(cm[c, g], start[c, g, j], "bf16", "nt") for c, g, j in cgj}
    y = {e: jnp.where(head0, y0[e], y1[e]) + jnp.where(head0, ea[e + (0,)], ea[e + (1,)]) * ycs[e]
         + x2[e] * dvec_ref[:, xs[e[1], e[2]]] for e in cgj}
    gs = {g: slice(g * gw, (g + 1) * gw) for g in range(MB_GROUPS)}
    zg = {(c, g): z_ref[rows[c], gs[g]] for c, g in cg}
    yg = {(c, g): jnp.concatenate([y[c, g, j] for j in range(npair)], axis=1) * (zg[c, g] * _sigmoid(zg[c, g]))
          for c, g in cg}
    ms = {e: jnp.mean(yg[e] * yg[e], axis=-1, keepdims=True) for e in cg}
    out = {(c, g): yg[c, g] * lax.rsqrt(ms[c, g] + EPS) * nw_ref[:, gs[g]] for c, g in cg}
    for c, g in cg:
        o_ref[rows[c], gs[g]] = out[c, g].astype(BF16)


def _ssd(p, conv_w, conv_b, bias_row, negA_row, d_row, norm_w, j, rb):
    s = p.shape[0]
    nx = conv_w.shape[2]
    prow = lambda c: pl.BlockSpec((None, 1, c), lambda r: (j, 0, 0))
    return pl.pallas_call(
        functools.partial(_ssd_kernel, nchunk=rb // CHUNK),
        grid=(s // rb,),
        in_specs=[pl.BlockSpec((rb, MB_DI), lambda r: (r, 3)),
                  pl.BlockSpec((rb, nx), lambda r: (r, OD_XBC // nx)),
                  pl.BlockSpec((rb, LANES), lambda r: (r, OD_SMALL // LANES)),
                  pl.BlockSpec((None, MB_CONV, nx), lambda r: (j, 0, 0)),
                  prow(nx), prow(LANES), prow(LANES), prow(MB_DI), prow(MB_DI)],
        out_specs=pl.BlockSpec((rb, MB_DI), lambda r: (r, 0)),
        out_shape=jax.ShapeDtypeStruct((s, MB_DI), BF16),
        scratch_shapes=[pltpu.VMEM((MB_HEADS // 2, 2 * MB_P, MB_N), F32),
                        pltpu.VMEM((rb + 8, nx), F32), pltpu.VMEM((rb, nx), F32)],
        compiler_params=_cparams(("arbitrary",)),
        name="ssd",
    )(p, p, p, conv_w, conv_b, bias_row, negA_row, d_row, norm_w)


def _rows3(x):
    return x.reshape(x.shape[0], 1, -1)


def _odd_in_weight(w):
    wt = jnp.swapaxes(w, 1, 2)
    head_w = 2048 + 1024 + 1024 + 8 + 16
    return jnp.concatenate([wt[:, 0:2048], wt[:, 2056:3080], wt[:, 3080:4104], wt[:, 2048:2056], wt[:, 5640:5656],
                            jnp.zeros((wt.shape[0], OD_XBC - head_w, wt.shape[2]), wt.dtype), wt[:, 4104:5640]],
                           axis=1).astype(BF16)


def _small_rows(*xs):
    row = jnp.concatenate(xs, axis=1)
    return _rows3(jnp.pad(row, ((0, 0), (0, LANES - row.shape[1]))))


def kernel(x, norm_mix, norm_ffn, norm_final, w_in_even, w_out_even, hg_lb_table, hg_norm, rw_mu, rw_w0, rw_w2, rw_a0, rw_a2, rw_g2, rw_k_k, rw_k_a, rw_r_k, rw_ln_w, rw_ln_b, w_in_odd, w_out_odd, ml_i_bias, ml_f_bias, ml_norm, mb_conv_w, mb_conv_b, mb_dt_bias, mb_A_log, mb_D, mb_norm, ffn_w_up, ffn_w_down):
    b, s, d = x.shape
    assert b == 1 and d == D_MODEL and s % TM == 0
    rb_hg, rb_rw, rb_ml, rb_ssd = min(s, 256), min(s, 512), min(s, 256), min(s, 256)

    g_mix, g_ffn = _rows3(norm_mix), _rows3(norm_ffn)
    w_in_e, w_in_o = jnp.swapaxes(w_in_even, 1, 2).astype(BF16), _odd_in_weight(w_in_odd)
    w_out_e, w_out_o = w_out_even.astype(BF16), w_out_odd.astype(BF16)
    w_down = ffn_w_down.astype(BF16)
    rw = _rwkv7_params(rw_mu, rw_w0, rw_w2, rw_a0, rw_a2, rw_g2, rw_k_k, rw_k_a, rw_r_k, rw_ln_w, rw_ln_b)
    zeros8 = jnp.zeros((ml_i_bias.shape[0], 2 * ML_HEADS), F32)
    ml_bias = _small_rows(ml_i_bias, ml_f_bias)
    dt_bias = _small_rows(zeros8, mb_dt_bias)
    neg_a = _small_rows(zeros8, -jnp.exp(mb_A_log.astype(F32)))
    d_rows = _rows3(jnp.repeat(mb_D, MB_P, axis=1))

    h = x.reshape(s, d)
    for layer in range(DEPTH):
        j = layer // 2
        if layer % 2 == 0:
            p = _norm_matmul(h, g_mix, layer, w_in_e, j)
            o_a = _hgrn2(p, hg_lb_table, _rows3(hg_norm), layer, rb_hg)
            o_b = _rwkv7(p, rw, j, rb_rw)
            h = _matmul2_residual(o_a, o_b, w_out_e, j, h)
        else:
            p = _norm_matmul(h, g_mix, layer, w_in_o, j)
            o_a = _mlstm(p, ml_bias, _rows3(ml_norm), j, rb_ml)
            o_b = _ssd(p, mb_conv_w, _rows3(mb_conv_b), dt_bias, neg_a, d_rows, _rows3(mb_norm), j, rb_ssd)
            h = _matmul2_residual(o_a, o_b, w_out_o, j, h)
        act = _norm_swiglu(h, g_ffn, ffn_w_up, layer)
        h = _matmul_residual(act, w_down, layer, h, TN)
    return _rmsnorm(h, norm_final).reshape(b, s, d)
```

```python
import functools

import jax
import jax.numpy as jnp
from jax import lax
from jax.experimental import pallas as pl
from jax.experimental.pallas import tpu as pltpu

F32 = jnp.float32
BF16 = jnp.bfloat16
HI = lax.Precision.HIGHEST

D_MODEL = 2048
DEPTH = 4
CHUNK = 64
EPS = 1e-6
LANES = 128
SUB = 16

HG_HEADS, HG_D = 8, 128
RW_HEADS, RW_N, RW_W = 16, 64, 1024
RW_GN_EPS = 64e-5
ML_HEADS, ML_DQK, ML_DV = 4, 128, 256
ML_CAP = 15.0
MB_HEADS, MB_P, MB_N, MB_GROUPS, MB_CONV = 16, 64, 128, 2, 4
MB_DI = MB_HEADS * MB_P
D_FF = 5632

NP_ODD = 6144
RW_TAIL = 7168
RW_TAIL_COLS = 288
RW_TAIL_W = 512
OD_SMALL = 4096
OD_XBC = 4608

TM = 1024
TN = 512
TN_IN = 768
VMEM_LIMIT = 56 * 1024 * 1024


def _cparams(sem):
    return pltpu.CompilerParams(dimension_semantics=sem, vmem_limit_bytes=VMEM_LIMIT)


def _dot(a, b, prec=None):
    return jnp.dot(a, b, preferred_element_type=F32, precision=prec)


def _dot_nt(a, b, prec=None):
    return lax.dot_general(a, b, (((1,), (1,)), ((), ())), preferred_element_type=F32, precision=prec)


def _dot_tn(a, b, prec=None):
    return lax.dot_general(a, b, (((0,), (0,)), ((), ())), preferred_element_type=F32, precision=prec)


def _sigmoid(x):
    return 1.0 / (1.0 + jnp.exp(-x))


def _log_sigmoid(x):
    return jnp.minimum(x, 0.0) - jnp.log1p(jnp.exp(-jnp.abs(x)))


def _softplus(x):
    return jnp.maximum(x, 0.0) + jnp.log1p(jnp.exp(-jnp.abs(x)))


def _iota(shape, dim):
    return lax.broadcasted_iota(jnp.int32, shape, dim)


def _tril(n, strict=False):
    r, c = _iota((n, n), 0), _iota((n, n), 1)
    return (r > c) if strict else (r >= c)


def _rms_rows(x, g):
    return (x * lax.rsqrt(jnp.mean(x * x, axis=-1, keepdims=True) + EPS)) * g


def _norm_mm_kernel(x_ref, g_ref, w_ref, o_ref, xn_ref):
    @pl.when(pl.program_id(1) == 0)
    def _():
        xn_ref[...] = _rms_rows(x_ref[...], g_ref[...]).astype(BF16)

    o_ref[...] = _dot_nt(xn_ref[...], w_ref[...])


def _norm_matmul(x, g, gl, wt, wl):
    s, k = x.shape
    n = wt.shape[1]
    tn = TN_IN
    return pl.pallas_call(
        _norm_mm_kernel,
        grid=(s // TM, pl.cdiv(n, tn)),
        in_specs=[pl.BlockSpec((TM, k), lambda i, j: (i, 0)),
                  pl.BlockSpec((None, 1, k), lambda i, j: (gl, 0, 0)),
                  pl.BlockSpec((None, tn, k), lambda i, j: (wl, j, 0))],
        out_specs=pl.BlockSpec((TM, tn), lambda i, j: (i, j)),
        out_shape=jax.ShapeDtypeStruct((s, n), F32),
        scratch_shapes=[pltpu.VMEM((TM, k), BF16)],
        compiler_params=_cparams(("parallel", "arbitrary")),
        name="norm_in_proj",
    )(x, g, wt)


def _norm_swiglu_kernel(x_ref, g_ref, wg_ref, wu_ref, o_ref, xn_ref):
    @pl.when(pl.program_id(1) == 0)
    def _():
        xn_ref[...] = _rms_rows(x_ref[...], g_ref[...]).astype(BF16)

    xn = xn_ref[...]
    gate = _dot(xn, wg_ref[...].astype(BF16))
    up = _dot(xn, wu_ref[...].astype(BF16))
    o_ref[...] = (gate * _sigmoid(gate) * up).astype(BF16)


def _norm_swiglu(x, g, w_up, layer):
    s, k = x.shape
    nj = D_FF // TN
    return pl.pallas_call(
        _norm_swiglu_kernel,
        grid=(s // TM, nj),
        in_specs=[pl.BlockSpec((TM, k), lambda i, j: (i, 0)),
                  pl.BlockSpec((None, 1, k), lambda i, j: (layer, 0, 0)),
                  pl.BlockSpec((None, k, TN), lambda i, j: (layer, 0, j)),
                  pl.BlockSpec((None, k, TN), lambda i, j: (layer, 0, j + nj))],
        out_specs=pl.BlockSpec((TM, TN), lambda i, j: (i, j)),
        out_shape=jax.ShapeDtypeStruct((s, D_FF), BF16),
        scratch_shapes=[pltpu.VMEM((TM, k), BF16)],
        compiler_params=_cparams(("parallel", "arbitrary")),
        name="norm_ffn_up",
    )(x, g, w_up, w_up)


def _mm_res_kernel(x_ref, w_ref, r_ref, o_ref):
    o_ref[...] = r_ref[...] + _dot(x_ref[...], w_ref[...].astype(BF16))


def _matmul_residual(x, w, wl, res, tn):
    s, k = x.shape
    n = w.shape[2]
    return pl.pallas_call(
        _mm_res_kernel,
        grid=(s // TM, n // tn),
        in_specs=[pl.BlockSpec((TM, k), lambda i, j: (i, 0)),
                  pl.BlockSpec((None, k, tn), lambda i, j: (wl, 0, j)),
                  pl.BlockSpec((TM, tn), lambda i, j: (i, j))],
        out_specs=pl.BlockSpec((TM, tn), lambda i, j: (i, j)),
        out_shape=jax.ShapeDtypeStruct((s, n), F32),
        compiler_params=_cparams(("parallel", "parallel")),
        name="proj_residual",
    )(x, w, res)


def _mm_res_norm_kernel(x_ref, w_ref, r_ref, g_ref, o_ref, *, tn):
    j = pl.program_id(1)
    cols = pl.ds(pl.multiple_of(j * tn, tn), tn)
    o_ref[:, cols] = r_ref[...] + _dot(x_ref[...], w_ref[...].astype(BF16))

    @pl.when(j == pl.num_programs(1) - 1)
    def _():
        o_ref[...] = _rms_rows(o_ref[...], g_ref[...])


def _matmul_residual_norm(x, w, wl, res, g, tn):
    s, k = x.shape
    n = w.shape[2]
    return pl.pallas_call(
        functools.partial(_mm_res_norm_kernel, tn=tn),
        grid=(s // TM, n // tn),
        in_specs=[pl.BlockSpec((TM, k), lambda i, j: (i, 0)),
                  pl.BlockSpec((None, k, tn), lambda i, j: (wl, 0, j)),
                  pl.BlockSpec((TM, tn), lambda i, j: (i, j)),
                  pl.BlockSpec((1, n), lambda i, j: (0, 0))],
        out_specs=pl.BlockSpec((TM, n), lambda i, j: (i, 0)),
        out_shape=jax.ShapeDtypeStruct((s, n), F32),
        compiler_params=_cparams(("parallel", "arbitrary")),
        name="proj_residual_norm",
    )(x, w, res, g.reshape(1, n))


def _mm2_res_kernel(x1_ref, x2_ref, w1_ref, w2_ref, r_ref, o_ref):
    o_ref[...] = r_ref[...] + (_dot(x1_ref[...], w1_ref[...]) + _dot(x2_ref[...], w2_ref[...]))


def _matmul2_residual(x1, x2, w, wl, res):
    s, k = x1.shape
    n = w.shape[2]
    assert x2.shape == (s, k) and w.shape[1] == 2 * k
    tm = min(2 * TM, s)
    return pl.pallas_call(
        _mm2_res_kernel,
        grid=(s // tm, n // TN),
        in_specs=[pl.BlockSpec((tm, k), lambda i, j: (i, 0)),
                  pl.BlockSpec((tm, k), lambda i, j: (i, 0)),
                  pl.BlockSpec((None, k, TN), lambda i, j: (wl, 0, j)),
                  pl.BlockSpec((None, k, TN), lambda i, j: (wl, 1, j)),
                  pl.BlockSpec((tm, TN), lambda i, j: (i, j))],
        out_specs=pl.BlockSpec((tm, TN), lambda i, j: (i, j)),
        out_shape=jax.ShapeDtypeStruct((s, n), F32),
        compiler_params=_cparams(("parallel", "parallel")),
        name="out_proj_residual",
    )(x1, x2, w, w, res)


def _rmsnorm_kernel(x_ref, g_ref, o_ref):
    o_ref[...] = _rms_rows(x_ref[...], g_ref[...])


def _rmsnorm(x, g):
    s, k = x.shape
    return pl.pallas_call(
        _rmsnorm_kernel,
        grid=(s // TM,),
        in_specs=[pl.BlockSpec((TM, k), lambda i: (i, 0)), pl.BlockSpec((1, k), lambda i: (0, 0))],
        out_specs=pl.BlockSpec((TM, k), lambda i: (i, 0)),
        out_shape=jax.ShapeDtypeStruct((s, k), F32),
        compiler_params=_cparams(("parallel",)),
        name="final_norm",
    )(x, g.reshape(1, k))


def _hgrn2_kernel(q_ref, f_ref, v_ref, g_ref, lbt_ref, nw_ref, o_ref, st_ref, *, layer, nchunk):
    @pl.when(pl.program_id(1) == 0)
    def _():
        st_ref[...] = jnp.zeros_like(st_ref)

    t = lbt_ref[...]
    e = jnp.exp(t - jnp.max(t, axis=0, keepdims=True))
    sm = e / jnp.sum(e, axis=0, keepdims=True)
    lb = jnp.zeros((1, HG_D), F32)
    for i in range(1, layer + 1):
        lb = lb + sm[i:i + 1, :]
    log_lb = jnp.log(lb)
    log_1m = jnp.log1p(-lb)
    nw = nw_ref[...]

    tril = _tril(CHUNK).astype(BF16)
    rowid = _iota((SUB, HG_D), 0)
    nsub = CHUNK // SUB
    cs = range(nchunk)
    rows = [slice(c * CHUNK, (c + 1) * CHUNK) for c in cs]

    fp = [f_ref[r, :] for r in rows]
    v = [v_ref[r, :] for r in rows]
    b2 = [log_1m + _log_sigmoid(x) for x in fp]
    log_f = [jnp.maximum(log_lb, x) + jnp.log1p(jnp.exp(-jnp.abs(log_lb - x))) for x in b2]
    k = [(1.0 - lb) * _sigmoid(-x) for x in fp]
    q = [x * _sigmoid(x) for x in (q_ref[r, :] for r in rows)]
    bc = [_cumsum_rows(tril, x) for x in log_f]
    b_last = [x[CHUNK - 1:CHUNK, :] for x in bc]
    kv = [_mm(v[c], k[c] * jnp.exp(b_last[c] - bc[c]), "bf16", "tn") for c in cs]
    st = st_ref[...]
    starts = []
    for c in cs:
        starts.append(st)
        st = jnp.exp(b_last[c]) * st + kv[c]
    st_ref[...] = st
    acc = [_mm(q[c] * jnp.exp(bc[c]), starts[c], "bf16", "nt") for c in cs]
    acc = [[a[blk * SUB:(blk + 1) * SUB] for blk in range(nsub)] for a in acc]
    cb = [(c, blk) for blk in range(1, nsub) for c in cs]
    ref_b = {(c, blk): bc[c][blk * SUB - 1:blk * SUB, :] for c, blk in cb}
    qx = {(c, blk): q[c][blk * SUB:(blk + 1) * SUB] * jnp.exp(bc[c][blk * SUB:(blk + 1) * SUB] - ref_b[c, blk])
          for c, blk in cb}
    kx = {(c, blk): k[c][0:blk * SUB] * jnp.exp(ref_b[c, blk] - bc[c][0:blk * SUB]) for c, blk in cb}
    sc = {e: _mm(qx[e], kx[e], "bf16", "nt") for e in cb}
    od = {(c, blk): _mm(sc[c, blk], v[c][0:blk * SUB], "bf16") for c, blk in cb}
    for c, blk in cb:
        acc[c][blk] = acc[c][blk] + od[c, blk]
    tile8 = 8
    acc = [[[a[t0:t0 + tile8] for t0 in range(0, SUB, tile8)] for a in blks] for blks in acc]
    for s in range(SUB):
        first = s // tile8
        for c in cs:
            for blk in range(nsub):
                lo = blk * SUB
                b_s, k_s, v_s = (x[c][lo + s:lo + s + 1, :] for x in (bc, k, v))
                for ti in range(first, SUB // tile8):
                    t0 = lo + ti * tile8
                    d = bc[c][t0:t0 + tile8] - b_s
                    if ti == first and s % tile8 > 0:
                        d = jnp.where(rowid[:tile8] >= s % tile8, d, -jnp.inf)
                    col = jnp.sum(q[c][t0:t0 + tile8] * k_s * jnp.exp(d), axis=1, keepdims=True)
                    acc[c][blk][ti] = acc[c][blk][ti] + col * v_s
    for c in cs:
        o = jnp.concatenate([piece for blk_acc in acc[c] for piece in blk_acc], axis=0)
        y = o * lax.rsqrt(jnp.mean(o * o, axis=-1, keepdims=True) + EPS) * nw
        o_ref[rows[c], :] = (y * _sigmoid(g_ref[rows[c], :])).astype(BF16)


def _hgrn2(p, lb_table, norm_w, layer, rb):
    s = p.shape[0]
    nb = HG_HEADS
    col = lambda off: pl.BlockSpec((rb, HG_D), lambda h, r, off=off: (r, off + h))
    return pl.pallas_call(
        functools.partial(_hgrn2_kernel, layer=layer, nchunk=rb // CHUNK),
        grid=(HG_HEADS, s // rb),
        in_specs=[col(0), col(nb), col(2 * nb), col(3 * nb),
                  pl.BlockSpec((DEPTH, HG_D), lambda h, r: (0, h)),
                  pl.BlockSpec((None, 1, HG_D), lambda h, r: (layer // 2, 0, h))],
        out_specs=pl.BlockSpec((rb, HG_D), lambda h, r: (r, h)),
        out_shape=jax.ShapeDtypeStruct((s, HG_HEADS * HG_D), BF16),
        scratch_shapes=[pltpu.VMEM((HG_D, HG_D), F32)],
        compiler_params=_cparams(("parallel", "arbitrary")),
        name="hgrn2",
    )(p, p, p, p, lb_table, norm_w)


def _shift_lerp(x, prev_row, mu):
    rolled = pltpu.roll(x, 1, 0)
    shifted = jnp.where(_iota(x.shape, 0) == 0, prev_row, rolled)
    return x + (shifted - x) * mu


def _split_bf16(x):
    hi = x.astype(BF16)
    lo = (x - hi.astype(F32)).astype(BF16)
    return hi, lo


def _mm(a, b, mode, kind="nn"):
    dot = {"nn": _dot, "nt": _dot_nt, "tn": _dot_tn}[kind]
    if mode == "hi":
        return dot(a, b, HI)
    if mode == "bf16":
        return dot(a.astype(BF16), b.astype(BF16))
    ah, al = _split_bf16(a)
    bh, bl = _split_bf16(b)
    return dot(ah, bh) + (dot(ah, bl) + dot(al, bh))


def _cumsum_rows(tril_bf16, x):
    hi, lo = _split_bf16(x)
    lo2 = (x - hi.astype(F32) - lo.astype(F32)).astype(BF16)
    return _dot(tril_bf16, hi) + (_dot(tril_bf16, lo) + _dot(tril_bf16, lo2))


def _neumann_inverse(mats, mode):
    n = mats[0].shape[0]
    eye = (_iota((n, n), 0) == _iota((n, n), 1)).astype(F32)
    p = [_mm(x, x, mode) for x in mats]
    m = [eye + x for x in mats]
    for _ in range(4):
        both = [_mm(x, jnp.concatenate([x, y], axis=1), mode) for x, y in zip(p, m)]
        m = [y + b[:, n:] for y, b in zip(m, both)]
        p = [b[:, :n] for b in both]
    return [y + _mm(x, y, mode) for x, y in zip(p, m)]


RW_PREC = dict(score="bf16", neumann="bf16", apply="bf16", state="bf16")


def _rwkv7_kernel(r_ref, k_ref, v_ref, tail_ref,
                  mu_r_ref, mu_k_ref, mu_v_ref, mu_tail_ref,
                  w0_ref, a0_ref, w2_ref, a2_ref, g2_ref, kk_ref, ka_ref, rk_ref, lnw_ref, lnb_ref,
                  o_ref, st_ref, prev_ref, *, nchunk, npair):
    rb = r_ref.shape[0]
    pc = RW_PREC
    wide = npair * LANES

    @pl.when(pl.program_id(1) == 0)
    def _():
        st_ref[...] = jnp.zeros_like(st_ref)
        prev_ref[...] = jnp.zeros_like(prev_ref)

    lane = _iota((1, LANES), 1)
    head0 = lane < RW_N

    def head_sum(x):
        s0 = jnp.sum(jnp.where(head0, x, 0.0), axis=1, keepdims=True)
        s1 = jnp.sum(jnp.where(head0, 0.0, x), axis=1, keepdims=True)
        return jnp.where(head0, s0, s1)

    tw = tail_ref.shape[1]
    tail_raw = jnp.where(_iota((1, tw), 1) < RW_TAIL_COLS, tail_ref[...], 0.0)
    tail = _shift_lerp(tail_raw, prev_ref[3:4, 0:tw], mu_tail_ref[...])
    prev_ref[3:4, 0:tw] = tail_raw[rb - 1:rb]
    wa_t = jnp.tanh(tail[:, 0:LANES]).astype(BF16)
    wa_b = tail[:, 0:LANES].astype(BF16)
    gl_s = _sigmoid(tail[:, LANES:3 * LANES]).astype(BF16)
    ps = range(npair)

    def tile_inputs(p):
        ts = slice(p * LANES, (p + 1) * LANES)
        r_raw, k_raw, v_raw = r_ref[:, ts], k_ref[:, ts], v_ref[:, ts]
        r_p = _shift_lerp(r_raw, prev_ref[0:1, ts], mu_r_ref[:, ts])
        k_p = _shift_lerp(k_raw, prev_ref[1:2, ts], mu_k_ref[:, ts])
        v_p = _shift_lerp(v_raw, prev_ref[2:3, ts], mu_v_ref[:, ts])
        prev_ref[0:1, ts] = r_raw[rb - 1:rb]
        prev_ref[1:2, ts] = k_raw[rb - 1:rb]
        prev_ref[2:3, ts] = v_raw[rb - 1:rb]
        w_log = -_softplus(-(w0_ref[:, ts] + _dot(wa_t, w2_ref[:, ts].astype(BF16)))) - 0.5
        lw_p = -jnp.exp(w_log)
        a_p = _sigmoid(a0_ref[:, ts] + _dot(wa_b, a2_ref[:, ts].astype(BF16)))
        gate_p = _dot(gl_s, g2_ref[:, ts].astype(BF16))
        kk_p = k_p * kk_ref[:, ts]
        kk_p = kk_p / jnp.maximum(jnp.sqrt(head_sum(kk_p * kk_p)), 1e-12)
        k_p = k_p * (1.0 + (a_p - 1.0) * ka_ref[:, ts])
        bonus_p = head_sum(r_p * k_p * rk_ref[:, ts]) * v_p
        return r_p, k_p, v_p, a_p, lw_p, kk_p, bonus_p, gate_p

    head1 = jnp.logical_not(head0)
    bd = (_iota((LANES, LANES), 0) < RW_N) == (_iota((LANES, LANES), 1) < RW_N)
    gs = 2 * CHUNK
    ri, ci = _iota((gs, gs), 0), _iota((gs, gs), 1)
    same = (ri // CHUNK) == (ci // CHUNK)
    strict = jnp.logical_and(same, ri > ci)
    incl = jnp.logical_and(same, ri >= ci)
    tril = _tril(CHUNK).astype(BF16)
    head00 = jnp.concatenate([head0, head0], axis=1)

    def stack_heads(x, masked):
        return jnp.concatenate([jnp.where(head0, x, 0.0), jnp.where(head1, x, 0.0)] if masked else [x, x], axis=0)

    def unstack_heads(x):
        return jnp.where(head0 if x.shape[1] == LANES else head00, x[:CHUNK], x[CHUNK:])

    cs = range(nchunk)
    grp = [(p, c) for c in cs for p in ps]
    rows = [slice(c * CHUNK, (c + 1) * CHUNK) for c in cs]
    v, g, rt, at, bt, kt, a_ab, rab, aakv, rakv, bonus, gate = ({} for _ in range(12))
    for p in ps:
        r_p, k_p, v_p, a_p, lw_p, kk_p, bonus[p], gate[p] = tile_inputs(p)
        mine = [(p, c) for c in cs]
        cut = lambda x: {(p, c): x[rows[c]] for c in cs}
        r_c, k_c, a_c, lw_c, kk_c = cut(r_p), cut(k_p), cut(a_p), cut(lw_p), cut(kk_p)
        v.update(cut(v_p))
        g.update({e: _cumsum_rows(tril, lw_c[e]) for e in mine})
        ieg = {e: jnp.exp(-g[e]) for e in mine}
        rt.update({e: r_c[e] * jnp.exp(g[e]) for e in mine})
        at.update({e: -kk_c[e] * jnp.exp(g[e] - lw_c[e]) for e in mine})
        bt.update({e: kk_c[e] * a_c[e] * ieg[e] for e in mine})
        kt.update({e: k_c[e] * ieg[e] for e in mine})
        l_a = {e: stack_heads(at[e], True) for e in mine}
        l_r = {e: stack_heads(rt[e], True) for e in mine}
        r_bk = {e: jnp.concatenate([stack_heads(bt[e], True), stack_heads(kt[e], True)], axis=0) for e in mine}
        s_a = {e: _mm(l_a[e], r_bk[e], pc["score"], "nt") for e in mine}
        s_r = {e: _mm(l_r[e], r_bk[e], pc["score"], "nt") for e in mine}
        a_ab.update({e: jnp.where(strict, s_a[e][:, :gs], 0.0) for e in mine})
        rab.update({e: jnp.where(incl, s_r[e][:, :gs], 0.0) for e in mine})
        v_st = {e: stack_heads(v[e], False) for e in mine}
        aakv.update({e: _mm(jnp.where(strict, s_a[e][:, gs:], 0.0), v_st[e], pc["apply"]) for e in mine})
        rakv.update({e: unstack_heads(_mm(jnp.where(incl, s_r[e][:, gs:], 0.0), v_st[e], pc["apply"])) for e in mine})
    t_inv = dict(zip(grp, _neumann_inverse([a_ab[e] for e in grp], pc["neumann"])))
    xs = {e: unstack_heads(_mm(t_inv[e], jnp.concatenate([stack_heads(at[e], False), aakv[e]], axis=1), pc["apply"]))
          for e in grp}
    ta = {e: xs[e][:, :LANES] for e in grp}
    tav = {e: xs[e][:, LANES:] for e in grp}
    egl = {e: jnp.exp(g[e][CHUNK - 1:CHUNK, :]) for e in grp}
    bte = {e: bt[e] * egl[e] for e in grp}
    m_mat = {e: jnp.where(bd, _mm(ta[e], bte[e], pc["state"], "tn"), 0.0) for e in grp}
    c_mat = {e: jnp.where(bd, _mm(jnp.concatenate([tav[e], v[e]], axis=0),
                                  jnp.concatenate([bte[e], kt[e] * egl[e]], axis=0), pc["state"], "tn"), 0.0)
             for e in grp}

    st = [st_ref[p] for p in ps]
    start = {}
    for c in cs:
        for p in ps:
            start[p, c] = st[p]
            st[p] = st[p] * egl[p, c] + _mm(st[p], m_mat[p, c], pc["state"]) + c_mat[p, c]
    for p in ps:
        st_ref[p] = st[p]
    u = {e: _mm(ta[e], start[e], pc["state"], "nt") + tav[e] for e in grp}
    o_in = {e: _mm(rt[e], start[e], pc["state"], "nt") for e in grp}
    y = {e: _mm(rab[e], stack_heads(u[e], True), pc["apply"]) for e in grp}
    o = {e: o_in[e] + (y[e][:CHUNK] + y[e][CHUNK:]) + rakv[e] for e in grp}

    for p in ps:
        ts = slice(p * LANES, (p + 1) * LANES)
        o_p = jnp.concatenate([o[p, c] for c in cs], axis=0)
        mu = head_sum(o_p) * (1.0 / RW_N)
        d = o_p - mu
        var = head_sum(d * d) * (1.0 / RW_N)
        y_p = d * lax.rsqrt(var + RW_GN_EPS) * lnw_ref[:, ts] + lnb_ref[:, ts]
        o_ref[:, ts] = ((y_p + bonus[p]) * gate[p]).astype(BF16)


def _rwkv7_params(mu, w0, w2, a0, a2, g2, k_k, k_a, r_k, ln_w, ln_b):
    n = mu.shape[0]
    row = lambda x: x.reshape(n, 1, -1)
    assert 2 * w2.shape[1] == LANES and 2 * a2.shape[1] == LANES
    mu_p = jnp.pad(mu, ((0, 0), (0, 3 * RW_W + RW_TAIL_W - mu.shape[1])))
    return dict(mu=row(mu_p), w0=row(w0), a0=row(a0),
                w2=jnp.concatenate([w2, jnp.zeros_like(w2)], axis=1),
                a2=jnp.concatenate([jnp.zeros_like(a2), a2], axis=1),
                g2=jnp.pad(g2, ((0, 0), (0, 2 * LANES - g2.shape[1]), (0, 0))),
                k_k=row(k_k), k_a=row(k_a), r_k=row(r_k), ln_w=row(ln_w), ln_b=row(ln_b))


RW_PAIRS = 4


def _rwkv7(p, prm, j, rb):
    s = p.shape[0]
    wide = RW_PAIRS * LANES
    base = 4 * HG_HEADS * LANES // wide
    nb = RW_W // wide
    col = lambda off: pl.BlockSpec((rb, wide), lambda h, r, off=off: (r, off + h))
    par = pl.BlockSpec((None, 1, wide), lambda h, r: (j, 0, h))
    par_off = lambda off: pl.BlockSpec((None, 1, wide), lambda h, r, off=off: (j, 0, off + h))
    lora = lambda rows: pl.BlockSpec((None, rows, wide), lambda h, r: (j, 0, h))
    assert RW_TAIL % RW_TAIL_W == 0 and (3 * RW_W) % RW_TAIL_W == 0
    return pl.pallas_call(
        functools.partial(_rwkv7_kernel, nchunk=rb // CHUNK, npair=RW_PAIRS),
        grid=(nb, s // rb),
        in_specs=[col(base), col(base + nb), col(base + 2 * nb),
                  pl.BlockSpec((rb, RW_TAIL_W), lambda h, r: (r, RW_TAIL // RW_TAIL_W)),
                  par, par_off(nb), par_off(2 * nb),
                  pl.BlockSpec((None, 1, RW_TAIL_W), lambda h, r: (j, 0, 3 * RW_W // RW_TAIL_W)),
                  par, par, lora(LANES), lora(LANES), lora(2 * LANES),
                  par, par, par, par, par],
        out_specs=pl.BlockSpec((rb, wide), lambda h, r: (r, h)),
        out_shape=jax.ShapeDtypeStruct((s, RW_W), BF16),
        scratch_shapes=[pltpu.VMEM((RW_PAIRS, LANES, LANES), F32), pltpu.VMEM((8, max(wide, RW_TAIL_W)), F32)],
        compiler_params=_cparams(("parallel", "arbitrary")),
        name="rwkv7",
    )(p, p, p, p, prm["mu"], prm["mu"], prm["mu"], prm["mu"],
      prm["w0"], prm["a0"], prm["w2"], prm["a2"], prm["g2"],
      prm["k_k"], prm["k_a"], prm["r_k"], prm["ln_w"], prm["ln_b"])


def _lane_col(x, idx):
    return jnp.sum(jnp.where(_iota(x.shape, 1) == idx, x, 0.0), axis=1, keepdims=True)


def _transpose_rows(x):
    eye = (_iota((LANES, LANES), 0) == _iota((LANES, LANES), 1)).astype(BF16)
    hi, lo = _split_bf16(x)
    lo2 = (x - hi.astype(F32) - lo.astype(F32)).astype(BF16)
    return _dot_nt(eye, hi) + (_dot_nt(eye, lo) + _dot_nt(eye, lo2))


def _mlstm_kernel(q_ref, k_ref, v_ref, og_ref, sm_ref, bias_ref, nw_ref, o_ref, c_ref, n_ref, m_ref, *, nchunk):
    @pl.when(pl.program_id(0) == 0)
    def _():
        c_ref[...] = jnp.zeros_like(c_ref)
        n_ref[...] = jnp.zeros_like(n_ref)
        m_ref[...] = jnp.zeros_like(m_ref)

    incl = _tril(CHUNK)
    tril = incl.astype(BF16)
    lane = _iota((CHUNK, LANES), 1)
    is_f = jnp.logical_and(lane >= ML_HEADS, lane < 2 * ML_HEADS)
    scale = ML_DQK ** -0.5
    cs = range(nchunk)
    hs = range(ML_HEADS)
    rows = [slice(c * CHUNK, (c + 1) * CHUNK) for c in cs]
    ch = [(c, h) for c in cs for h in hs]

    pre = [sm_ref[r, :] + bias_ref[...] for r in rows]
    cap = [ML_CAP * jnp.tanh(x / ML_CAP) for x in pre]
    x = [jnp.where(is_f, _log_sigmoid(y), y) for y in cap]
    cum = [_cumsum_rows(tril, y) for y in x]
    x_t = [_transpose_rows(y) for y in x]
    cum_t = [_transpose_rows(y) for y in cum]

    q = {(c, h): q_ref[rows[c], h * ML_DQK:(h + 1) * ML_DQK] * scale for c, h in ch}
    k = {(c, h): k_ref[rows[c], h * ML_DQK:(h + 1) * ML_DQK] for c, h in ch}
    v = {(c, h): v_ref[rows[c], h * ML_DV:(h + 1) * ML_DV] for c, h in ch}
    b_col = {(c, h): _lane_col(cum[c], ML_HEADS + h) for c, h in ch}
    i_col = {(c, h): _lane_col(x[c], h) for c, h in ch}
    b_row = {(c, h): cum_t[c][ML_HEADS + h:ML_HEADS + h + 1, :] for c, h in ch}
    i_row = {(c, h): x_t[c][h:h + 1, :] for c, h in ch}
    qk = {e: _mm(q[e], k[e], "bf16", "nt") for e in ch}
    dmat = {e: jnp.where(incl, b_col[e] - b_row[e] + i_row[e], -jnp.inf) for e in ch}
    dmax = {e: jnp.max(dmat[e], axis=1, keepdims=True) for e in ch}
    b_last = {e: b_row[e][:, CHUNK - 1:CHUNK] for e in ch}
    src_row = {e: b_last[e] - b_row[e] + i_row[e] for e in ch}
    src_col = {e: b_last[e] - b_col[e] + i_col[e] for e in ch}
    src_max = {e: jnp.max(src_row[e], axis=1, keepdims=True) for e in ch}

    m_prev, m_new = {}, {}
    for h in hs:
        m = m_ref[h:h + 1, 0:1]
        for c in cs:
            m_prev[c, h] = m
            m = jnp.maximum(b_last[c, h] + m, src_max[c, h])
            m_new[c, h] = m
        m_ref[h:h + 1, :] = jnp.broadcast_to(m, (1, LANES))
    inter = {e: b_col[e] + m_prev[e] for e in ch}
    m_t = {e: jnp.maximum(inter[e], dmax[e]) for e in ch}
    w_inter = {e: jnp.exp(inter[e] - m_t[e]) for e in ch}
    pmat = {e: jnp.exp(dmat[e] - m_t[e]) * qk[e] for e in ch}
    num = {e: _mm(pmat[e], v[e], "bf16") for e in ch}
    den = {e: jnp.sum(pmat[e], axis=1, keepdims=True) for e in ch}
    decay = {e: jnp.exp(b_last[e] + m_prev[e] - m_new[e]) for e in ch}
    wk = {e: jnp.exp(src_col[e] - m_new[e]) * k[e] for e in ch}
    kv = {e: _mm(wk[e], v[e], "bf16", "tn") for e in ch}
    ksum = {e: jnp.sum(wk[e], axis=0, keepdims=True) for e in ch}

    c_start, n_start = {}, {}
    for h in hs:
        c_mat, n_row = c_ref[h], n_ref[h:h + 1, :]
        for c in cs:
            c_start[c, h], n_start[c, h] = c_mat, n_row
            c_mat = decay[c, h] * c_mat + kv[c, h]
            n_row = decay[c, h] * n_row + ksum[c, h]
        c_ref[h] = c_mat
        n_ref[h:h + 1, :] = n_row

    qc = {e: _mm(q[e], c_start[e], "bf16") for e in ch}
    vs = {h: slice(h * ML_DV, (h + 1) * ML_DV) for h in hs}
    qn = {e: jnp.sum(q[e] * n_start[e], axis=1, keepdims=True) for e in ch}
    num = {e: num[e] + w_inter[e] * qc[e] for e in ch}
    den = {e: den[e] + w_inter[e] * qn[e] for e in ch}
    h_out = {e: num[e] / jnp.maximum(jnp.abs(den[e]), jnp.exp(-m_t[e])) for e in ch}
    ms = {e: jnp.mean(h_out[e] * h_out[e], axis=-1, keepdims=True) for e in ch}
    gate = {(c, h): _sigmoid(og_ref[rows[c], vs[h]]) for c, h in ch}
    y = {(c, h): h_out[c, h] * lax.rsqrt(ms[c, h] + EPS) * nw_ref[:, vs[h]] * gate[c, h] for c, h in ch}
    for c, h in ch:
        o_ref[rows[c], vs[h]] = y[c, h].astype(BF16)


def _mlstm(p, bias_row, norm_w, j, rb):
    s = p.shape[0]
    nq = ML_HEADS * ML_DQK
    nv = ML_HEADS * ML_DV
    return pl.pallas_call(
        functools.partial(_mlstm_kernel, nchunk=rb // CHUNK),
        grid=(s // rb,),
        in_specs=[pl.BlockSpec((rb, nq), lambda r: (r, 0)),
                  pl.BlockSpec((rb, nq), lambda r: (r, 1)),
                  pl.BlockSpec((rb, nv), lambda r: (r, 1)),
                  pl.BlockSpec((rb, nv), lambda r: (r, 2)),
                  pl.BlockSpec((rb, LANES), lambda r: (r, OD_SMALL // LANES)),
                  pl.BlockSpec((None, 1, LANES), lambda r: (j, 0, 0)),
                  pl.BlockSpec((None, 1, nv), lambda r: (j, 0, 0))],
        out_specs=pl.BlockSpec((rb, nv), lambda r: (r, 0)),
        out_shape=jax.ShapeDtypeStruct((s, nv), BF16),
        scratch_shapes=[pltpu.VMEM((ML_HEADS, ML_DQK, ML_DV), F32),
                        pltpu.VMEM((8, ML_DQK), F32), pltpu.VMEM((8, LANES), F32)],
        compiler_params=_cparams(("arbitrary",)),
        name="mlstm",
    )(p, p, p, p, p, bias_row, norm_w)


def _ssd_kernel(z_ref, xbc_ref, sm_ref, cw_ref, cb_ref, bias_ref, negA_ref, dvec_ref, nw_ref, o_ref,
                st_ref, xin_s, xc_s, *, nchunk):
    rb = z_ref.shape[0]
    pad = 8

    @pl.when(pl.program_id(0) == 0)
    def _():
        st_ref[...] = jnp.zeros_like(st_ref)
        xin_s[rb:rb + pad, :] = jnp.zeros((pad, xin_s.shape[1]), F32)

    xin_s[0:pad, :] = xin_s[rb:rb + pad, :]
    xin_s[pad:rb + pad, :] = xbc_ref[...]
    acc = cb_ref[...] + xin_s[pad:rb + pad, :] * cw_ref[MB_CONV - 1:MB_CONV, :]
    for j in range(MB_CONV - 1):
        lo = pad - (MB_CONV - 1 - j)
        acc = acc + xin_s[lo:lo + rb, :] * cw_ref[j:j + 1, :]
    xc_s[...] = acc * _sigmoid(acc)

    incl = _tril(CHUNK)
    tril = incl.astype(BF16)
    lane = _iota((1, LANES), 1)
    head0 = lane < MB_P
    sub0 = _iota((LANES, 1), 0) < MB_P
    hpg = MB_HEADS // MB_GROUPS
    gw = MB_DI // MB_GROUPS
    dt_lane0 = 2 * ML_HEADS
    npair = hpg // 2
    cs = range(nchunk)
    rows = [slice(c * CHUNK, (c + 1) * CHUNK) for c in cs]
    cg = [(c, g) for c in cs for g in range(MB_GROUPS)]
    cgj = [(c, g, j) for c, g in cg for j in range(npair)]

    dt = [_softplus(sm_ref[r, :] + bias_ref[...]) for r in rows]
    a_cum = [_cumsum_rows(tril, negA_ref[...] * x) for x in dt]
    a_t = [_transpose_rows(x) for x in a_cum]
    dt_t = [_transpose_rows(x) for x in dt]
    bm = {(c, g): xc_s[rows[c], MB_DI + g * MB_N:MB_DI + (g + 1) * MB_N] for c, g in cg}
    cm = {(c, g): xc_s[rows[c], MB_DI + (MB_GROUPS + g) * MB_N:MB_DI + (MB_GROUPS + g + 1) * MB_N] for c, g in cg}
    cb = {e: _mm(cm[e], bm[e], "bf16", "nt") for e in cg}
    xs = {(g, j): slice(g * gw + j * LANES, g * gw + (j + 1) * LANES) for g in range(MB_GROUPS) for j in range(npair)}
    x2 = {(c, g, j): xc_s[rows[c], xs[g, j]] for c, g, j in cgj}

    sc, ea, ws, el = {}, {}, {}, {}
    for c, g, j in cgj:
        for e in range(2):
            ln = dt_lane0 + g * hpg + 2 * j + e
            a_col = _lane_col(a_cum[c], ln)
            a_row = a_t[c][ln:ln + 1, :]
            a_last = a_row[:, CHUNK - 1:CHUNK]
            seg = jnp.exp(jnp.where(incl, a_col - a_row, -jnp.inf))
            sc[c, g, j, e] = seg * cb[c, g] * dt_t[c][ln:ln + 1, :]
            ea[c, g, j, e] = jnp.exp(a_col)
            ws[c, g, j, e] = jnp.exp(a_last - a_col) * _lane_col(dt[c], ln)
            el[c, g, j, e] = jnp.exp(a_last)
    y0 = {e: _mm(sc[e + (0,)], x2[e], "bf16") for e in cgj}
    y1 = {e: _mm(sc[e + (1,)], x2[e], "bf16") for e in cgj}
    kv = {(c, g, j): _mm(x2[c, g, j] * jnp.where(head0, ws[c, g, j, 0], ws[c, g, j, 1]), bm[c, g], "bf16", "tn")
          for c, g, j in cgj}

    start = {}
    for g in range(MB_GROUPS):
        for j in range(npair):
            si = g * npair + j
            st = st_ref[si]
            for c in cs:
                start[c, g, j] = st
                st = jnp.where(sub0, el[c, g, j, 0], el[c, g, j, 1]) * st + kv[c, g, j]
            st_ref[si] = st

    ycs = {(c, g, j): _mm(cm[c, g], start[c, g, j], "bf16", "nt") for c, g, j in cgj}
    y = {e: jnp.where(head0, y0[e], y1[e]) + jnp.where(head0, ea[e + (0,)], ea[e + (1,)]) * ycs[e]
         + x2[e] * dvec_ref[:, xs[e[1], e[2]]] for e in cgj}
    gs = {g: slice(g * gw, (g + 1) * gw) for g in range(MB_GROUPS)}
    zg = {(c, g): z_ref[rows[c], gs[g]] for c, g in cg}
    yg = {(c, g): jnp.concatenate([y[c, g, j] for j in range(npair)], axis=1) * (zg[c, g] * _sigmoid(zg[c, g]))
          for c, g in cg}
    ms = {e: jnp.mean(yg[e] * yg[e], axis=-1, keepdims=True) for e in cg}
    out = {(c, g): yg[c, g] * lax.rsqrt(ms[c, g] + EPS) * nw_ref[:, gs[g]] for c, g in cg}
    for c, g in cg:
        o_ref[rows[c], gs[g]] = out[c, g].astype(BF16)


def _ssd(p, conv_w, conv_b, bias_row, negA_row, d_row, norm_w, j, rb):
    s = p.shape[0]
    nx = conv_w.shape[2]
    prow = lambda c: pl.BlockSpec((None, 1, c), lambda r: (j, 0, 0))
    return pl.pallas_call(
        functools.partial(_ssd_kernel, nchunk=rb // CHUNK),
        grid=(s // rb,),
        in_specs=[pl.BlockSpec((rb, MB_DI), lambda r: (r, 3)),
                  pl.BlockSpec((rb, nx), lambda r: (r, OD_XBC // nx)),
                  pl.BlockSpec((rb, LANES), lambda r: (r, OD_SMALL // LANES)),
                  pl.BlockSpec((None, MB_CONV, nx), lambda r: (j, 0, 0)),
                  prow(nx), prow(LANES), prow(LANES), prow(MB_DI), prow(MB_DI)],
        out_specs=pl.BlockSpec((rb, MB_DI), lambda r: (r, 0)),
        out_shape=jax.ShapeDtypeStruct((s, MB_DI), BF16),
        scratch_shapes=[pltpu.VMEM((MB_HEADS // 2, 2 * MB_P, MB_N), F32),
                        pltpu.VMEM((rb + 8, nx), F32), pltpu.VMEM((rb, nx), F32)],
        compiler_params=_cparams(("arbitrary",)),
        name="ssd",
    )(p, p, p, conv_w, conv_b, bias_row, negA_row, d_row, norm_w)


def _rows3(x):
    return x.reshape(x.shape[0], 1, -1)


def _odd_in_weight(w):
    wt = jnp.swapaxes(w, 1, 2)
    head_w = 2048 + 1024 + 1024 + 8 + 16
    return jnp.concatenate([wt[:, 0:2048], wt[:, 2056:3080], wt[:, 3080:4104], wt[:, 2048:2056], wt[:, 5640:5656],
                            jnp.zeros((wt.shape[0], OD_XBC - head_w, wt.shape[2]), wt.dtype), wt[:, 4104:5640]],
                           axis=1).astype(BF16)


def _small_rows(*xs):
    row = jnp.concatenate(xs, axis=1)
    return _rows3(jnp.pad(row, ((0, 0), (0, LANES - row.shape[1]))))


def kernel(x, norm_mix, norm_ffn, norm_final, w_in_even, w_out_even, hg_lb_table, hg_norm, rw_mu, rw_w0, rw_w2, rw_a0, rw_a2, rw_g2, rw_k_k, rw_k_a, rw_r_k, rw_ln_w, rw_ln_b, w_in_odd, w_out_odd, ml_i_bias, ml_f_bias, ml_norm, mb_conv_w, mb_conv_b, mb_dt_bias, mb_A_log, mb_D, mb_norm, ffn_w_up, ffn_w_down):
    b, s, d = x.shape
    assert b == 1 and d == D_MODEL and s % TM == 0
    rb_hg, rb_rw, rb_ml, rb_ssd = min(s, 256), min(s, 512), min(s, 256), min(s, 256)

    g_mix, g_ffn = _rows3(norm_mix), _rows3(norm_ffn)
    w_in_e, w_in_o = jnp.swapaxes(w_in_even, 1, 2).astype(BF16), _odd_in_weight(w_in_odd)
    w_out_e, w_out_o = w_out_even.astype(BF16), w_out_odd.astype(BF16)
    w_down = ffn_w_down.astype(BF16)
    rw = _rwkv7_params(rw_mu, rw_w0, rw_w2, rw_a0, rw_a2, rw_g2, rw_k_k, rw_k_a, rw_r_k, rw_ln_w, rw_ln_b)
    zeros8 = jnp.zeros((ml_i_bias.shape[0], 2 * ML_HEADS), F32)
    ml_bias = _small_rows(ml_i_bias, ml_f_bias)
    dt_bias = _small_rows(zeros8, mb_dt_bias)
    neg_a = _small_rows(zeros8, -jnp.exp(mb_A_log.astype(F32)))
    d_rows = _rows3(jnp.repeat(mb_D, MB_P, axis=1))

    h = x.reshape(s, d)
    for layer in range(DEPTH):
        j = layer // 2
        if layer % 2 == 0:
            p = _norm_matmul(h, g_mix, layer, w_in_e, j)
            o_a = _hgrn2(p, hg_lb_table, _rows3(hg_norm), layer, rb_hg)
            o_b = _rwkv7(p, rw, j, rb_rw)
            h = _matmul2_residual(o_a, o_b, w_out_e, j, h)
        else:
            p = _norm_matmul(h, g_mix, layer, w_in_o, j)
            o_a = _mlstm(p, ml_bias, _rows3(ml_norm), j, rb_ml)
            o_b = _ssd(p, mb_conv_w, _rows3(mb_conv_b), dt_bias, neg_a, d_rows, _rows3(mb_norm), j, rb_ssd)
            h = _matmul2_residual(o_a, o_b, w_out_o, j, h)
        act = _norm_swiglu(h, g_ffn, ffn_w_up, layer)
        if layer < DEPTH - 1:
            h = _matmul_residual(act, w_down, layer, h, TN)
        else:
            h = _matmul_residual_norm(act, w_down, layer, h, norm_final, TN)
    return h.reshape(b, s, d)
```

```python
import functools

import jax
import jax.numpy as jnp
from jax import lax
from jax.experimental import pallas as pl
from jax.experimental.pallas import tpu as pltpu

F32 = jnp.float32
BF16 = jnp.bfloat16
HI = lax.Precision.HIGHEST

D_MODEL = 2048
DEPTH = 4
CHUNK = 64
EPS = 1e-6
LANES = 128
SUB = 16

HG_HEADS, HG_D = 8, 128
RW_HEADS, RW_N, RW_W = 16, 64, 1024
RW_GN_EPS = 64e-5
ML_HEADS, ML_DQK, ML_DV = 4, 128, 256
ML_CAP = 15.0
MB_HEADS, MB_P, MB_N, MB_GROUPS, MB_CONV = 16, 64, 128, 2, 4
MB_DI = MB_HEADS * MB_P
D_FF = 5632

NP_ODD = 6144
RW_TAIL = 7168
RW_TAIL_COLS = 288
RW_TAIL_W = 512
OD_SMALL = 4096
OD_XBC = 4608

TM = 1024
TN = 512
TN_IN = 768
VMEM_LIMIT = 56 * 1024 * 1024


def _cparams(sem):
    return pltpu.CompilerParams(dimension_semantics=sem, vmem_limit_bytes=VMEM_LIMIT)


def _dot(a, b, prec=None):
    return jnp.dot(a, b, preferred_element_type=F32, precision=prec)


def _dot_nt(a, b, prec=None):
    return lax.dot_general(a, b, (((1,), (1,)), ((), ())), preferred_element_type=F32, precision=prec)


def _dot_tn(a, b, prec=None):
    return lax.dot_general(a, b, (((0,), (0,)), ((), ())), preferred_element_type=F32, precision=prec)


def _sigmoid(x):
    return 1.0 / (1.0 + jnp.exp(-x))


def _log_sigmoid(x):
    return jnp.minimum(x, 0.0) - jnp.log1p(jnp.exp(-jnp.abs(x)))


def _softplus(x):
    return jnp.maximum(x, 0.0) + jnp.log1p(jnp.exp(-jnp.abs(x)))


def _iota(shape, dim):
    return lax.broadcasted_iota(jnp.int32, shape, dim)


def _tril(n, strict=False):
    r, c = _iota((n, n), 0), _iota((n, n), 1)
    return (r > c) if strict else (r >= c)


def _rms_rows(x, g):
    return (x * lax.rsqrt(jnp.mean(x * x, axis=-1, keepdims=True) + EPS)) * g


def _norm_mm_kernel(x_ref, g_ref, w_ref, o_ref, xn_ref):
    @pl.when(pl.program_id(1) == 0)
    def _():
        xn_ref[...] = _rms_rows(x_ref[...], g_ref[...]).astype(BF16)

    o_ref[...] = _dot_nt(xn_ref[...], w_ref[...])


def _norm_matmul(x, g, gl, wt, wl):
    s, k = x.shape
    n = wt.shape[1]
    tn = TN_IN
    return pl.pallas_call(
        _norm_mm_kernel,
        grid=(s // TM, pl.cdiv(n, tn)),
        in_specs=[pl.BlockSpec((TM, k), lambda i, j: (i, 0)),
                  pl.BlockSpec((None, 1, k), lambda i, j: (gl, 0, 0)),
                  pl.BlockSpec((None, tn, k), lambda i, j: (wl, j, 0))],
        out_specs=pl.BlockSpec((TM, tn), lambda i, j: (i, j)),
        out_shape=jax.ShapeDtypeStruct((s, n), F32),
        scratch_shapes=[pltpu.VMEM((TM, k), BF16)],
        compiler_params=_cparams(("parallel", "arbitrary")),
        name="norm_in_proj",
    )(x, g, wt)


def _norm_swiglu_kernel(x_ref, g_ref, wg_ref, wu_ref, o_ref, xn_ref):
    @pl.when(pl.program_id(1) == 0)
    def _():
        xn_ref[...] = _rms_rows(x_ref[...], g_ref[...]).astype(BF16)

    xn = xn_ref[...]
    gate = _dot(xn, wg_ref[...].astype(BF16))
    up = _dot(xn, wu_ref[...].astype(BF16))
    o_ref[...] = (gate * _sigmoid(gate) * up).astype(BF16)


def _norm_swiglu(x, g, w_up, layer):
    s, k = x.shape
    nj = D_FF // TN
    return pl.pallas_call(
        _norm_swiglu_kernel,
        grid=(s // TM, nj),
        in_specs=[pl.BlockSpec((TM, k), lambda i, j: (i, 0)),
                  pl.BlockSpec((None, 1, k), lambda i, j: (layer, 0, 0)),
                  pl.BlockSpec((None, k, TN), lambda i, j: (layer, 0, j)),
                  pl.BlockSpec((None, k, TN), lambda i, j: (layer, 0, j + nj))],
        out_specs=pl.BlockSpec((TM, TN), lambda i, j: (i, j)),
        out_shape=jax.ShapeDtypeStruct((s, D_FF), BF16),
        scratch_shapes=[pltpu.VMEM((TM, k), BF16)],
        compiler_params=_cparams(("parallel", "arbitrary")),
        name="norm_ffn_up",
    )(x, g, w_up, w_up)


def _mm_res_kernel(x_ref, w_ref, r_ref, o_ref):
    o_ref[...] = r_ref[...] + _dot(x_ref[...], w_ref[...].astype(BF16))


def _matmul_residual(x, w, wl, res, tn):
    s, k = x.shape
    n = w.shape[2]
    return pl.pallas_call(
        _mm_res_kernel,
        grid=(s // TM, n // tn),
        in_specs=[pl.BlockSpec((TM, k), lambda i, j: (i, 0)),
                  pl.BlockSpec((None, k, tn), lambda i, j: (wl, 0, j)),
                  pl.BlockSpec((TM, tn), lambda i, j: (i, j))],
        out_specs=pl.BlockSpec((TM, tn), lambda i, j: (i, j)),
        out_shape=jax.ShapeDtypeStruct((s, n), F32),
        compiler_params=_cparams(("parallel", "parallel")),
        name="proj_residual",
    )(x, w, res)


def _mm_res_norm_kernel(x_ref, w_ref, r_ref, g_ref, o_ref, *, tn):
    j = pl.program_id(1)
    cols = pl.ds(pl.multiple_of(j * tn, tn), tn)
    o_ref[:, cols] = r_ref[...] + _dot(x_ref[...], w_ref[...].astype(BF16))

    @pl.when(j == pl.num_programs(1) - 1)
    def _():
        o_ref[...] = _rms_rows(o_ref[...], g_ref[...])


def _matmul_residual_norm(x, w, wl, res, g, tn):
    s, k = x.shape
    n = w.shape[2]
    return pl.pallas_call(
        functools.partial(_mm_res_norm_kernel, tn=tn),
        grid=(s // TM, n // tn),
        in_specs=[pl.BlockSpec((TM, k), lambda i, j: (i, 0)),
                  pl.BlockSpec((None, k, tn), lambda i, j: (wl, 0, j)),
                  pl.BlockSpec((TM, tn), lambda i, j: (i, j)),
                  pl.BlockSpec((1, n), lambda i, j: (0, 0))],
        out_specs=pl.BlockSpec((TM, n), lambda i, j: (i, 0)),
        out_shape=jax.ShapeDtypeStruct((s, n), F32),
        compiler_params=_cparams(("parallel", "arbitrary")),
        name="proj_residual_norm",
    )(x, w, res, g.reshape(1, n))


def _mm2_res_kernel(x1_ref, x2_ref, w1_ref, w2_ref, r_ref, o_ref):
    o_ref[...] = r_ref[...] + (_dot(x1_ref[...], w1_ref[...]) + _dot(x2_ref[...], w2_ref[...]))


def _matmul2_residual(x1, x2, w, wl, res):
    s, k = x1.shape
    n = w.shape[2]
    assert x2.shape == (s, k) and w.shape[1] == 2 * k
    tm = min(2 * TM, s)
    return pl.pallas_call(
        _mm2_res_kernel,
        grid=(s // tm, n // TN),
        in_specs=[pl.BlockSpec((tm, k), lambda i, j: (i, 0)),
                  pl.BlockSpec((tm, k), lambda i, j: (i, 0)),
                  pl.BlockSpec((None, k, TN), lambda i, j: (wl, 0, j)),
                  pl.BlockSpec((None, k, TN), lambda i, j: (wl, 1, j)),
                  pl.BlockSpec((tm, TN), lambda i, j: (i, j))],
        out_specs=pl.BlockSpec((tm, TN), lambda i, j: (i, j)),
        out_shape=jax.ShapeDtypeStruct((s, n), F32),
        compiler_params=_cparams(("parallel", "parallel")),
        name="out_proj_residual",
    )(x1, x2, w, w, res)


def _rmsnorm_kernel(x_ref, g_ref, o_ref):
    o_ref[...] = _rms_rows(x_ref[...], g_ref[...])


def _rmsnorm(x, g):
    s, k = x.shape
    return pl.pallas_call(
        _rmsnorm_kernel,
        grid=(s // TM,),
        in_specs=[pl.BlockSpec((TM, k), lambda i: (i, 0)), pl.BlockSpec((1, k), lambda i: (0, 0))],
        out_specs=pl.BlockSpec((TM, k), lambda i: (i, 0)),
        out_shape=jax.ShapeDtypeStruct((s, k), F32),
        compiler_params=_cparams(("parallel",)),
        name="final_norm",
    )(x, g.reshape(1, k))


def _hgrn2_kernel(q_ref, f_ref, v_ref, g_ref, lbt_ref, nw_ref, o_ref, st_ref, *, layer, nchunk):
    @pl.when(pl.program_id(1) == 0)
    def _():
        st_ref[...] = jnp.zeros_like(st_ref)

    t = lbt_ref[...]
    e = jnp.exp(t - jnp.max(t, axis=0, keepdims=True))
    sm = e / jnp.sum(e, axis=0, keepdims=True)
    lb = jnp.zeros((1, HG_D), F32)
    for i in range(1, layer + 1):
        lb = lb + sm[i:i + 1, :]
    log_lb = jnp.log(lb)
    log_1m = jnp.log1p(-lb)
    nw = nw_ref[...]

    tril = _tril(CHUNK).astype(BF16)
    rowid = _iota((SUB, HG_D), 0)
    nsub = CHUNK // SUB
    cs = range(nchunk)
    rows = [slice(c * CHUNK, (c + 1) * CHUNK) for c in cs]

    fp = [f_ref[r, :] for r in rows]
    v = [v_ref[r, :] for r in rows]
    b2 = [log_1m + _log_sigmoid(x) for x in fp]
    log_f = [jnp.maximum(log_lb, x) + jnp.log1p(jnp.exp(-jnp.abs(log_lb - x))) for x in b2]
    k = [(1.0 - lb) * _sigmoid(-x) for x in fp]
    q = [x * _sigmoid(x) for x in (q_ref[r, :] for r in rows)]
    bc = [_cumsum_rows(tril, x) for x in log_f]
    b_last = [x[CHUNK - 1:CHUNK, :] for x in bc]
    kv = [_mm(v[c], k[c] * jnp.exp(b_last[c] - bc[c]), "bf16", "tn") for c in cs]
    st = st_ref[...]
    starts = []
    for c in cs:
        starts.append(st)
        st = jnp.exp(b_last[c]) * st + kv[c]
    st_ref[...] = st
    acc = [_mm(q[c] * jnp.exp(bc[c]), starts[c], "bf16", "nt") for c in cs]
    acc = [[a[blk * SUB:(blk + 1) * SUB] for blk in range(nsub)] for a in acc]
    cb = [(c, blk) for blk in range(1, nsub) for c in cs]
    ref_b = {(c, blk): bc[c][blk * SUB - 1:blk * SUB, :] for c, blk in cb}
    qx = {(c, blk): q[c][blk * SUB:(blk + 1) * SUB] * jnp.exp(bc[c][blk * SUB:(blk + 1) * SUB] - ref_b[c, blk])
          for c, blk in cb}
    kx = {(c, blk): k[c][0:blk * SUB] * jnp.exp(ref_b[c, blk] - bc[c][0:blk * SUB]) for c, blk in cb}
    sc = {e: _mm(qx[e], kx[e], "bf16", "nt") for e in cb}
    od = {(c, blk): _mm(sc[c, blk], v[c][0:blk * SUB], "bf16") for c, blk in cb}
    for c, blk in cb:
        acc[c][blk] = acc[c][blk] + od[c, blk]
    tile8 = 8
    acc = [[[a[t0:t0 + tile8] for t0 in range(0, SUB, tile8)] for a in blks] for blks in acc]
    for s in range(SUB):
        first = s // tile8
        for c in cs:
            for blk in range(nsub):
                lo = blk * SUB
                b_s, k_s, v_s = (x[c][lo + s:lo + s + 1, :] for x in (bc, k, v))
                for ti in range(first, SUB // tile8):
                    t0 = lo + ti * tile8
                    d = bc[c][t0:t0 + tile8] - b_s
                    if ti == first and s % tile8 > 0:
                        d = jnp.where(rowid[:tile8] >= s % tile8, d, -jnp.inf)
                    col = jnp.sum(q[c][t0:t0 + tile8] * k_s * jnp.exp(d), axis=1, keepdims=True)
                    acc[c][blk][ti] = acc[c][blk][ti] + col * v_s
    for c in cs:
        o = jnp.concatenate([piece for blk_acc in acc[c] for piece in blk_acc], axis=0)
        y = o * lax.rsqrt(jnp.mean(o * o, axis=-1, keepdims=True) + EPS) * nw
        o_ref[rows[c], :] = (y * _sigmoid(g_ref[rows[c], :])).astype(BF16)


def _hgrn2(p, lb_table, norm_w, layer, rb):
    s = p.shape[0]
    nb = HG_HEADS
    col = lambda off: pl.BlockSpec((rb, HG_D), lambda h, r, off=off: (r, off + h))
    return pl.pallas_call(
        functools.partial(_hgrn2_kernel, layer=layer, nchunk=rb // CHUNK),
        grid=(HG_HEADS, s // rb),
        in_specs=[col(0), col(nb), col(2 * nb), col(3 * nb),
                  pl.BlockSpec((DEPTH, HG_D), lambda h, r: (0, h)),
                  pl.BlockSpec((None, 1, HG_D), lambda h, r: (layer // 2, 0, h))],
        out_specs=pl.BlockSpec((rb, HG_D), lambda h, r: (r, h)),
        out_shape=jax.ShapeDtypeStruct((s, HG_HEADS * HG_D), BF16),
        scratch_shapes=[pltpu.VMEM((HG_D, HG_D), F32)],
        compiler_params=_cparams(("parallel", "arbitrary")),
        name="hgrn2",
    )(p, p, p, p, lb_table, norm_w)


def _shift_lerp(x, prev_row, mu):
    rolled = pltpu.roll(x, 1, 0)
    shifted = jnp.where(_iota(x.shape, 0) == 0, prev_row, rolled)
    return x + (shifted - x) * mu


def _split_bf16(x):
    hi = x.astype(BF16)
    lo = (x - hi.astype(F32)).astype(BF16)
    return hi, lo


def _mm(a, b, mode, kind="nn"):
    dot = {"nn": _dot, "nt": _dot_nt, "tn": _dot_tn}[kind]
    if mode == "hi":
        return dot(a, b, HI)
    if mode == "bf16":
        return dot(a.astype(BF16), b.astype(BF16))
    ah, al = _split_bf16(a)
    bh, bl = _split_bf16(b)
    return dot(ah, bh) + (dot(ah, bl) + dot(al, bh))


def _cumsum_rows(tril_bf16, x):
    hi, lo = _split_bf16(x)
    lo2 = (x - hi.astype(F32) - lo.astype(F32)).astype(BF16)
    return _dot(tril_bf16, hi) + (_dot(tril_bf16, lo) + _dot(tril_bf16, lo2))


def _neumann_inverse(mats, mode):
    n = mats[0].shape[0]
    eye = (_iota((n, n), 0) == _iota((n, n), 1)).astype(F32)
    p = [_mm(x, x, mode) for x in mats]
    m = [eye + x for x in mats]
    for _ in range(4):
        both = [_mm(x, jnp.concatenate([x, y], axis=1), mode) for x, y in zip(p, m)]
        m = [y + b[:, n:] for y, b in zip(m, both)]
        p = [b[:, :n] for b in both]
    return [y + _mm(x, y, mode) for x, y in zip(p, m)]


RW_PREC = dict(score="bf16", neumann="bf16", apply="bf16", state="bf16")


def _rwkv7_kernel(r_ref, k_ref, v_ref, tail_ref,
                  mu_r_ref, mu_k_ref, mu_v_ref, mu_tail_ref,
                  w0_ref, a0_ref, w2_ref, a2_ref, g2_ref, kk_ref, ka_ref, rk_ref, lnw_ref, lnb_ref,
                  o_ref, st_ref, prev_ref, *, nchunk, npair):
    rb = r_ref.shape[0]
    pc = RW_PREC
    wide = npair * LANES

    @pl.when(pl.program_id(1) == 0)
    def _():
        st_ref[...] = jnp.zeros_like(st_ref)
        prev_ref[...] = jnp.zeros_like(prev_ref)

    lane = _iota((1, LANES), 1)
    head0 = lane < RW_N

    def head_sum(x):
        s0 = jnp.sum(jnp.where(head0, x, 0.0), axis=1, keepdims=True)
        s1 = jnp.sum(jnp.where(head0, 0.0, x), axis=1, keepdims=True)
        return jnp.where(head0, s0, s1)

    tw = tail_ref.shape[1]
    tail_raw = jnp.where(_iota((1, tw), 1) < RW_TAIL_COLS, tail_ref[...], 0.0)
    tail = _shift_lerp(tail_raw, prev_ref[3:4, 0:tw], mu_tail_ref[...])
    prev_ref[3:4, 0:tw] = tail_raw[rb - 1:rb]
    wa_t = jnp.tanh(tail[:, 0:LANES]).astype(BF16)
    wa_b = tail[:, 0:LANES].astype(BF16)
    gl_s = _sigmoid(tail[:, LANES:3 * LANES]).astype(BF16)
    ps = range(npair)

    def tile_inputs(p):
        ts = slice(p * LANES, (p + 1) * LANES)
        r_raw, k_raw, v_raw = r_ref[:, ts], k_ref[:, ts], v_ref[:, ts]
        r_p = _shift_lerp(r_raw, prev_ref[0:1, ts], mu_r_ref[:, ts])
        k_p = _shift_lerp(k_raw, prev_ref[1:2, ts], mu_k_ref[:, ts])
        v_p = _shift_lerp(v_raw, prev_ref[2:3, ts], mu_v_ref[:, ts])
        prev_ref[0:1, ts] = r_raw[rb - 1:rb]
        prev_ref[1:2, ts] = k_raw[rb - 1:rb]
        prev_ref[2:3, ts] = v_raw[rb - 1:rb]
        w_log = -_softplus(-(w0_ref[:, ts] + _dot(wa_t, w2_ref[:, ts].astype(BF16)))) - 0.5
        lw_p = -jnp.exp(w_log)
        a_p = _sigmoid(a0_ref[:, ts] + _dot(wa_b, a2_ref[:, ts].astype(BF16)))
        gate_p = _dot(gl_s, g2_ref[:, ts].astype(BF16))
        kk_p = k_p * kk_ref[:, ts]
        kk_p = kk_p / jnp.maximum(jnp.sqrt(head_sum(kk_p * kk_p)), 1e-12)
        k_p = k_p * (1.0 + (a_p - 1.0) * ka_ref[:, ts])
        bonus_p = head_sum(r_p * k_p * rk_ref[:, ts]) * v_p
        return r_p, k_p, v_p, a_p, lw_p, kk_p, bonus_p, gate_p

    head1 = jnp.logical_not(head0)
    bd = (_iota((LANES, LANES), 0) < RW_N) == (_iota((LANES, LANES), 1) < RW_N)
    gs = 2 * CHUNK
    ri, ci = _iota((gs, gs), 0), _iota((gs, gs), 1)
    same = (ri // CHUNK) == (ci // CHUNK)
    strict = jnp.logical_and(same, ri > ci)
    incl = jnp.logical_and(same, ri >= ci)
    tril = _tril(CHUNK).astype(BF16)
    head00 = jnp.concatenate([head0, head0], axis=1)

    def stack_heads(x, masked):
        return jnp.concatenate([jnp.where(head0, x, 0.0), jnp.where(head1, x, 0.0)] if masked else [x, x], axis=0)

    def unstack_heads(x):
        return jnp.where(head0 if x.shape[1] == LANES else head00, x[:CHUNK], x[CHUNK:])

    cs = range(nchunk)
    grp = [(p, c) for c in cs for p in ps]
    rows = [slice(c * CHUNK, (c + 1) * CHUNK) for c in cs]
    v, g, rt, at, bt, kt, a_ab, rab, aakv, rakv, bonus, gate = ({} for _ in range(12))
    for p in ps:
        r_p, k_p, v_p, a_p, lw_p, kk_p, bonus[p], gate[p] = tile_inputs(p)
        mine = [(p, c) for c in cs]
        cut = lambda x: {(p, c): x[rows[c]] for c in cs}
        r_c, k_c, a_c, lw_c, kk_c = cut(r_p), cut(k_p), cut(a_p), cut(lw_p), cut(kk_p)
        v.update(cut(v_p))
        g.update({e: _cumsum_rows(tril, lw_c[e]) for e in mine})
        ieg = {e: jnp.exp(-g[e]) for e in mine}
        rt.update({e: r_c[e] * jnp.exp(g[e]) for e in mine})
        at.update({e: -kk_c[e] * jnp.exp(g[e] - lw_c[e]) for e in mine})
        bt.update({e: kk_c[e] * a_c[e] * ieg[e] for e in mine})
        kt.update({e: k_c[e] * ieg[e] for e in mine})
        l_a = {e: stack_heads(at[e], True) for e in mine}
        l_r = {e: stack_heads(rt[e], True) for e in mine}
        r_bk = {e: jnp.concatenate([stack_heads(bt[e], True), stack_heads(kt[e], True)], axis=0) for e in mine}
        s_a = {e: _mm(l_a[e], r_bk[e], pc["score"], "nt") for e in mine}
        s_r = {e: _mm(l_r[e], r_bk[e], pc["score"], "nt") for e in mine}
        a_ab.update({e: jnp.where(strict, s_a[e][:, :gs], 0.0) for e in mine})
        rab.update({e: jnp.where(incl, s_r[e][:, :gs], 0.0) for e in mine})
        v_st = {e: stack_heads(v[e], False) for e in mine}
        aakv.update({e: _mm(jnp.where(strict, s_a[e][:, gs:], 0.0), v_st[e], pc["apply"]) for e in mine})
        rakv.update({e: unstack_heads(_mm(jnp.where(incl, s_r[e][:, gs:], 0.0), v_st[e], pc["apply"])) for e in mine})
    t_inv = dict(zip(grp, _neumann_inverse([a_ab[e] for e in grp], pc["neumann"])))
    xs = {e: unstack_heads(_mm(t_inv[e], jnp.concatenate([stack_heads(at[e], False), aakv[e]], axis=1), pc["apply"]))
          for e in grp}
    ta = {e: xs[e][:, :LANES] for e in grp}
    tav = {e: xs[e][:, LANES:] for e in grp}
    egl = {e: jnp.exp(g[e][CHUNK - 1:CHUNK, :]) for e in grp}
    bte = {e: bt[e] * egl[e] for e in grp}
    m_mat = {e: jnp.where(bd, _mm(ta[e], bte[e], pc["state"], "tn"), 0.0) for e in grp}
    c_mat = {e: jnp.where(bd, _mm(jnp.concatenate([tav[e], v[e]], axis=0),
                                  jnp.concatenate([bte[e], kt[e] * egl[e]], axis=0), pc["state"], "tn"), 0.0)
             for e in grp}

    st = [st_ref[p] for p in ps]
    start = {}
    for c in cs:
        for p in ps:
            start[p, c] = st[p]
            st[p] = st[p] * egl[p, c] + _mm(st[p], m_mat[p, c], pc["state"]) + c_mat[p, c]
    for p in ps:
        st_ref[p] = st[p]
    u = {e: _mm(ta[e], start[e], pc["state"], "nt") + tav[e] for e in grp}
    o_in = {e: _mm(rt[e], start[e], pc["state"], "nt") for e in grp}
    y = {e: _mm(rab[e], stack_heads(u[e], True), pc["apply"]) for e in grp}
    o = {e: o_in[e] + (y[e][:CHUNK] + y[e][CHUNK:]) + rakv[e] for e in grp}

    for p in ps:
        ts = slice(p * LANES, (p + 1) * LANES)
        o_p = jnp.concatenate([o[p, c] for c in cs], axis=0)
        mu = head_sum(o_p) * (1.0 / RW_N)
        d = o_p - mu
        var = head_sum(d * d) * (1.0 / RW_N)
        y_p = d * lax.rsqrt(var + RW_GN_EPS) * lnw_ref[:, ts] + lnb_ref[:, ts]
        o_ref[:, ts] = ((y_p + bonus[p]) * gate[p]).astype(BF16)


def _rwkv7_params(mu, w0, w2, a0, a2, g2, k_k, k_a, r_k, ln_w, ln_b):
    n = mu.shape[0]
    row = lambda x: x.reshape(n, 1, -1)
    assert 2 * w2.shape[1] == LANES and 2 * a2.shape[1] == LANES
    mu_p = jnp.pad(mu, ((0, 0), (0, 3 * RW_W + RW_TAIL_W - mu.shape[1])))
    return dict(mu=row(mu_p), w0=row(w0), a0=row(a0),
                w2=jnp.concatenate([w2, jnp.zeros_like(w2)], axis=1),
                a2=jnp.concatenate([jnp.zeros_like(a2), a2], axis=1),
                g2=jnp.pad(g2, ((0, 0), (0, 2 * LANES - g2.shape[1]), (0, 0))),
                k_k=row(k_k), k_a=row(k_a), r_k=row(r_k), ln_w=row(ln_w), ln_b=row(ln_b))


RW_PAIRS = 4


def _rwkv7(p, prm, j, rb):
    s = p.shape[0]
    wide = RW_PAIRS * LANES
    base = 4 * HG_HEADS * LANES // wide
    nb = RW_W // wide
    col = lambda off: pl.BlockSpec((rb, wide), lambda h, r, off=off: (r, off + h))
    par = pl.BlockSpec((None, 1, wide), lambda h, r: (j, 0, h))
    par_off = lambda off: pl.BlockSpec((None, 1, wide), lambda h, r, off=off: (j, 0, off + h))
    lora = lambda rows: pl.BlockSpec((None, rows, wide), lambda h, r: (j, 0, h))
    assert RW_TAIL % RW_TAIL_W == 0 and (3 * RW_W) % RW_TAIL_W == 0
    return pl.pallas_call(
        functools.partial(_rwkv7_kernel, nchunk=rb // CHUNK, npair=RW_PAIRS),
        grid=(nb, s // rb),
        in_specs=[col(base), col(base + nb), col(base + 2 * nb),
                  pl.BlockSpec((rb, RW_TAIL_W), lambda h, r: (r, RW_TAIL // RW_TAIL_W)),
                  par, par_off(nb), par_off(2 * nb),
                  pl.BlockSpec((None, 1, RW_TAIL_W), lambda h, r: (j, 0, 3 * RW_W // RW_TAIL_W)),
                  par, par, lora(LANES), lora(LANES), lora(2 * LANES),
                  par, par, par, par, par],
        out_specs=pl.BlockSpec((rb, wide), lambda h, r: (r, h)),
        out_shape=jax.ShapeDtypeStruct((s, RW_W), BF16),
        scratch_shapes=[pltpu.VMEM((RW_PAIRS, LANES, LANES), F32), pltpu.VMEM((8, max(wide, RW_TAIL_W)), F32)],
        compiler_params=_cparams(("parallel", "arbitrary")),
        name="rwkv7",
    )(p, p, p, p, prm["mu"], prm["mu"], prm["mu"], prm["mu"],
      prm["w0"], prm["a0"], prm["w2"], prm["a2"], prm["g2"],
      prm["k_k"], prm["k_a"], prm["r_k"], prm["ln_w"], prm["ln_b"])


def _lane_col(x, idx):
    return jnp.sum(jnp.where(_iota(x.shape, 1) == idx, x, 0.0), axis=1, keepdims=True)


def _transpose_rows(x):
    eye = (_iota((LANES, LANES), 0) == _iota((LANES, LANES), 1)).astype(BF16)
    hi, lo = _split_bf16(x)
    lo2 = (x - hi.astype(F32) - lo.astype(F32)).astype(BF16)
    return _dot_nt(eye, hi) + (_dot_nt(eye, lo) + _dot_nt(eye, lo2))


def _mlstm_kernel(q_ref, k_ref, v_ref, og_ref, sm_ref, bias_ref, nw_ref, o_ref, c_ref, n_ref, m_ref, *, nchunk):
    @pl.when(pl.program_id(0) == 0)
    def _():
        c_ref[...] = jnp.zeros_like(c_ref)
        n_ref[...] = jnp.zeros_like(n_ref)
        m_ref[...] = jnp.zeros_like(m_ref)

    incl = _tril(CHUNK)
    tril = incl.astype(BF16)
    lane = _iota((CHUNK, LANES), 1)
    is_f = jnp.logical_and(lane >= ML_HEADS, lane < 2 * ML_HEADS)
    scale = ML_DQK ** -0.5
    cs = range(nchunk)
    hs = range(ML_HEADS)
    rows = [slice(c * CHUNK, (c + 1) * CHUNK) for c in cs]
    ch = [(c, h) for c in cs for h in hs]

    pre = [sm_ref[r, :] + bias_ref[...] for r in rows]
    cap = [ML_CAP * jnp.tanh(x / ML_CAP) for x in pre]
    x = [jnp.where(is_f, _log_sigmoid(y), y) for y in cap]
    cum = [_cumsum_rows(tril, y) for y in x]
    x_t = [_transpose_rows(y) for y in x]
    cum_t = [_transpose_rows(y) for y in cum]

    q = {(c, h): q_ref[rows[c], h * ML_DQK:(h + 1) * ML_DQK] * scale for c, h in ch}
    k = {(c, h): k_ref[rows[c], h * ML_DQK:(h + 1) * ML_DQK] for c, h in ch}
    v = {(c, h): v_ref[rows[c], h * ML_DV:(h + 1) * ML_DV] for c, h in ch}
    b_col = {(c, h): _lane_col(cum[c], ML_HEADS + h) for c, h in ch}
    i_col = {(c, h): _lane_col(x[c], h) for c, h in ch}
    b_row = {(c, h): cum_t[c][ML_HEADS + h:ML_HEADS + h + 1, :] for c, h in ch}
    i_row = {(c, h): x_t[c][h:h + 1, :] for c, h in ch}
    qk = {e: _mm(q[e], k[e], "bf16", "nt") for e in ch}
    dmat = {e: jnp.where(incl, b_col[e] - b_row[e] + i_row[e], -jnp.inf) for e in ch}
    dmax = {e: jnp.max(dmat[e], axis=1, keepdims=True) for e in ch}
    b_last = {e: b_row[e][:, CHUNK - 1:CHUNK] for e in ch}
    src_row = {e: b_last[e] - b_row[e] + i_row[e] for e in ch}
    src_col = {e: b_last[e] - b_col[e] + i_col[e] for e in ch}
    src_max = {e: jnp.max(src_row[e], axis=1, keepdims=True) for e in ch}

    m_prev, m_new = {}, {}
    for h in hs:
        m = m_ref[h:h + 1, 0:1]
        for c in cs:
            m_prev[c, h] = m
            m = jnp.maximum(b_last[c, h] + m, src_max[c, h])
            m_new[c, h] = m
        m_ref[h:h + 1, :] = jnp.broadcast_to(m, (1, LANES))
    inter = {e: b_col[e] + m_prev[e] for e in ch}
    m_t = {e: jnp.maximum(inter[e], dmax[e]) for e in ch}
    w_inter = {e: jnp.exp(inter[e] - m_t[e]) for e in ch}
    pmat = {e: jnp.exp(dmat[e] - m_t[e]) * qk[e] for e in ch}
    num = {e: _mm(pmat[e], v[e], "bf16") for e in ch}
    den = {e: jnp.sum(pmat[e], axis=1, keepdims=True) for e in ch}
    decay = {e: jnp.exp(b_last[e] + m_prev[e] - m_new[e]) for e in ch}
    wk = {e: jnp.exp(src_col[e] - m_new[e]) * k[e] for e in ch}
    kv = {e: _mm(wk[e], v[e], "bf16", "tn") for e in ch}
    ksum = {e: jnp.sum(wk[e], axis=0, keepdims=True) for e in ch}

    c_start, n_start = {}, {}
    for h in hs:
        c_mat, n_row = c_ref[h], n_ref[h:h + 1, :]
        for c in cs:
            c_start[c, h], n_start[c, h] = c_mat, n_row
            c_mat = decay[c, h] * c_mat + kv[c, h]
            n_row = decay[c, h] * n_row + ksum[c, h]
        c_ref[h] = c_mat
        n_ref[h:h + 1, :] = n_row

    qc = {e: _mm(q[e], c_start[e], "bf16") for e in ch}
    vs = {h: slice(h * ML_DV, (h + 1) * ML_DV) for h in hs}
    qn = {e: jnp.sum(q[e] * n_start[e], axis=1, keepdims=True) for e in ch}
    num = {e: num[e] + w_inter[e] * qc[e] for e in ch}
    den = {e: den[e] + w_inter[e] * qn[e] for e in ch}
    h_out = {e: num[e] / jnp.maximum(jnp.abs(den[e]), jnp.exp(-m_t[e])) for e in ch}
    ms = {e: jnp.mean(h_out[e] * h_out[e], axis=-1, keepdims=True) for e in ch}
    gate = {(c, h): _sigmoid(og_ref[rows[c], vs[h]]) for c, h in ch}
    y = {(c, h): h_out[c, h] * lax.rsqrt(ms[c, h] + EPS) * nw_ref[:, vs[h]] * gate[c, h] for c, h in ch}
    for c, h in ch:
        o_ref[rows[c], vs[h]] = y[c, h].astype(BF16)


def _mlstm(p, bias_row, norm_w, j, rb):
    s = p.shape[0]
    nq = ML_HEADS * ML_DQK
    nv = ML_HEADS * ML_DV
    return pl.pallas_call(
        functools.partial(_mlstm_kernel, nchunk=rb // CHUNK),
        grid=(s // rb,),
        in_specs=[pl.BlockSpec((rb, nq), lambda r: (r, 0)),
                  pl.BlockSpec((rb, nq), lambda r: (r, 1)),
                  pl.BlockSpec((rb, nv), lambda r: (r, 1)),
                  pl.BlockSpec((rb, nv), lambda r: (r, 2)),
                  pl.BlockSpec((rb, LANES), lambda r: (r, OD_SMALL // LANES)),
                  pl.BlockSpec((None, 1, LANES), lambda r: (j, 0, 0)),
                  pl.BlockSpec((None, 1, nv), lambda r: (j, 0, 0))],
        out_specs=pl.BlockSpec((rb, nv), lambda r: (r, 0)),
        out_shape=jax.ShapeDtypeStruct((s, nv), BF16),
        scratch_shapes=[pltpu.VMEM((ML_HEADS, ML_DQK, ML_DV), F32),
                        pltpu.VMEM((8, ML_DQK), F32), pltpu.VMEM((8, LANES), F32)],
        compiler_params=_cparams(("arbitrary",)),
        name="mlstm",
    )(p, p, p, p, p, bias_row, norm_w)


def _ssd_kernel(z_ref, xbc_ref, sm_ref, cw_ref, cb_ref, bias_ref, negA_ref, dvec_ref, nw_ref, o_ref,
                st_ref, xin_s, xc_s, *, nchunk):
    rb = z_ref.shape[0]
    pad = 8

    @pl.when(pl.program_id(0) == 0)
    def _():
        st_ref[...] = jnp.zeros_like(st_ref)
        xin_s[rb:rb + pad, :] = jnp.zeros((pad, xin_s.shape[1]), F32)

    xin_s[0:pad, :] = xin_s[rb:rb + pad, :]
    xin_s[pad:rb + pad, :] = xbc_ref[...]
    acc = cb_ref[...] + xin_s[pad:rb + pad, :] * cw_ref[MB_CONV - 1:MB_CONV, :]
    for j in range(MB_CONV - 1):
        lo = pad - (MB_CONV - 1 - j)
        acc = acc + xin_s[lo:lo + rb, :] * cw_ref[j:j + 1, :]
    xc_s[...] = acc * _sigmoid(acc)

    incl = _tril(CHUNK)
    tril = incl.astype(BF16)
    lane = _iota((1, LANES), 1)
    head0 = lane < MB_P
    sub0 = _iota((LANES, 1), 0) < MB_P
    hpg = MB_HEADS // MB_GROUPS
    gw = MB_DI // MB_GROUPS
    dt_lane0 = 2 * ML_HEADS
    npair = hpg // 2
    cs = range(nchunk)
    rows = [slice(c * CHUNK, (c + 1) * CHUNK) for c in cs]
    cg = [(c, g) for c in cs for g in range(MB_GROUPS)]
    cgj = [(c, g, j) for c, g in cg for j in range(npair)]

    dt = [_softplus(sm_ref[r, :] + bias_ref[...]) for r in rows]
    a_cum = [_cumsum_rows(tril, negA_ref[...] * x) for x in dt]
    a_t = [_transpose_rows(x) for x in a_cum]
    dt_t = [_transpose_rows(x) for x in dt]
    bm = {(c, g): xc_s[rows[c], MB_DI + g * MB_N:MB_DI + (g + 1) * MB_N] for c, g in cg}
    cm = {(c, g): xc_s[rows[c], MB_DI + (MB_GROUPS + g) * MB_N:MB_DI + (MB_GROUPS + g + 1) * MB_N] for c, g in cg}
    cb = {e: _mm(cm[e], bm[e], "bf16", "nt") for e in cg}
    xs = {(g, j): slice(g * gw + j * LANES, g * gw + (j + 1) * LANES) for g in range(MB_GROUPS) for j in range(npair)}
    x2 = {(c, g, j): xc_s[rows[c], xs[g, j]] for c, g, j in cgj}

    sc, ea, ws, el = {}, {}, {}, {}
    for c, g, j in cgj:
        for e in range(2):
            ln = dt_lane0 + g * hpg + 2 * j + e
            a_col = _lane_col(a_cum[c], ln)
            a_row = a_t[c][ln:ln + 1, :]
            a_last = a_row[:, CHUNK - 1:CHUNK]
            seg = jnp.exp(jnp.where(incl, a_col - a_row, -jnp.inf))
            sc[c, g, j, e] = seg * cb[c, g] * dt_t[c][ln:ln + 1, :]
            ea[c, g, j, e] = jnp.exp(a_col)
            ws[c, g, j, e] = jnp.exp(a_last - a_col) * _lane_col(dt[c], ln)
            el[c, g, j, e] = jnp.exp(a_last)
    y0 = {e: _mm(sc[e + (0,)], x2[e], "bf16") for e in cgj}
    y1 = {e: _mm(sc[e + (1,)], x2[e], "bf16") for e in cgj}
    kv = {(c, g, j): _mm(x2[c, g, j] * jnp.where(head0, ws[c, g, j, 0], ws[c, g, j, 1]), bm[c, g], "bf16", "tn")
          for c, g, j in cgj}

    start = {}
    for g in range(MB_GROUPS):
        for j in range(npair):
            si = g * npair + j
            st = st_ref[si]
            for c in cs:
                start[c, g, j] = st
                st = jnp.where(sub0, el[c, g, j, 0], el[c, g, j, 1]) * st + kv[c, g, j]
            st_ref[si] = st

    ycs = {(c, g, j): _mm(cm[c, g], start[c, g, j], "bf16", "nt") for c, g, j in cgj}
    y = {e: jnp.where(head0, y0[e], y1[e]) + jnp.where(head0, ea[e + (0,)], ea[e + (1,)]) * ycs[e]
         + x2[e] * dvec_ref[:, xs[e[1], e[2]]] for e in cgj}
    gs = {g: slice(g * gw, (g + 1) * gw) for g in range(MB_GROUPS)}
    zg = {(c, g): z_ref[rows[c], gs[g]] for c, g in cg}
    yg = {(c, g): jnp.concatenate([y[c, g, j] for j in range(npair)], axis=1) * (zg[c, g] * _sigmoid(zg[c, g]))
          for c, g in cg}
    ms = {e: jnp.mean(yg[e] * yg[e], axis=-1, keepdims=True) for e in cg}
    out = {(c, g): yg[c, g] * lax.rsqrt(ms[c, g] + EPS) * nw_ref[:, gs[g]] for c, g in cg}
    for c, g in cg:
        o_ref[rows[c], gs[g]] = out[c, g].astype(BF16)


def _ssd(p, conv_w, conv_b, bias_row, negA_row, d_row, norm_w, j, rb):
    s = p.shape[0]
    nx = conv_w.shape[2]
    prow = lambda c: pl.BlockSpec((None, 1, c), lambda r: (j, 0, 0))
    return pl.pallas_call(
        functools.partial(_ssd_kernel, nchunk=rb // CHUNK),
        grid=(s // rb,),
        in_specs=[pl.BlockSpec((rb, MB_DI), lambda r: (r, 3)),
                  pl.BlockSpec((rb, nx), lambda r: (r, OD_XBC // nx)),
                  pl.BlockSpec((rb, LANES), lambda r: (r, OD_SMALL // LANES)),
                  pl.BlockSpec((None, MB_CONV, nx), lambda r: (j, 0, 0)),
                  prow(nx), prow(LANES), prow(LANES), prow(MB_DI), prow(MB_DI)],
        out_specs=pl.BlockSpec((rb, MB_DI), lambda r: (r, 0)),
        out_shape=jax.ShapeDtypeStruct((s, MB_DI), BF16),
        scratch_shapes=[pltpu.VMEM((MB_HEADS // 2, 2 * MB_P, MB_N), F32),
                        pltpu.VMEM((rb + 8, nx), F32), pltpu.VMEM((rb, nx), F32)],
        compiler_params=_cparams(("arbitrary",)),
        name="ssd",
    )(p, p, p, conv_w, conv_b, bias_row, negA_row, d_row, norm_w)


def _rows3(x):
    return x.reshape(x.shape[0], 1, -1)


def _odd_in_weight(w):
    wt = jnp.swapaxes(w, 1, 2)
    head_w = 2048 + 1024 + 1024 + 8 + 16
    return jnp.concatenate([wt[:, 0:2048], wt[:, 2056:3080], wt[:, 3080:4104], wt[:, 2048:2056], wt[:, 5640:5656],
                            jnp.zeros((wt.shape[0], OD_XBC - head_w, wt.shape[2]), wt.dtype), wt[:, 4104:5640]],
                           axis=1).astype(BF16)


def _small_rows(*xs):
    row = jnp.concatenate(xs, axis=1)
    return _rows3(jnp.pad(row, ((0, 0), (0, LANES - row.shape[1]))))


def kernel(x, norm_mix, norm_ffn, norm_final, w_in_even, w_out_even, hg_lb_table, hg_norm, rw_mu, rw_w0, rw_w2, rw_a0, rw_a2, rw_g2, rw_k_k, rw_k_a, rw_r_k, rw_ln_w, rw_ln_b, w_in_odd, w_out_odd, ml_i_bias, ml_f_bias, ml_norm, mb_conv_w, mb_conv_b, mb_dt_bias, mb_A_log, mb_D, mb_norm, ffn_w_up, ffn_w_down):
    b, s, d = x.shape
    assert b == 1 and d == D_MODEL and s % TM == 0
    rb_hg, rb_rw, rb_ml, rb_ssd = min(s, 512), min(s, 512), min(s, 512), min(s, 256)

    g_mix, g_ffn = _rows3(norm_mix), _rows3(norm_ffn)
    w_in_e, w_in_o = jnp.swapaxes(w_in_even, 1, 2).astype(BF16), _odd_in_weight(w_in_odd)
    w_out_e, w_out_o = w_out_even.astype(BF16), w_out_odd.astype(BF16)
    w_down = ffn_w_down.astype(BF16)
    rw = _rwkv7_params(rw_mu, rw_w0, rw_w2, rw_a0, rw_a2, rw_g2, rw_k_k, rw_k_a, rw_r_k, rw_ln_w, rw_ln_b)
    zeros8 = jnp.zeros((ml_i_bias.shape[0], 2 * ML_HEADS), F32)
    ml_bias = _small_rows(ml_i_bias, ml_f_bias)
    dt_bias = _small_rows(zeros8, mb_dt_bias)
    neg_a = _small_rows(zeros8, -jnp.exp(mb_A_log.astype(F32)))
    d_rows = _rows3(jnp.repeat(mb_D, MB_P, axis=1))

    h = x.reshape(s, d)
    for layer in range(DEPTH):
        j = layer // 2
        if layer % 2 == 0:
            p = _norm_matmul(h, g_mix, layer, w_in_e, j)
            o_a = _hgrn2(p, hg_lb_table, _rows3(hg_norm), layer, rb_hg)
            o_b = _rwkv7(p, rw, j, rb_rw)
            h = _matmul2_residual(o_a, o_b, w_out_e, j, h)
        else:
            p = _norm_matmul(h, g_mix, layer, w_in_o, j)
            o_a = _mlstm(p, ml_bias, _rows3(ml_norm), j, rb_ml)
            o_b = _ssd(p, mb_conv_w, _rows3(mb_conv_b), dt_bias, neg_a, d_rows, _rows3(mb_norm), j, rb_ssd)
            h = _matmul2_residual(o_a, o_b, w_out_o, j, h)
        act = _norm_swiglu(h, g_ffn, ffn_w_up, layer)
        if layer < DEPTH - 1:
            h = _matmul_residual(act, w_down, layer, h, TN)
        else:
            h = _matmul_residual_norm(act, w_down, layer, h, norm_final, TN)
    return h.reshape(b, s, d)
```
